```python
import math
import jax
import jax.numpy as jnp
from jax import lax
import numpy as np

D_MODEL = 1024
BATCH = 16
SEQ = 4096
DEPTH = 4
DEC_BATCH = 32
DEC_SEQ = 32
PAST_LEN = 2048

CHUNK = 64
Q_BLOCK = 128
EPS = 1e-6
ROPE_BASE = 10000.0
F32 = jnp.float32
N_AB = DEPTH // 2
N_CD = DEPTH // 2
RET_HEADS = 4
RET_DK = 64
RET_DV = 128
SB_HEADS = 8
SB_DH = 64
BAND_HEADS = 8
BAND_DH = 64
BAND_CHUNKS = 8
BAND_WINDOW = BAND_CHUNKS * CHUNK
REL_CLIP = 128
SSD_HEADS = 8
SSD_P = 64
SSD_GROUPS = 2
SSD_N = 128
SSD_CONV = 4
SSD_INNER = SSD_HEADS * SSD_P
SSD_CONV_DIM = SSD_INNER + 2 * SSD_GROUPS * SSD_N
D_FF = 2816
RET_QK_W = RET_HEADS * RET_DK
RET_V_W = RET_HEADS * RET_DV
SB_W = SB_HEADS * SB_DH
BAND_W = BAND_HEADS * BAND_DH
AB_IN = 2 * RET_QK_W + 2 * RET_V_W + 3 * SB_W
AB_OUT = RET_V_W + SB_W
CD_IN = 3 * BAND_W + SSD_INNER + SSD_CONV_DIM + SSD_HEADS
CD_OUT = BAND_W + SSD_INNER

kernel_name = 'hybrid_streaming_encoder_step'


def rms_norm(x, g):
    xf = x.astype(F32)
    y = xf * lax.rsqrt(jnp.mean(xf * xf, axis=-1, keepdims=True) + EPS)
    return (y * g.astype(F32)).astype(x.dtype)


def swiglu(x, w_in, w_out):
    gate, up = jnp.split(x @ w_in, 2, axis=-1)
    return (jax.nn.silu(gate) * up) @ w_out


def rotary(x, pos):
    half = x.shape[-1] // 2
    inv = jnp.power(ROPE_BASE, -jnp.arange(half, dtype=F32) / half)
    ang = pos.astype(F32)[:, None] * inv[None, :]
    cos = jnp.cos(ang)[None, :, None, :]
    sin = jnp.sin(ang)[None, :, None, :]
    xf = x.astype(F32)
    x1, x2 = xf[..., :half], xf[..., half:]
    return jnp.concatenate([x1 * cos - x2 * sin, x1 * sin + x2 * cos], axis=-1).astype(x.dtype)


def retention(q, k, v, state0):
    bsz, L, H, _ = q.shape
    c = min(CHUNK, L)
    n = L // c
    lg = jnp.log1p(-jnp.exp2(-5.0 - jnp.arange(H, dtype=F32)))
    idx = jnp.arange(c, dtype=F32)
    diff = idx[:, None] - idx[None, :]
    dmask = jnp.where(diff >= 0, jnp.exp(lg[:, None, None] * jnp.maximum(diff, 0.0)), 0.0)
    q_dec = jnp.exp(lg[None, :] * (idx[:, None] + 1.0))
    k_dec = jnp.exp(lg[None, :] * (c - 1.0 - idx[:, None]))
    c_dec = jnp.exp(lg * c)

    def chunks(t):
        return jnp.moveaxis(t.astype(F32).reshape(bsz, n, c, H, t.shape[-1]), 1, 0)

    def step(S, inp):
        qi, ki, vi = inp
        att = jnp.einsum('bihd,bjhd->bhij', qi, ki) * dmask
        o = jnp.einsum('bhij,bjhe->bihe', att, vi) + jnp.einsum('bihd,bhde->bihe', qi, S) * q_dec[None, :, :, None]
        S = S * c_dec[None, :, None, None] + jnp.einsum('bjhd,bjhe->bhde', ki * k_dec[None, :, :, None], vi)
        return S, o

    S, o = lax.scan(step, state0.astype(F32), (chunks(q), chunks(k), chunks(v)))
    return jnp.moveaxis(o, 0, 1).reshape(bsz, L, H, -1), S


def stick_breaking(q, k, v, q_pos, k_pos):
    z = jnp.einsum('bqhd,bkhd->bhqk', q, k).astype(F32) * (q.shape[-1] ** -0.5)
    mask = k_pos[None, :] < q_pos[:, None]
    log_1mb = jnp.where(mask, jax.nn.log_sigmoid(-z), 0.0)
    later = lax.cumsum(log_1mb, axis=3, reverse=True) - log_1mb
    w = jnp.where(mask, jnp.exp(jax.nn.log_sigmoid(z) + later), 0.0)
    return jnp.einsum('bhqk,bkhd->bqhd', w, v.astype(F32))


def stick_breaking_blocks(q, k, v):
    bsz, L, H, d = q.shape
    pos = jnp.arange(L, dtype=jnp.int32)

    def blk(i):
        s0 = i * Q_BLOCK
        qb = lax.dynamic_slice_in_dim(q, s0, Q_BLOCK, axis=1)
        qp = lax.dynamic_slice_in_dim(pos, s0, Q_BLOCK)
        return stick_breaking(qb, k, v, qp, pos)

    out = lax.map(blk, jnp.arange(L // Q_BLOCK))
    return jnp.moveaxis(out, 0, 1).reshape(bsz, L, H, d)


def band_attend(q, k, v, q_pos, k_pos, bias_table):
    s = jnp.einsum('bqhd,bkhd->bhqk', q, k).astype(F32) * (q.shape[-1] ** -0.5)
    rel = jnp.clip(q_pos[:, None] - k_pos[None, :], -REL_CLIP, REL_CLIP) + REL_CLIP
    s = s + bias_table.astype(F32)[:, rel][None]
    qc = jnp.floor_divide(q_pos, CHUNK)[:, None]
    kc = jnp.floor_divide(k_pos, CHUNK)[None, :]
    mask = (k_pos >= 0)[None, :] & (kc <= qc) & (kc >= qc - BAND_CHUNKS)
    p = jax.nn.softmax(jnp.where(mask, s, -1e30), axis=-1)
    return jnp.einsum('bhqk,bkhd->bqhd', p, v.astype(F32))


def band_prompt(q, k, v, bias_table):
    bsz, L, H, d = q.shape
    pad = jnp.zeros((bsz, BAND_WINDOW, H, d), k.dtype)
    kp = jnp.concatenate([pad, k], axis=1)
    vp = jnp.concatenate([pad, v], axis=1)
    pos = jnp.arange(L, dtype=jnp.int32)
    kpos = jnp.arange(-BAND_WINDOW, L, dtype=jnp.int32)
    span = BAND_WINDOW + CHUNK

    def one(i):
        s0 = i * CHUNK
        return band_attend(lax.dynamic_slice_in_dim(q, s0, CHUNK, axis=1),
                           lax.dynamic_slice_in_dim(kp, s0, span, axis=1),
                           lax.dynamic_slice_in_dim(vp, s0, span, axis=1),
                           lax.dynamic_slice_in_dim(pos, s0, CHUNK),
                           lax.dynamic_slice_in_dim(kpos, s0, span), bias_table)

    out = lax.map(one, jnp.arange(L // CHUNK))
    return jnp.moveaxis(out, 0, 1).reshape(bsz, L, H, d)


def causal_conv(u, buf, w, b):
    L = u.shape[1]
    full = jnp.concatenate([buf, u], axis=1)
    acc = b.astype(F32) + sum(full[:, i:i + L].astype(F32) * w[i].astype(F32) for i in range(SSD_CONV))
    return jax.nn.silu(acc).astype(u.dtype), full[:, L:]


def ssd(x, dt, a, bm, cm, state0):
    bsz, L, H, P = x.shape
    G, N = bm.shape[2], bm.shape[3]
    R = H // G
    c = min(CHUNK, L)
    n = L // c
    xd = x.astype(F32) * dt[..., None]
    da = dt * a

    def chunks(t):
        return jnp.moveaxis(t.astype(F32).reshape((bsz, n, c) + t.shape[2:]), 1, 0)

    causal = jnp.tril(jnp.ones((c, c), dtype=bool))[None, :, :, None]

    def step(S, inp):
        xi, dai, bi, ci = inp
        cum = jnp.cumsum(dai, axis=1)
        seg = jnp.exp(jnp.where(causal, cum[:, :, None, :] - cum[:, None, :, :], -jnp.inf))
        cb = jnp.einsum('bign,bjgn->bgij', ci, bi)
        xg = xi.reshape(bsz, c, G, R, P)
        y = jnp.einsum('bgij,bijgr,bjgrp->bigrp', cb, seg.reshape(bsz, c, c, G, R), xg)
        y = y + jnp.einsum('bign,bgrpn->bigrp', ci, S.reshape(bsz, G, R, P, N)) * jnp.exp(cum).reshape(bsz, c, G, R)[..., None]
        last = cum[:, -1]
        w_rem = jnp.exp(last[:, None, :] - cum).reshape(bsz, c, G, R)
        S = S * jnp.exp(last)[:, :, None, None] + jnp.einsum('bjgrp,bjgn->bgrpn', xg * w_rem[..., None], bi).reshape(bsz, H, P, N)
        return S, y.reshape(bsz, c, H, P)

    S, y = lax.scan(step, state0.astype(F32), (chunks(xd), chunks(da), chunks(bm), chunks(cm)))
    return jnp.moveaxis(y, 0, 1).reshape(bsz, L, H, P), S


def ab_mixer(h, pos, w_in, ret_gain, w_out, ret_state, sb_past):
    bsz, L, _ = h.shape
    o1 = RET_QK_W
    o2 = o1 + RET_QK_W
    o3 = o2 + RET_V_W
    o4 = o3 + RET_V_W
    o5 = o4 + SB_W
    o6 = o5 + SB_W
    rq, rk, rv, rg, sq, sk, sv = jnp.split(h @ w_in, [o1, o2, o3, o4, o5, o6], axis=-1)
    rq = rotary(rq.reshape(bsz, L, RET_HEADS, RET_DK), pos) * (RET_DK ** -0.5)
    rk = rotary(rk.reshape(bsz, L, RET_HEADS, RET_DK), pos)
    rv = rv.reshape(bsz, L, RET_HEADS, RET_DV)
    if ret_state is None:
        ret_state = jnp.zeros((bsz, RET_HEADS, RET_DK, RET_DV), F32)
    ro, ret_new = retention(rq, rk, rv, ret_state)
    ro = ro * lax.rsqrt(jnp.mean(ro * ro, axis=-1, keepdims=True) + EPS)
    ro = ro.reshape(bsz, L, RET_V_W) * ret_gain.astype(F32) * jax.nn.silu(rg.astype(F32))
    sq = sq.reshape(bsz, L, SB_HEADS, SB_DH)
    sk = sk.reshape(bsz, L, SB_HEADS, SB_DH)
    sv = sv.reshape(bsz, L, SB_HEADS, SB_DH)
    if sb_past is None:
        so = stick_breaking_blocks(sq, sk, sv)
    else:
        ck, cv = sb_past
        past = ck.shape[1]
        k_all = jnp.concatenate([ck.astype(sk.dtype), sk], axis=1)
        v_all = jnp.concatenate([cv.astype(sv.dtype), sv], axis=1)
        so = stick_breaking(sq, k_all, v_all, pos, jnp.arange(past + L, dtype=jnp.int32))
    mix = jnp.concatenate([ro.astype(h.dtype), so.reshape(bsz, L, SB_W).astype(h.dtype)], axis=-1)
    return mix @ w_out, ret_new.astype(h.dtype), sk, sv


def cd_mixer(h, pos, w_in, rel_bias, conv_w, conv_b, dt_bias, a_log, d_skip, ssd_gain, w_out,
             band_past, ssm_state, conv_buf):
    bsz, L, _ = h.shape
    o1 = BAND_W
    o2 = 2 * BAND_W
    o3 = 3 * BAND_W
    o4 = o3 + SSD_INNER
    o5 = o4 + SSD_CONV_DIM
    bq, bk, bv, z, xbc, dt = jnp.split(h @ w_in, [o1, o2, o3, o4, o5], axis=-1)
    bq = bq.reshape(bsz, L, BAND_HEADS, BAND_DH)
    bk = bk.reshape(bsz, L, BAND_HEADS, BAND_DH)
    bv = bv.reshape(bsz, L, BAND_HEADS, BAND_DH)
    if band_past is None:
        bo = band_prompt(bq, bk, bv, rel_bias)
        keep = min(BAND_WINDOW, L)
        new_k, new_v = bk[:, L - keep:], bv[:, L - keep:]
    else:
        ck, cv = band_past
        wlen = ck.shape[1]
        k_all = jnp.concatenate([ck.astype(bk.dtype), bk], axis=1)
        v_all = jnp.concatenate([cv.astype(bv.dtype), bv], axis=1)
        k_pos = jnp.concatenate([pos[0] - wlen + jnp.arange(wlen, dtype=jnp.int32), pos])
        bo = band_attend(bq, k_all, v_all, pos, k_pos, rel_bias)
        new_k, new_v = bk, bv
    if conv_buf is None:
        conv_buf = jnp.zeros((bsz, SSD_CONV - 1, SSD_CONV_DIM), xbc.dtype)
    if ssm_state is None:
        ssm_state = jnp.zeros((bsz, SSD_HEADS, SSD_P, SSD_N), F32)
    xbc, conv_new = causal_conv(xbc, conv_buf.astype(xbc.dtype), conv_w, conv_b)
    xs, bm, cm = jnp.split(xbc, [SSD_INNER, SSD_INNER + SSD_GROUPS * SSD_N], axis=-1)
    xs = xs.reshape(bsz, L, SSD_HEADS, SSD_P)
    dt = jax.nn.softplus(dt.astype(F32) + dt_bias.astype(F32))
    a = -jnp.exp(a_log.astype(F32))
    y, ssm_new = ssd(xs, dt, a, bm.reshape(bsz, L, SSD_GROUPS, SSD_N), cm.reshape(bsz, L, SSD_GROUPS, SSD_N), ssm_state)
    y = y + d_skip.astype(F32)[:, None] * xs.astype(F32)
    y = rms_norm(y.reshape(bsz, L, SSD_INNER) * jax.nn.silu(z.astype(F32)), ssd_gain)
    mix = jnp.concatenate([bo.reshape(bsz, L, BAND_W).astype(h.dtype), y.astype(h.dtype)], axis=-1)
    return mix @ w_out, new_k, new_v, ssm_new.astype(h.dtype), conv_new


def trunk(x, pos, weights, past):
    (norm_g, ffn_w_in, ffn_w_out, final_g, w_in_ab, ret_gain, w_out_ab, w_in_cd, rel_bias,
     conv_w, conv_b, dt_bias, a_log, d_skip, ssd_gain, w_out_cd) = weights
    outs = ([], [], [], [], [], [], [])
    for l in range(DEPTH):
        j = l // 2
        x = x + 0.5 * swiglu(rms_norm(x, norm_g[l, 0]), ffn_w_in[l, 0], ffn_w_out[l, 0])
        h = rms_norm(x, norm_g[l, 1])
        if l % 2 == 0:
            ret0 = None if past is None else past[0][j]
            sbp = None if past is None else (past[1][j], past[2][j])
            y, r, k, v = ab_mixer(h, pos, w_in_ab[j], ret_gain[j], w_out_ab[j], ret0, sbp)
            outs[0].append(r)
            outs[1].append(k)
            outs[2].append(v)
        else:
            bp = None if past is None else (past[3][j], past[4][j])
            s0 = None if past is None else past[5][j]
            c0 = None if past is None else past[6][j]
            y, k, v, s, cb = cd_mixer(h, pos, w_in_cd[j], rel_bias[j], conv_w[j], conv_b[j], dt_bias[j],
                                      a_log[j], d_skip[j], ssd_gain[j], w_out_cd[j], bp, s0, c0)
            outs[3].append(k)
            outs[4].append(v)
            outs[5].append(s)
            outs[6].append(cb)
        x = x + y
        x = x + 0.5 * swiglu(rms_norm(x, norm_g[l, 2]), ffn_w_in[l, 1], ffn_w_out[l, 1])
    return rms_norm(x, final_g), tuple(jnp.stack(o) for o in outs)


def setup_inputs(seed: int = 0) -> dict:
    key = jax.random.key(seed)
    ks = jax.random.split(key, 32)

    def nrm(k, shape, s):
        return jax.random.normal(k, shape, F32) * s

    band_rows = min(BAND_WINDOW, PAST_LEN)
    dt0 = jnp.exp(jax.random.uniform(ks[20], (N_CD, SSD_HEADS), F32) * (math.log(0.1) - math.log(1e-3)) + math.log(1e-3))
    dt_bias = dt0 + jnp.log(-jnp.expm1(-dt0))
    a_log = jnp.log(jax.random.uniform(ks[21], (N_CD, SSD_HEADS), F32, minval=1.0, maxval=16.0))
    return {
        'x_prompt': nrm(ks[0], (BATCH, SEQ, D_MODEL), 1.0),
        'x_sample': nrm(ks[1], (DEC_BATCH, DEC_SEQ, D_MODEL), 1.0),
        'state_ret': nrm(ks[2], (N_AB, DEC_BATCH, RET_HEADS, RET_DK, RET_DV), 0.5),
        'cache_sb_k': nrm(ks[3], (N_AB, DEC_BATCH, PAST_LEN, SB_HEADS, SB_DH), 1.0),
        'cache_sb_v': nrm(ks[4], (N_AB, DEC_BATCH, PAST_LEN, SB_HEADS, SB_DH), 1.0),
        'cache_band_k': nrm(ks[5], (N_CD, DEC_BATCH, band_rows, BAND_HEADS, BAND_DH), 1.0),
        'cache_band_v': nrm(ks[6], (N_CD, DEC_BATCH, band_rows, BAND_HEADS, BAND_DH), 1.0),
        'state_ssm': nrm(ks[7], (N_CD, DEC_BATCH, SSD_HEADS, SSD_P, SSD_N), 0.1),
        'state_conv': nrm(ks[8], (N_CD, DEC_BATCH, SSD_CONV - 1, SSD_CONV_DIM), 1.0),
        'norm_g': 1.0 + nrm(ks[9], (DEPTH, 3, D_MODEL), 0.02),
        'ffn_w_in': nrm(ks[10], (DEPTH, 2, D_MODEL, 2 * D_FF), D_MODEL ** -0.5),
        'ffn_w_out': nrm(ks[11], (DEPTH, 2, D_FF, D_MODEL), D_FF ** -0.5),
        'final_g': 1.0 + nrm(ks[12], (D_MODEL,), 0.02),
        'w_in_ab': nrm(ks[13], (N_AB, D_MODEL, AB_IN), D_MODEL ** -0.5),
        'ret_gain': 1.0 + nrm(ks[14], (N_AB, RET_V_W), 0.02),
        'w_out_ab': nrm(ks[15], (N_AB, AB_OUT, D_MODEL), AB_OUT ** -0.5),
        'w_in_cd': nrm(ks[16], (N_CD, D_MODEL, CD_IN), D_MODEL ** -0.5),
        'rel_bias': nrm(ks[17], (N_CD, BAND_HEADS, 2 * REL_CLIP + 1), 0.2),
        'conv_w': nrm(ks[18], (N_CD, SSD_CONV, SSD_CONV_DIM), SSD_CONV ** -0.5),
        'conv_b': nrm(ks[19], (N_CD, SSD_CONV_DIM), 0.01),
        'dt_bias': dt_bias,
        'a_log': a_log,
        'd_skip': 1.0 + nrm(ks[22], (N_CD, SSD_HEADS), 0.1),
        'ssd_gain': 1.0 + nrm(ks[23], (N_CD, SSD_INNER), 0.02),
        'w_out_cd': nrm(ks[24], (N_CD, CD_OUT, D_MODEL), CD_OUT ** -0.5),
    }


def reference(x_prompt, x_sample, state_ret, cache_sb_k, cache_sb_v, cache_band_k, cache_band_v,
              state_ssm, state_conv, norm_g, ffn_w_in, ffn_w_out, final_g, w_in_ab, ret_gain, w_out_ab,
              w_in_cd, rel_bias, conv_w, conv_b, dt_bias, a_log, d_skip, ssd_gain, w_out_cd):
    weights = (norm_g, ffn_w_in, ffn_w_out, final_g, w_in_ab, ret_gain, w_out_ab, w_in_cd, rel_bias,
               conv_w, conv_b, dt_bias, a_log, d_skip, ssd_gain, w_out_cd)
    pos_p = jnp.arange(x_prompt.shape[1], dtype=jnp.int32)
    y_prompt, (p_ret, p_sb_k, p_sb_v, p_band_k, p_band_v, p_ssm, p_conv) = trunk(x_prompt, pos_p, weights, None)
    past_len = cache_sb_k.shape[2]
    pos_s = past_len + jnp.arange(x_sample.shape[1], dtype=jnp.int32)
    past = (state_ret, cache_sb_k, cache_sb_v, cache_band_k, cache_band_v, state_ssm, state_conv)
    y_sample, (s_ret, s_sb_k, s_sb_v, s_band_k, s_band_v, s_ssm, s_conv) = trunk(x_sample, pos_s, weights, past)
    return (y_prompt, y_sample, p_ret, p_sb_k, p_sb_v, p_band_k, p_band_v, p_ssm, p_conv,
            s_ret, s_sb_k, s_sb_v, s_band_k, s_band_v, s_ssm, s_conv)
```

```python
import functools
import math

import jax
import jax.numpy as jnp
from jax import lax
from jax.experimental import pallas as pl
from jax.experimental.pallas import tpu as pltpu

F32 = jnp.float32
BF16 = jnp.bfloat16

CHUNK = 64
EPS = 1e-6
ROPE_BASE = 10000.0
RET_HEADS, RET_DK, RET_DV = 4, 64, 128
SB_HEADS, SB_DH = 8, 64
BAND_HEADS, BAND_DH, BAND_CHUNKS, REL_CLIP = 8, 64, 8, 128
BAND_WINDOW = BAND_CHUNKS * CHUNK
SSD_HEADS, SSD_P, SSD_GROUPS, SSD_N, SSD_CONV = 8, 64, 2, 128, 4
SSD_INNER = SSD_HEADS * SSD_P
SSD_CONV_DIM = SSD_INNER + 2 * SSD_GROUPS * SSD_N
RET_QK_W = RET_HEADS * RET_DK
RET_V_W = RET_HEADS * RET_DV
SB_W = SB_HEADS * SB_DH
BAND_W = BAND_HEADS * BAND_DH
HEAD_SCALE = 0.125

LANES = 128
SUBLANES = 8
VMEM_LIMIT_BYTES = 56 * 1024 * 1024

NEG_BIG = -1e30

ROW_TILE = 512
FFN_CHUNK = 256
RET_TILE = 256
SB_TILE = 256
BAND_TILE = 128
SSD_TILE = 128


def _cparams(*sem):
    return pltpu.CompilerParams(dimension_semantics=sem, vmem_limit_bytes=VMEM_LIMIT_BYTES)


def _rms(x, g):
    return x * lax.rsqrt(jnp.mean(x * x, axis=-1, keepdims=True) + EPS) * g


def _silu(x):
    return x * jax.nn.sigmoid(x)


def _softplus(x):
    return jnp.maximum(x, 0.0) + jnp.log(1.0 + jnp.exp(-jnp.abs(x)))


def _dot(a, b):
    return jnp.dot(a, b, preferred_element_type=F32)


def _dot_nt(a, b):
    return lax.dot_general(a, b, (((1,), (1,)), ((), ())), preferred_element_type=F32)


def _dot_tn(a, b):
    return lax.dot_general(a, b, (((0,), (0,)), ((), ())), preferred_element_type=F32)


def _split_dot(x, m):
    hi = x.astype(BF16)
    lo = (x - hi.astype(F32)).astype(BF16)
    return _dot(hi, m) + _dot(lo, m)


def _lane_iota(shape):
    return lax.broadcasted_iota(jnp.int32, shape, len(shape) - 1)


def _ffn_kernel(*refs, nf, tf, has_mix, final):
    refs = list(refs)
    x_ref = refs.pop(0)
    if has_mix:
        ma_ref, mb_ref, wa_ref, wb_ref = refs[:4]
        refs = refs[4:]
    g_ref, win_ref, wout_ref = refs[:3]
    refs = refs[3:]
    if final:
        fg_ref = refs.pop(0)
    o_ref, x1_ref, h_ref, acc_ref = refs

    x = x_ref[...]
    if has_mix:
        x = x + _dot(ma_ref[...], wa_ref[...]) + _dot(mb_ref[...], wb_ref[...])
    x1_ref[...] = x
    h_ref[...] = _rms(x, g_ref[...]).astype(BF16)
    acc_ref[...] = jnp.zeros_like(acc_ref)

    def body(f, carry):
        gu = _dot(h_ref[...], win_ref[f])
        a = (_silu(gu[:, :tf]) * gu[:, tf:]).astype(BF16)
        acc_ref[...] += _dot(a, wout_ref[f])
        return carry

    lax.fori_loop(0, nf, body, 0)
    y = x1_ref[...] + 0.5 * acc_ref[...]
    if final:
        y = _rms(y, fg_ref[...])
    o_ref[...] = y


def _ffn(x, g, win, wout, mix=None, final_g=None):
    t, d = x.shape
    nf, _, tf2 = win.shape
    tf = tf2 // 2
    tm = min(ROW_TILE, t)
    assert t % tm == 0
    row = lambda i: (i, 0)
    const2 = lambda i: (0, 0)
    const3 = lambda i: (0, 0, 0)
    args = [x]
    specs = [pl.BlockSpec((tm, d), row)]
    if mix is not None:
        ma, mb, wa, wb = mix
        args += [ma, mb, wa, wb]
        specs += [pl.BlockSpec((tm, ma.shape[1]), row), pl.BlockSpec((tm, mb.shape[1]), row),
                  pl.BlockSpec(wa.shape, const2), pl.BlockSpec(wb.shape, const2)]
    args += [g.reshape(1, d), win, wout]
    specs += [pl.BlockSpec((1, d), const2), pl.BlockSpec(win.shape, const3), pl.BlockSpec(wout.shape, const3)]
    if final_g is not None:
        args.append(final_g.reshape(1, d))
        specs.append(pl.BlockSpec((1, d), const2))
    return pl.pallas_call(
        functools.partial(_ffn_kernel, nf=nf, tf=tf, has_mix=mix is not None, final=final_g is not None),
        grid=(t // tm,),
        in_specs=specs,
        out_specs=pl.BlockSpec((tm, d), row),
        out_shape=jax.ShapeDtypeStruct((t, d), F32),
        scratch_shapes=[pltpu.VMEM((tm, d), F32), pltpu.VMEM((tm, d), BF16), pltpu.VMEM((tm, d), F32)],
        compiler_params=_cparams("parallel"),
        name="ffn",
    )(*args)


def _proj_kernel(*refs, specs, has_rot):
    refs = list(refs)
    x_ref, g_ref, w_ref = refs[:3]
    refs = refs[3:]
    if has_rot:
        cos_ref, sin_ref = refs[:2]
        refs = refs[2:]
    h = _rms(x_ref[...], g_ref[...]).astype(BF16)
    for o_ref, (c0, width, scale, c_swap) in zip(refs, specs):
        y = _dot(h, w_ref[:, c0:c0 + width])
        if c_swap is not None:
            y = y * cos_ref[...] + _dot(h, w_ref[:, c_swap:c_swap + width]) * sin_ref[...]
        if scale != 1.0:
            y = y * scale
        o_ref[...] = y.astype(o_ref.dtype)


def _proj(x, g, w, specs, dtypes, rot=None):
    t, d = x.shape
    tm = min(ROW_TILE, t)
    assert t % tm == 0
    row = lambda i: (i, 0)
    const2 = lambda i: (0, 0)
    args = [x, g.reshape(1, d), w]
    in_specs = [pl.BlockSpec((tm, d), row), pl.BlockSpec((1, d), const2), pl.BlockSpec(w.shape, const2)]
    if rot is not None:
        cos, sin = rot
        nrot = cos.shape[0] // tm
        rot_map = lambda i: (i % nrot, 0)
        args += [cos, sin]
        in_specs += [pl.BlockSpec((tm, cos.shape[1]), rot_map), pl.BlockSpec((tm, sin.shape[1]), rot_map)]
    return pl.pallas_call(
        functools.partial(_proj_kernel, specs=tuple(specs), has_rot=rot is not None),
        grid=(t // tm,),
        in_specs=in_specs,
        out_specs=[pl.BlockSpec((tm, s[1]), row) for s in specs],
        out_shape=[jax.ShapeDtypeStruct((t, s[1]), dt) for s, dt in zip(specs, dtypes)],
        compiler_params=_cparams("parallel"),
        name="proj",
    )(*args)


def _ret_kernel(q_ref, k_ref, v_ref, rg_ref, gain_ref, s0_ref, dm_ref, qd_ref, kd_ref, cdec_ref,
                o_ref, sn_ref, s_scr):
    c = pl.program_id(1)

    @pl.when(c == 0)
    def _():
        s_scr[...] = s0_ref[0]

    q = q_ref[0]
    k = k_ref[0]
    v = v_ref[0]
    qs = (q.astype(F32) * qd_ref[...]).astype(BF16)
    ks = (k.astype(F32) * kd_ref[...]).astype(BF16)
    lane = _lane_iota((1, RET_QK_W))
    s_all = s_scr[...]
    s_bf = s_all.astype(BF16)
    upd = _dot_tn(ks, v)
    zero = jnp.zeros_like(q)
    for h in range(RET_HEADS):
        hm = (lane >= h * RET_DK) & (lane < (h + 1) * RET_DK)
        vh = v[:, h * RET_DV:(h + 1) * RET_DV]
        att = _dot_nt(jnp.where(hm, q, zero), k) * dm_ref[h]
        o = _dot(att.astype(BF16), vh) + _dot(jnp.where(hm, qs, zero), s_bf)
        o = o * lax.rsqrt(jnp.mean(o * o, axis=-1, keepdims=True) + EPS)
        sl = slice(h * RET_DV, (h + 1) * RET_DV)
        o_ref[0, :, sl] = (o * gain_ref[:, sl] * _silu(rg_ref[0, :, sl])).astype(o_ref.dtype)
        rows = slice(h * RET_DK, (h + 1) * RET_DK)
        s_scr[rows, :] = s_all[rows, :] * cdec_ref[rows, :] + upd[rows, sl]

    @pl.when(c == pl.num_programs(1) - 1)
    def _():
        sn_ref[0] = s_scr[...]


def _retention(q, k, v, rg, gain, s0, tc):
    b, l, _ = q.shape
    assert l % tc == 0
    lg = jnp.log1p(-jnp.exp2(-5.0 - jnp.arange(RET_HEADS, dtype=F32)))
    idx = jnp.arange(tc, dtype=F32)
    diff = idx[:, None] - idx[None, :]
    dm = jnp.where(diff >= 0, jnp.exp(lg[:, None, None] * jnp.maximum(diff, 0.0)), 0.0)
    qd = jnp.repeat(jnp.exp(lg[None, :] * (idx[:, None] + 1.0)), RET_DK, axis=1)
    kd = jnp.repeat(jnp.exp(lg[None, :] * (tc - 1.0 - idx[:, None])), RET_DK, axis=1)
    cdec = jnp.broadcast_to(jnp.repeat(jnp.exp(lg * tc), RET_DK)[:, None], (RET_QK_W, RET_DV))
    blk = lambda w: pl.BlockSpec((1, tc, w), lambda i, c: (i, c, 0))
    const2 = lambda i, c: (0, 0)
    st_spec = pl.BlockSpec((1, RET_QK_W, RET_DV), lambda i, c: (i, 0, 0))
    return pl.pallas_call(
        _ret_kernel,
        grid=(b, l // tc),
        in_specs=[blk(RET_QK_W), blk(RET_QK_W), blk(RET_V_W), blk(RET_V_W),
                  pl.BlockSpec((1, RET_V_W), const2), st_spec,
                  pl.BlockSpec((RET_HEADS, tc, tc), lambda i, c: (0, 0, 0)),
                  pl.BlockSpec((tc, RET_QK_W), const2), pl.BlockSpec((tc, RET_QK_W), const2),
                  pl.BlockSpec((RET_QK_W, RET_DV), const2)],
        out_specs=[blk(RET_V_W), st_spec],
        out_shape=[jax.ShapeDtypeStruct((b, l, RET_V_W), BF16),
                   jax.ShapeDtypeStruct((b, RET_QK_W, RET_DV), F32)],
        scratch_shapes=[pltpu.VMEM((RET_QK_W, RET_DV), F32)],
        compiler_params=_cparams("parallel", "arbitrary"),
        name="retention",
    )(q, k, v, rg, gain.reshape(1, RET_V_W), s0, dm, qd, kd, cdec)


def _cum_matrix(n):
    j = jnp.arange(n)[:, None]
    s = jnp.arange(n)[None, :]
    return jnp.concatenate([(j >= s).astype(BF16), jnp.ones((n, LANES), BF16)], axis=1)


def _sb_kernel(q_ref, kd_ref, vd_ref, kp_ref, vp_ref, md_ref, mp_ref, o_ref, *, tq, tkd, tk, blocks_per_q,
               n_past):
    i = pl.program_id(2)
    lane = _lane_iota((1, LANES))
    q = q_ref[0]
    zero = jnp.zeros_like(q)
    row = lax.broadcasted_iota(jnp.int32, (tq, tkd), 0)
    col = lax.broadcasted_iota(jnp.int32, (tq, tkd), 1)
    dmask = col < row
    kd = kd_ref[0].astype(BF16)
    vd = vd_ref[0].astype(BF16)
    nblk = i * blocks_per_q if n_past is None else n_past
    outs = []
    for h in range(2):
        qm = jnp.where((lane >= SB_DH) if h else (lane < SB_DH), q, zero)
        z = _dot_nt(qm, kd)
        ie = _split_dot(jnp.where(dmask, -_softplus(z), 0.0), md_ref[...])
        w = jnp.where(dmask, jnp.exp(z + ie[:, :tkd]), 0.0)
        acc = _dot(w.astype(BF16), vd)
        carry = ie[:, tkd:]

        def body(jj, st, qm=qm):
            acc, carry = st
            off = pl.multiple_of((nblk - 1 - jj) * tk, tk)
            kb = kp_ref[0, pl.ds(off, tk), :].astype(BF16)
            vb = vp_ref[0, pl.ds(off, tk), :].astype(BF16)
            z = _dot_nt(qm, kb)
            ie = _split_dot(-_softplus(z), mp_ref[...])
            w = jnp.exp(z + ie[:, :tk] + jnp.tile(carry, (1, tk // LANES)))
            return acc + _dot(w.astype(BF16), vb), carry + ie[:, tk:]

        acc, carry = lax.fori_loop(0, nblk, body, (acc, carry))
        outs.append(acc)
    o_ref[0] = jnp.where(lane < SB_DH, outs[0], outs[1]).astype(o_ref.dtype)


def _stick_breaking(q, k_own, v_own, k_past, v_past, prompt):
    b, lq, w = q.shape
    hp = w // LANES
    if prompt:
        tq = tkd = tk = min(SB_TILE, lq)
        blocks_per_q, n_past = 1, None
    else:
        tq = lq
        tkd = k_own.shape[1]
        tk = SB_TILE
        assert k_past.shape[1] % tk == 0
        blocks_per_q, n_past = 0, k_past.shape[1] // tk
    assert lq % tq == 0
    lp = k_past.shape[1]
    qspec = pl.BlockSpec((1, tq, LANES), lambda i, p, j: (i, j, p))
    dspec = pl.BlockSpec((1, tkd, LANES), lambda i, p, j: (i, j, p))
    pspec = pl.BlockSpec((1, lp, LANES), lambda i, p, j: (i, 0, p))
    md = _cum_matrix(tkd)
    mp = _cum_matrix(tk)
    const2 = lambda i, p, j: (0, 0)
    return pl.pallas_call(
        functools.partial(_sb_kernel, tq=tq, tkd=tkd, tk=tk, blocks_per_q=blocks_per_q, n_past=n_past),
        grid=(b, hp, lq // tq),
        in_specs=[qspec, dspec, dspec, pspec, pspec, pl.BlockSpec(md.shape, const2), pl.BlockSpec(mp.shape, const2)],
        out_specs=qspec,
        out_shape=jax.ShapeDtypeStruct((b, lq, w), BF16),
        compiler_params=_cparams("parallel", "parallel", "arbitrary"),
        name="stick_breaking",
    )(q, k_own, v_own, k_past, v_past, md, mp)


def _band_kernel(q_ref, k_ref, v_ref, bm_ref, o_ref, *, tq, span):
    i = pl.program_id(2)
    ws = pl.multiple_of(jnp.maximum(i * tq - (span - tq), 0), LANES)
    kw = k_ref[0, pl.ds(ws, span), :].astype(BF16)
    vw = v_ref[0, pl.ds(ws, span), :].astype(BF16)
    lane = _lane_iota((1, LANES))
    q = q_ref[0]
    zero = jnp.zeros_like(q)
    outs = []
    for h in range(2):
        qm = jnp.where((lane >= BAND_DH) if h else (lane < BAND_DH), q, zero)
        s = _dot_nt(qm, kw) + bm_ref[0, h]
        p = jnp.exp(s - jnp.max(s, axis=-1, keepdims=True))
        outs.append(_dot(p.astype(BF16), vw) / jnp.sum(p, axis=-1, keepdims=True))
    o_ref[0] = jnp.where(lane < BAND_DH, outs[0], outs[1]).astype(o_ref.dtype)


def _band_bias(rel_bias, q_pos, k_pos, k_valid):
    rel = jnp.clip(q_pos[:, None] - k_pos[None, :], -REL_CLIP, REL_CLIP) + REL_CLIP
    qc = jnp.floor_divide(q_pos, CHUNK)[:, None]
    kc = jnp.floor_divide(k_pos, CHUNK)[None, :]
    mask = (k_valid & (k_pos >= 0))[None, :] & (kc <= qc) & (kc >= qc - BAND_CHUNKS)
    return jnp.where(mask[None], rel_bias.astype(F32)[:, rel], NEG_BIG)


def _band(q, k, v, bm, tq, span):
    b, lq, w = q.shape
    hp = w // LANES
    nd = bm.shape[0]
    lk = k.shape[1]
    qspec = pl.BlockSpec((1, tq, LANES), lambda i, p, j: (i, j, p))
    kspec = pl.BlockSpec((1, lk, LANES), lambda i, p, j: (i, 0, p))
    bspec = pl.BlockSpec((1, 2, tq, span), lambda i, p, j: (jnp.minimum(j, nd - 1), p, 0, 0))
    return pl.pallas_call(
        functools.partial(_band_kernel, tq=tq, span=span),
        grid=(b, hp, lq // tq),
        in_specs=[qspec, kspec, kspec, bspec],
        out_specs=qspec,
        out_shape=jax.ShapeDtypeStruct((b, lq, w), BF16),
        compiler_params=_cparams("parallel", "parallel", "arbitrary"),
        name="band",
    )(q, k, v, bm)


def _ssd_kernel(xbc_ref, z_ref, dt_ref, cw_ref, cb_ref, dtb_ref, alog_ref, dskip_ref, gain_ref, st0_ref,
                cbuf_ref, tri_ref, e_ref, y_ref, stn_ref, st_scr, xwin, *, tc, valid_len):
    c = pl.program_id(1)

    @pl.when(c == 0)
    def _():
        st_scr[...] = st0_ref[0]
        xwin[0:SUBLANES, :] = cbuf_ref[0]

    u = xbc_ref[0]
    xwin[SUBLANES:SUBLANES + tc, :] = u
    acc = cb_ref[...] + cw_ref[SSD_CONV - 1:SSD_CONV, :] * u
    for i in range(SSD_CONV - 1):
        off = SUBLANES - (SSD_CONV - 1) + i
        acc = acc + cw_ref[i:i + 1, :] * xwin[off:off + tc, :]
    xwin[0:SUBLANES, :] = u[tc - SUBLANES:, :]
    xc = _silu(acc)
    xs = xc[:, :SSD_INNER]

    rowid = c * tc + lax.broadcasted_iota(jnp.int32, (tc, LANES), 0)
    dt = jnp.where(rowid < valid_len, _softplus(dt_ref[0] + dtb_ref[...]), 0.0)
    da = dt * (-jnp.exp(alog_ref[...]))
    cum = _split_dot_left(tri_ref[...], da)
    dt_b = _split_dot(dt, e_ref[...])
    cum_b = _split_dot(cum, e_ref[...])
    last_b = cum_b[tc - 1:tc, :]
    xd = xs * dt_b
    xw = (xd * jnp.exp(last_b - cum_b)).astype(BF16)
    xd_bf = xd.astype(BF16)
    cum_t = cum.T

    lane = _lane_iota((1, LANES))
    ri = lax.broadcasted_iota(jnp.int32, (tc, tc), 0)
    ci = lax.broadcasted_iota(jnp.int32, (tc, tc), 1)
    causal = ri >= ci
    heads_per_group = SSD_HEADS // SSD_GROUPS
    gw = heads_per_group * SSD_P
    y_parts = []
    for g in range(SSD_GROUPS):
        bg = xc[:, SSD_INNER + g * SSD_N:SSD_INNER + (g + 1) * SSD_N].astype(BF16)
        cg = xc[:, SSD_INNER + (SSD_GROUPS + g) * SSD_N:SSD_INNER + (SSD_GROUPS + g + 1) * SSD_N].astype(BF16)
        cb = _dot_nt(cg, bg)
        gl = slice(g * gw, (g + 1) * gw)
        y_parts.append((g, _dot(cg, st_scr[:, gl].astype(BF16))))
        for pr in range(heads_per_group // 2):
            pidx = g * (heads_per_group // 2) + pr
            pl_ = slice(pidx * LANES, (pidx + 1) * LANES)
            cpair = cum_b[:, pl_]
            rolled = pltpu.roll(cpair, SSD_P, axis=1)
            pair_out = []
            for hh in range(2):
                h = 2 * pidx + hh
                colv = jnp.where((lane >= SSD_P) if hh else (lane < SSD_P), cpair, rolled)
                seg = jnp.exp(jnp.where(causal, colv - cum_t[h:h + 1, :], NEG_BIG))
                pair_out.append(_dot((cb * seg).astype(BF16), xd_bf[:, pl_]))
            y_parts.append((None, jnp.where(lane < SSD_P, pair_out[0], pair_out[1])))
        st_scr[:, gl] = st_scr[:, gl] * jnp.exp(last_b[:, gl]) + _dot_tn(bg, xw[:, gl])

    y_intra = jnp.concatenate([p for g, p in y_parts if g is None], axis=1)
    y_inter = jnp.concatenate([p for g, p in y_parts if g is not None], axis=1)
    y = y_intra + y_inter * jnp.exp(cum_b) + dskip_ref[...] * xs
    y = _rms(y * _silu(z_ref[0]), gain_ref[...])
    y_ref[0] = y.astype(y_ref.dtype)

    @pl.when(c == pl.num_programs(1) - 1)
    def _():
        stn_ref[0] = st_scr[...]


def _split_dot_left(m, x):
    hi = x.astype(BF16)
    lo = (x - hi.astype(F32)).astype(BF16)
    return _dot(m, hi) + _dot(m, lo)


def _ssd(xbc, z, dt, conv_w, conv_b, dt_bias, a_log, d_skip, gain, st0, cbuf, valid_len):
    b, l, _ = xbc.shape
    tc = SSD_TILE
    assert l % tc == 0
    pad = LANES - SSD_HEADS
    tri = (jnp.arange(tc)[:, None] >= jnp.arange(tc)[None, :]).astype(BF16)
    expand = (jnp.arange(LANES)[:, None] == (jnp.arange(SSD_INNER) // SSD_P)[None, :]).astype(BF16)
    blk = lambda w: pl.BlockSpec((1, tc, w), lambda i, c: (i, c, 0))
    const2 = lambda i, c: (0, 0)
    cs = lambda a: pl.BlockSpec(a.shape, const2)
    st_spec = pl.BlockSpec((1, SSD_N, SSD_INNER), lambda i, c: (i, 0, 0))
    params = [conv_w, conv_b.reshape(1, -1),
              jnp.pad(dt_bias.astype(F32), (0, pad)).reshape(1, LANES),
              jnp.pad(a_log.astype(F32), (0, pad)).reshape(1, LANES),
              jnp.repeat(d_skip.astype(F32), SSD_P).reshape(1, SSD_INNER),
              gain.reshape(1, SSD_INNER)]
    return pl.pallas_call(
        functools.partial(_ssd_kernel, tc=tc, valid_len=valid_len),
        grid=(b, l // tc),
        in_specs=[blk(SSD_CONV_DIM), blk(SSD_INNER), blk(LANES)] + [cs(p) for p in params]
                 + [st_spec, pl.BlockSpec((1, SUBLANES, SSD_CONV_DIM), lambda i, c: (i, 0, 0)), cs(tri), cs(expand)],
        out_specs=[blk(SSD_INNER), st_spec],
        out_shape=[jax.ShapeDtypeStruct((b, l, SSD_INNER), BF16),
                   jax.ShapeDtypeStruct((b, SSD_N, SSD_INNER), F32)],
        scratch_shapes=[pltpu.VMEM((SSD_N, SSD_INNER), F32), pltpu.VMEM((SUBLANES + tc, SSD_CONV_DIM), F32)],
        compiler_params=_cparams("parallel", "arbitrary"),
        name="ssd",
    )(xbc, z, dt, *params, st0, cbuf, tri, expand)


def _rot_tables(pos, rows):
    half = RET_DK // 2
    inv = jnp.power(ROPE_BASE, -jnp.arange(half, dtype=F32) / half)
    ang = pos.astype(F32)[:, None] * inv[None, :]
    cos = jnp.tile(jnp.cos(ang), (rows // pos.shape[0], 2 * RET_HEADS))
    sin = jnp.sin(ang)
    sin = jnp.tile(jnp.concatenate([-sin, sin], axis=1), (rows // pos.shape[0], RET_HEADS))
    return cos, sin


def _swap_halves(w, heads, dh):
    d = w.shape[0]
    return jnp.flip(w.reshape(d, heads, 2, dh // 2), axis=2).reshape(d, heads * dh)


def _ab_layer(x, b, l, pos0, g, w_ext, ret_gain, past):
    t = b * l
    rows = max(l, min(ROW_TILE, t))
    pos = pos0 + jnp.arange(l, dtype=jnp.int32)
    rot = _rot_tables(pos, rows)
    o = [0]
    for wdt in (RET_QK_W, RET_QK_W, RET_V_W, RET_V_W, SB_W, SB_W, SB_W):
        o.append(o[-1] + wdt)
    specs = [(o[0], RET_QK_W, HEAD_SCALE, o[7]), (o[1], RET_QK_W, 1.0, o[7] + RET_QK_W),
             (o[2], RET_V_W, 1.0, None), (o[3], RET_V_W, 1.0, None),
             (o[4], SB_W, HEAD_SCALE, None), (o[5], SB_W, 1.0, None), (o[6], SB_W, 1.0, None)]
    rq, rk, rv, rg, sq, sk, sv = _proj(x, g, w_ext, specs, (BF16, BF16, BF16, F32, BF16, F32, F32), rot)
    r3 = lambda a: a.reshape(b, l, a.shape[-1])
    if past is None:
        s0 = jnp.zeros((b, RET_QK_W, RET_DV), F32)
        tc = min(RET_TILE, l)
    else:
        s0 = past[0].astype(F32).reshape(b, RET_QK_W, RET_DV)
        tc = l
    ro, s_new = _retention(r3(rq), r3(rk), r3(rv), r3(rg), ret_gain, s0, tc)
    sk3, sv3 = r3(sk), r3(sv)
    if past is None:
        so = _stick_breaking(r3(sq), sk3, sv3, sk3, sv3, prompt=True)
    else:
        ck = past[1].reshape(b, -1, SB_W)
        cv = past[2].reshape(b, -1, SB_W)
        padn = (-l) % LANES
        so = _stick_breaking(r3(sq), jnp.pad(sk3, ((0, 0), (0, padn), (0, 0))),
                             jnp.pad(sv3, ((0, 0), (0, padn), (0, 0))), ck, cv, prompt=False)
    outs = (s_new.reshape(b, RET_HEADS, RET_DK, RET_DV), sk3.reshape(b, l, SB_HEADS, SB_DH),
            sv3.reshape(b, l, SB_HEADS, SB_DH))
    return ro.reshape(t, RET_V_W), so.reshape(t, SB_W), outs


def _cd_layer(x, b, l, pos0, g, w_ext, rel_bias, conv_w, conv_b, dt_bias, a_log, d_skip, ssd_gain, past):
    t = b * l
    o = [0]
    for wdt in (BAND_W, BAND_W, BAND_W, SSD_INNER, SSD_CONV_DIM, LANES):
        o.append(o[-1] + wdt)
    specs = [(o[0], BAND_W, HEAD_SCALE, None), (o[1], BAND_W, 1.0, None), (o[2], BAND_W, 1.0, None),
             (o[3], SSD_INNER, 1.0, None), (o[4], SSD_CONV_DIM, 1.0, None), (o[5], LANES, 1.0, None)]
    bq, bk, bv, z, xbc, dt = _proj(x, g, w_ext, specs, (BF16, F32, F32, F32, F32, F32))
    r3 = lambda a: a.reshape(b, l, a.shape[-1])
    bk3, bv3, xbc3 = r3(bk), r3(bv), r3(xbc)
    if past is None:
        tq = min(BAND_TILE, l)
        span = BAND_WINDOW + tq
        nd = BAND_WINDOW // tq + 1
        qp = jnp.arange(tq, dtype=jnp.int32)
        kp = jnp.arange(span, dtype=jnp.int32)
        bm = jnp.stack([_band_bias(rel_bias, d * tq + qp, jnp.maximum(d * tq - BAND_WINDOW, 0) + kp,
                                   jnp.ones((span,), bool)) for d in range(nd)])
        bo = _band(r3(bq), bk3, bv3, bm, tq, span)
        keep = min(BAND_WINDOW, l)
        new_k, new_v = bk3[:, l - keep:], bv3[:, l - keep:]
        st0 = jnp.zeros((b, SSD_N, SSD_INNER), F32)
        cbuf = jnp.zeros((b, SUBLANES, SSD_CONV_DIM), F32)
        lp = l
    else:
        ck = past[0].reshape(b, -1, BAND_W)
        cv = past[1].reshape(b, -1, BAND_W)
        wlen = ck.shape[1]
        span = -(-(wlen + l) // LANES) * LANES
        padk = span - wlen - l
        k_all = jnp.concatenate([ck, bk3, jnp.zeros((b, padk, BAND_W), F32)], axis=1)
        v_all = jnp.concatenate([cv, bv3, jnp.zeros((b, padk, BAND_W), F32)], axis=1)
        qp = pos0 + jnp.arange(l, dtype=jnp.int32)
        kp = pos0 - wlen + jnp.arange(span, dtype=jnp.int32)
        bm = _band_bias(rel_bias, qp, kp, jnp.arange(span) < wlen + l)[None]
        bo = _band(r3(bq), k_all, v_all, bm, l, span)
        new_k, new_v = bk3, bv3
        st0 = jnp.transpose(past[2].astype(F32), (0, 3, 1, 2)).reshape(b, SSD_N, SSD_INNER)
        cbuf = jnp.pad(past[3].astype(F32), ((0, 0), (SUBLANES - (SSD_CONV - 1), 0), (0, 0)))
        lp = -(-l // SSD_TILE) * SSD_TILE
    conv_new = jnp.concatenate([cbuf[:, SUBLANES - (SSD_CONV - 1):], xbc3], axis=1)[:, l:]
    padl = lp - l
    pad3 = lambda a: jnp.pad(a, ((0, 0), (0, padl), (0, 0))) if padl else a
    y, st_new = _ssd(pad3(xbc3), pad3(r3(z)), pad3(r3(dt)), conv_w, conv_b, dt_bias, a_log, d_skip, ssd_gain,
                     st0, cbuf, l)
    y = y[:, :l]
    ssm_new = jnp.transpose(st_new.reshape(b, SSD_N, SSD_HEADS, SSD_P), (0, 2, 3, 1))
    outs = (new_k.reshape(b, -1, BAND_HEADS, BAND_DH), new_v.reshape(b, -1, BAND_HEADS, BAND_DH), ssm_new, conv_new)
    return bo.reshape(t, BAND_W), y.reshape(t, SSD_INNER), outs


def _trunk(x3, pos0, wts, past):
    (norm_g, win, wout, final_g, w_ab, ret_gain, wo_ab, w_cd, rel_bias, conv_w, conv_b, dt_bias, a_log,
     d_skip, ssd_gain, wo_cd) = wts
    b, l, d = x3.shape
    depth = norm_g.shape[0]
    x = x3.reshape(b * l, d)
    outs = ([], [], [], [], [], [], [])
    x = _ffn(x, norm_g[0, 0], win[0, 0], wout[0, 0])
    for layer in range(depth):
        j = layer // 2
        if layer % 2 == 0:
            pst = None if past is None else (past[0][j], past[1][j], past[2][j])
            ma, mb, (r, k, v) = _ab_layer(x, b, l, pos0, norm_g[layer, 1], w_ab[j], ret_gain[j], pst)
            outs[0].append(r)
            outs[1].append(k)
            outs[2].append(v)
            wo = wo_ab[j]
        else:
            pst = None if past is None else (past[3][j], past[4][j], past[5][j], past[6][j])
            ma, mb, (k, v, s, cb) = _cd_layer(x, b, l, pos0, norm_g[layer, 1], w_cd[j], rel_bias[j], conv_w[j],
                                              conv_b[j], dt_bias[j], a_log[j], d_skip[j], ssd_gain[j], pst)
            outs[3].append(k)
            outs[4].append(v)
            outs[5].append(s)
            outs[6].append(cb)
            wo = wo_cd[j]
        half = ma.shape[1]
        mix = (ma, mb, wo[:half], wo[half:])
        x = _ffn(x, norm_g[layer, 2], win[layer, 1], wout[layer, 1], mix=mix,
                 final_g=final_g if layer + 1 == depth else None)
        if layer + 1 < depth:
            x = _ffn(x, norm_g[layer + 1, 0], win[layer + 1, 0], wout[layer + 1, 0])
    return x.reshape(b, l, d), tuple(jnp.stack(o) for o in outs)


def kernel(x_prompt, x_sample, state_ret, cache_sb_k, cache_sb_v, cache_band_k, cache_band_v, state_ssm, state_conv, norm_g, ffn_w_in, ffn_w_out, final_g, w_in_ab, ret_gain, w_out_ab, w_in_cd, rel_bias, conv_w, conv_b, dt_bias, a_log, d_skip, ssd_gain, w_out_cd):
    d = x_prompt.shape[-1]
    depth, _, _, f2 = ffn_w_in.shape
    f = f2 // 2
    tf = FFN_CHUNK
    nf = f // tf
    assert nf * tf == f
    win = ffn_w_in.astype(BF16).reshape(depth, 2, d, 2, nf, tf)
    win = jnp.transpose(win, (0, 1, 4, 2, 3, 5)).reshape(depth, 2, nf, d, 2 * tf)
    wout = ffn_w_out.astype(BF16).reshape(depth, 2, nf, tf, d)
    w_ab = jnp.concatenate([w_in_ab, _swap3(w_in_ab[:, :, :RET_QK_W]), _swap3(w_in_ab[:, :, RET_QK_W:2 * RET_QK_W])],
                           axis=-1).astype(BF16)
    n_main = 3 * BAND_W + SSD_INNER + SSD_CONV_DIM
    w_cd = jnp.concatenate([w_in_cd, jnp.zeros(w_in_cd.shape[:2] + (LANES - SSD_HEADS,), w_in_cd.dtype)],
                           axis=-1).astype(BF16)
    assert w_cd.shape[-1] == n_main + LANES
    wts = (norm_g, win, wout, final_g, w_ab, ret_gain, w_out_ab.astype(BF16), w_cd, rel_bias, conv_w, conv_b,
           dt_bias, a_log, d_skip, ssd_gain, w_out_cd.astype(BF16))
    y_p, outs_p = _trunk(x_prompt, 0, wts, None)
    past = (state_ret, cache_sb_k, cache_sb_v, cache_band_k, cache_band_v, state_ssm, state_conv)
    y_s, outs_s = _trunk(x_sample, cache_sb_k.shape[2], wts, past)
    return (y_p, y_s) + outs_p + outs_s


def _swap3(w):
    return jnp.stack([_swap_halves(w[i], RET_HEADS, RET_DK) for i in range(w.shape[0])])
```

```python
import functools
import math

import jax
import jax.numpy as jnp
import numpy as np
from jax import lax
from jax.experimental import pallas as pl
from jax.experimental.pallas import tpu as pltpu

F32 = jnp.float32
BF16 = jnp.bfloat16

CHUNK = 64
EPS = 1e-6
ROPE_BASE = 10000.0
RET_HEADS, RET_DK, RET_DV = 4, 64, 128
SB_HEADS, SB_DH = 8, 64
BAND_HEADS, BAND_DH, BAND_CHUNKS, REL_CLIP = 8, 64, 8, 128
BAND_WINDOW = BAND_CHUNKS * CHUNK
SSD_HEADS, SSD_P, SSD_GROUPS, SSD_N, SSD_CONV = 8, 64, 2, 128, 4
SSD_INNER = SSD_HEADS * SSD_P
SSD_CONV_DIM = SSD_INNER + 2 * SSD_GROUPS * SSD_N
RET_QK_W = RET_HEADS * RET_DK
RET_V_W = RET_HEADS * RET_DV
SB_W = SB_HEADS * SB_DH
BAND_W = BAND_HEADS * BAND_DH
HEAD_SCALE = 0.125

LANES = 128
SUBLANES = 8
VMEM_LIMIT_BYTES = 56 * 1024 * 1024

NEG_BIG = -1e30

ROW_TILE = 512
FFN_CHUNK = 256
RET_TILE = 256
SB_TILE = 256
BAND_TILE = 128
SSD_TILE = 128


def _cparams(*sem):
    return pltpu.CompilerParams(dimension_semantics=sem, vmem_limit_bytes=VMEM_LIMIT_BYTES)


def _rms(x, g):
    return x * lax.rsqrt(jnp.mean(x * x, axis=-1, keepdims=True) + EPS) * g


def _silu(x):
    return x * jax.nn.sigmoid(x)


def _softplus(x):
    return jnp.maximum(x, 0.0) + jnp.log(1.0 + jnp.exp(-jnp.abs(x)))


def _dot(a, b):
    return jnp.dot(a, b, preferred_element_type=F32)


def _dot_nt(a, b):
    return lax.dot_general(a, b, (((1,), (1,)), ((), ())), preferred_element_type=F32)


def _dot_tn(a, b):
    return lax.dot_general(a, b, (((0,), (0,)), ((), ())), preferred_element_type=F32)


def _split_dot(x, m):
    hi = x.astype(BF16)
    lo = (x - hi.astype(F32)).astype(BF16)
    return _dot(hi, m) + _dot(lo, m)


def _lane_iota(shape):
    return lax.broadcasted_iota(jnp.int32, shape, len(shape) - 1)


def _ffn_kernel(*refs, nf, tf, has_mix, final):
    refs = list(refs)
    x_ref = refs.pop(0)
    if has_mix:
        ma_ref, mb_ref, wa_ref, wb_ref = refs[:4]
        refs = refs[4:]
    g_ref, win_ref, wout_ref = refs[:3]
    refs = refs[3:]
    if final:
        fg_ref = refs.pop(0)
    o_ref, x1_ref, h_ref, acc_ref = refs

    x = x_ref[...]
    if has_mix:
        x = x + _dot(ma_ref[...], wa_ref[...]) + _dot(mb_ref[...], wb_ref[...])
    x1_ref[...] = x
    h_ref[...] = _rms(x, g_ref[...]).astype(BF16)
    acc_ref[...] = jnp.zeros_like(acc_ref)

    def body(f, carry):
        gu = _dot(h_ref[...], win_ref[f])
        a = (_silu(gu[:, :tf]) * gu[:, tf:]).astype(BF16)
        acc_ref[...] += _dot(a, wout_ref[f])
        return carry

    lax.fori_loop(0, nf, body, 0)
    y = x1_ref[...] + 0.5 * acc_ref[...]
    if final:
        y = _rms(y, fg_ref[...])
    o_ref[...] = y


def _ffn(x, g, win, wout, mix=None, final_g=None):
    t, d = x.shape
    nf, _, tf2 = win.shape
    tf = tf2 // 2
    tm = min(ROW_TILE, t)
    assert t % tm == 0
    row = lambda i: (i, 0)
    const2 = lambda i: (0, 0)
    const3 = lambda i: (0, 0, 0)
    args = [x]
    specs = [pl.BlockSpec((tm, d), row)]
    if mix is not None:
        ma, mb, wa, wb = mix
        args += [ma, mb, wa, wb]
        specs += [pl.BlockSpec((tm, ma.shape[1]), row), pl.BlockSpec((tm, mb.shape[1]), row),
                  pl.BlockSpec(wa.shape, const2), pl.BlockSpec(wb.shape, const2)]
    args += [g.reshape(1, d), win, wout]
    specs += [pl.BlockSpec((1, d), const2), pl.BlockSpec(win.shape, const3), pl.BlockSpec(wout.shape, const3)]
    if final_g is not None:
        args.append(final_g.reshape(1, d))
        specs.append(pl.BlockSpec((1, d), const2))
    return pl.pallas_call(
        functools.partial(_ffn_kernel, nf=nf, tf=tf, has_mix=mix is not None, final=final_g is not None),
        grid=(t // tm,),
        in_specs=specs,
        out_specs=pl.BlockSpec((tm, d), row),
        out_shape=jax.ShapeDtypeStruct((t, d), F32),
        scratch_shapes=[pltpu.VMEM((tm, d), F32), pltpu.VMEM((tm, d), BF16), pltpu.VMEM((tm, d), F32)],
        compiler_params=_cparams("parallel"),
        name="ffn",
    )(*args)


def _proj_kernel(*refs, specs, has_rot):
    refs = list(refs)
    x_ref, g_ref, w_ref = refs[:3]
    refs = refs[3:]
    if has_rot:
        cos_ref, sin_ref = refs[:2]
        refs = refs[2:]
    h = _rms(x_ref[...], g_ref[...]).astype(BF16)
    for o_ref, (c0, width, scale, c_swap) in zip(refs, specs):
        y = _dot(h, w_ref[:, c0:c0 + width])
        if c_swap is not None:
            y = y * cos_ref[...] + _dot(h, w_ref[:, c_swap:c_swap + width]) * sin_ref[...]
        if scale != 1.0:
            y = y * scale
        o_ref[...] = y.astype(o_ref.dtype)


def _proj(x, g, w, specs, dtypes, rot=None):
    t, d = x.shape
    tm = min(ROW_TILE, t)
    assert t % tm == 0
    row = lambda i: (i, 0)
    const2 = lambda i: (0, 0)
    args = [x, g.reshape(1, d), w]
    in_specs = [pl.BlockSpec((tm, d), row), pl.BlockSpec((1, d), const2), pl.BlockSpec(w.shape, const2)]
    if rot is not None:
        cos, sin = rot
        nrot = cos.shape[0] // tm
        rot_map = lambda i: (i % nrot, 0)
        args += [cos, sin]
        in_specs += [pl.BlockSpec((tm, cos.shape[1]), rot_map), pl.BlockSpec((tm, sin.shape[1]), rot_map)]
    return pl.pallas_call(
        functools.partial(_proj_kernel, specs=tuple(specs), has_rot=rot is not None),
        grid=(t // tm,),
        in_specs=in_specs,
        out_specs=[pl.BlockSpec((tm, s[1]), row) for s in specs],
        out_shape=[jax.ShapeDtypeStruct((t, s[1]), dt) for s, dt in zip(specs, dtypes)],
        compiler_params=_cparams("parallel"),
        name="proj",
    )(*args)


def _ret_kernel(q_ref, k_ref, v_ref, rg_ref, gain_ref, s0_ref, dm_ref, qd_ref, kd_ref, cdec_ref,
                o_ref, sn_ref, s_scr):
    c = pl.program_id(1)

    @pl.when(c == 0)
    def _():
        s_scr[...] = s0_ref[0]

    q = q_ref[0]
    k = k_ref[0]
    v = v_ref[0]
    qs = (q.astype(F32) * qd_ref[...]).astype(BF16)
    ks = (k.astype(F32) * kd_ref[...]).astype(BF16)
    lane = _lane_iota((1, RET_QK_W))
    s_all = s_scr[...]
    s_bf = s_all.astype(BF16)
    upd = _dot_tn(ks, v)
    zero = jnp.zeros_like(q)
    for h in range(RET_HEADS):
        hm = (lane >= h * RET_DK) & (lane < (h + 1) * RET_DK)
        vh = v[:, h * RET_DV:(h + 1) * RET_DV]
        att = _dot_nt(jnp.where(hm, q, zero), k) * dm_ref[h]
        o = _dot(att.astype(BF16), vh) + _dot(jnp.where(hm, qs, zero), s_bf)
        o = o * lax.rsqrt(jnp.mean(o * o, axis=-1, keepdims=True) + EPS)
        sl = slice(h * RET_DV, (h + 1) * RET_DV)
        o_ref[0, :, sl] = (o * gain_ref[:, sl] * _silu(rg_ref[0, :, sl])).astype(o_ref.dtype)
        rows = slice(h * RET_DK, (h + 1) * RET_DK)
        s_scr[rows, :] = s_all[rows, :] * cdec_ref[rows, :] + upd[rows, sl]

    @pl.when(c == pl.num_programs(1) - 1)
    def _():
        sn_ref[0] = s_scr[...]


def _retention(q, k, v, rg, gain, s0, tc):
    b, l, _ = q.shape
    assert l % tc == 0
    lg = jnp.log1p(-jnp.exp2(-5.0 - jnp.arange(RET_HEADS, dtype=F32)))
    idx = jnp.arange(tc, dtype=F32)
    diff = idx[:, None] - idx[None, :]
    dm = jnp.where(diff >= 0, jnp.exp(lg[:, None, None] * jnp.maximum(diff, 0.0)), 0.0)
    qd = jnp.repeat(jnp.exp(lg[None, :] * (idx[:, None] + 1.0)), RET_DK, axis=1)
    kd = jnp.repeat(jnp.exp(lg[None, :] * (tc - 1.0 - idx[:, None])), RET_DK, axis=1)
    cdec = jnp.broadcast_to(jnp.repeat(jnp.exp(lg * tc), RET_DK)[:, None], (RET_QK_W, RET_DV))
    blk = lambda w: pl.BlockSpec((1, tc, w), lambda i, c: (i, c, 0))
    const2 = lambda i, c: (0, 0)
    st_spec = pl.BlockSpec((1, RET_QK_W, RET_DV), lambda i, c: (i, 0, 0))
    return pl.pallas_call(
        _ret_kernel,
        grid=(b, l // tc),
        in_specs=[blk(RET_QK_W), blk(RET_QK_W), blk(RET_V_W), blk(RET_V_W),
                  pl.BlockSpec((1, RET_V_W), const2), st_spec,
                  pl.BlockSpec((RET_HEADS, tc, tc), lambda i, c: (0, 0, 0)),
                  pl.BlockSpec((tc, RET_QK_W), const2), pl.BlockSpec((tc, RET_QK_W), const2),
                  pl.BlockSpec((RET_QK_W, RET_DV), const2)],
        out_specs=[blk(RET_V_W), st_spec],
        out_shape=[jax.ShapeDtypeStruct((b, l, RET_V_W), BF16),
                   jax.ShapeDtypeStruct((b, RET_QK_W, RET_DV), F32)],
        scratch_shapes=[pltpu.VMEM((RET_QK_W, RET_DV), F32)],
        compiler_params=_cparams("parallel", "arbitrary"),
        name="retention",
    )(q, k, v, rg, gain.reshape(1, RET_V_W), s0, dm, qd, kd, cdec)


def _cum_matrix(n):
    return (jnp.arange(n)[:, None] >= jnp.arange(n)[None, :]).astype(BF16)


SB_UNDERFLOW = -110.0


def _sb_block(qm, kb, vb, cum_m, carry, mask):
    z = _dot_nt(qm, kb)
    lg = -_softplus(z)
    if mask is not None:
        lg = jnp.where(mask, lg, 0.0)
    incl = _split_dot(lg, cum_m)
    w = jnp.exp(z + incl if carry is None else z + incl + carry)
    if mask is not None:
        w = jnp.where(mask, w, 0.0)
    return _dot(w.astype(BF16), vb), incl[:, 0:1]


def _sb_kernel(q_ref, kd_ref, vd_ref, kp_ref, vp_ref, md_ref, mp_ref, o_ref, *, tq, tkd, tk, blocks_per_q,
               n_past):
    i = pl.program_id(2)
    lane = _lane_iota((1, LANES))
    q = q_ref[0]
    zero = jnp.zeros_like(q)
    qms = [jnp.where(lane < SB_DH, q, zero), jnp.where(lane >= SB_DH, q, zero)]
    row = lax.broadcasted_iota(jnp.int32, (tq, tkd), 0)
    col = lax.broadcasted_iota(jnp.int32, (tq, tkd), 1)
    dmask = col < row
    kd = kd_ref[0].astype(BF16)
    vd = vd_ref[0].astype(BF16)
    nblk = i * blocks_per_q if n_past is None else n_past

    own = [_sb_block(qm, kd, vd, md_ref[...], None, dmask) for qm in qms]

    def live(c0, c1):
        return jnp.max(jnp.maximum(c0, c1)) > SB_UNDERFLOW

    def cond(st):
        return (st[0] >= 0) & st[1]

    def body(st):
        j, _, a0, a1, c0, c1 = st
        off = pl.multiple_of(j * tk, tk)
        kb = kp_ref[0, pl.ds(off, tk), :].astype(BF16)
        vb = vp_ref[0, pl.ds(off, tk), :].astype(BF16)
        d0, t0 = _sb_block(qms[0], kb, vb, mp_ref[...], c0, None)
        d1, t1 = _sb_block(qms[1], kb, vb, mp_ref[...], c1, None)
        c0 = c0 + t0
        c1 = c1 + t1
        return j - 1, live(c0, c1), a0 + d0, a1 + d1, c0, c1

    st = (jnp.asarray(nblk - 1, jnp.int32), live(own[0][1], own[1][1]), own[0][0], own[1][0], own[0][1], own[1][1])
    st = lax.while_loop(cond, body, st)
    o_ref[0] = jnp.where(lane < SB_DH, st[2], st[3]).astype(o_ref.dtype)


def _stick_breaking(q, k_own, v_own, k_past, v_past, prompt):
    b, lq, w = q.shape
    hp = w // LANES
    if prompt:
        tq = tkd = tk = min(SB_TILE, lq)
        blocks_per_q, n_past = 1, None
    else:
        tq = lq
        tkd = k_own.shape[1]
        tk = SB_TILE
        assert k_past.shape[1] % tk == 0
        blocks_per_q, n_past = 0, k_past.shape[1] // tk
    assert lq % tq == 0
    lp = k_past.shape[1]
    qspec = pl.BlockSpec((1, tq, LANES), lambda i, p, j: (i, j, p))
    dspec = pl.BlockSpec((1, tkd, LANES), lambda i, p, j: (i, j, p))
    pspec = pl.BlockSpec((1, lp, LANES), lambda i, p, j: (i, 0, p))
    md = _cum_matrix(tkd)
    mp = _cum_matrix(tk)
    const2 = lambda i, p, j: (0, 0)
    return pl.pallas_call(
        functools.partial(_sb_kernel, tq=tq, tkd=tkd, tk=tk, blocks_per_q=blocks_per_q, n_past=n_past),
        grid=(b, hp, lq // tq),
        in_specs=[qspec, dspec, dspec, pspec, pspec, pl.BlockSpec(md.shape, const2), pl.BlockSpec(mp.shape, const2)],
        out_specs=qspec,
        out_shape=jax.ShapeDtypeStruct((b, lq, w), BF16),
        compiler_params=_cparams("parallel", "parallel", "arbitrary"),
        name="stick_breaking",
    )(q, k_own, v_own, k_past, v_past, md, mp)


def _band_kernel(q_ref, k_ref, v_ref, bm_ref, o_ref, *, tq, span):
    i = pl.program_id(2)
    ws = pl.multiple_of(jnp.maximum(i * tq - (span - tq), 0), LANES)
    kw = k_ref[0, pl.ds(ws, span), :].astype(BF16)
    vw = v_ref[0, pl.ds(ws, span), :].astype(BF16)
    lane = _lane_iota((1, LANES))
    q = q_ref[0]
    zero = jnp.zeros_like(q)
    outs = []
    for h in range(2):
        qm = jnp.where((lane >= BAND_DH) if h else (lane < BAND_DH), q, zero)
        s = _dot_nt(qm, kw) + bm_ref[0, h]
        p = jnp.exp(s - jnp.max(s, axis=-1, keepdims=True))
        outs.append(_dot(p.astype(BF16), vw) / jnp.sum(p, axis=-1, keepdims=True))
    o_ref[0] = jnp.where(lane < BAND_DH, outs[0], outs[1]).astype(o_ref.dtype)


def _band_bias(rel_bias, q0, k0, tq, span, n_keys):
    d0 = q0 - k0
    n = span + tq
    padw = n + abs(d0)
    tab = rel_bias.astype(F32)
    heads = tab.shape[0]
    ext = jnp.concatenate([jnp.broadcast_to(tab[:, :1], (heads, padw)), tab,
                           jnp.broadcast_to(tab[:, -1:], (heads, padw))], axis=1)
    base = padw + REL_CLIP + d0
    g = jnp.concatenate([jnp.flip(ext[:, base - span + 1:base + 1], axis=1),
                         jnp.flip(ext[:, base + 1:base + tq + 1], axis=1)], axis=1)
    toe = jnp.tile(g, (1, tq))[:, :tq * (n - 1)].reshape(heads, tq, n - 1)[:, :, :span]
    q_pos = q0 + np.arange(tq)[:, None]
    k_pos = k0 + np.arange(span)[None, :]
    qc, kc = q_pos // CHUNK, k_pos // CHUNK
    mask = (np.arange(span)[None, :] < n_keys) & (k_pos >= 0) & (kc <= qc) & (kc >= qc - BAND_CHUNKS)
    return jnp.where(jnp.asarray(mask)[None], toe, NEG_BIG)


def _band(q, k, v, bm, tq, span):
    b, lq, w = q.shape
    hp = w // LANES
    nd = bm.shape[0]
    lk = k.shape[1]
    qspec = pl.BlockSpec((1, tq, LANES), lambda i, p, j: (i, j, p))
    kspec = pl.BlockSpec((1, lk, LANES), lambda i, p, j: (i, 0, p))
    bspec = pl.BlockSpec((1, 2, tq, span), lambda i, p, j: (jnp.minimum(j, nd - 1), p, 0, 0))
    return pl.pallas_call(
        functools.partial(_band_kernel, tq=tq, span=span),
        grid=(b, hp, lq // tq),
        in_specs=[qspec, kspec, kspec, bspec],
        out_specs=qspec,
        out_shape=jax.ShapeDtypeStruct((b, lq, w), BF16),
        compiler_params=_cparams("parallel", "parallel", "arbitrary"),
        name="band",
    )(q, k, v, bm)


def _ssd_kernel(xbc_ref, z_ref, dt_ref, cw_ref, cb_ref, dtb_ref, alog_ref, dskip_ref, gain_ref, st0_ref,
                cbuf_ref, tri_ref, e_ref, y_ref, stn_ref, st_scr, xwin, *, tc, valid_len):
    c = pl.program_id(1)

    @pl.when(c == 0)
    def _():
        st_scr[...] = st0_ref[0]
        xwin[0:SUBLANES, :] = cbuf_ref[0]

    u = xbc_ref[0]
    xwin[SUBLANES:SUBLANES + tc, :] = u
    acc = cb_ref[...] + cw_ref[SSD_CONV - 1:SSD_CONV, :] * u
    for i in range(SSD_CONV - 1):
        off = SUBLANES - (SSD_CONV - 1) + i
        acc = acc + cw_ref[i:i + 1, :] * xwin[off:off + tc, :]
    xwin[0:SUBLANES, :] = u[tc - SUBLANES:, :]
    xc = _silu(acc)
    xs = xc[:, :SSD_INNER]

    rowid = c * tc + lax.broadcasted_iota(jnp.int32, (tc, LANES), 0)
    dt = jnp.where(rowid < valid_len, _softplus(dt_ref[0] + dtb_ref[...]), 0.0)
    da = dt * (-jnp.exp(alog_ref[...]))
    cum = _split_dot_left(tri_ref[...], da)
    dt_b = _split_dot(dt, e_ref[...])
    cum_b = _split_dot(cum, e_ref[...])
    last_b = cum_b[tc - 1:tc, :]
    xd = xs * dt_b
    xw = (xd * jnp.exp(last_b - cum_b)).astype(BF16)
    xd_bf = xd.astype(BF16)
    cum_t = cum.T

    lane = _lane_iota((1, LANES))
    ri = lax.broadcasted_iota(jnp.int32, (tc, tc), 0)
    ci = lax.broadcasted_iota(jnp.int32, (tc, tc), 1)
    causal = ri >= ci
    heads_per_group = SSD_HEADS // SSD_GROUPS
    gw = heads_per_group * SSD_P
    y_parts = []
    for g in range(SSD_GROUPS):
        bg = xc[:, SSD_INNER + g * SSD_N:SSD_INNER + (g + 1) * SSD_N].astype(BF16)
        cg = xc[:, SSD_INNER + (SSD_GROUPS + g) * SSD_N:SSD_INNER + (SSD_GROUPS + g + 1) * SSD_N].astype(BF16)
        cb = _dot_nt(cg, bg)
        gl = slice(g * gw, (g + 1) * gw)
        y_parts.append((g, _dot(cg, st_scr[:, gl].astype(BF16))))
        for pr in range(heads_per_group // 2):
            pidx = g * (heads_per_group // 2) + pr
            pl_ = slice(pidx * LANES, (pidx + 1) * LANES)
            cpair = cum_b[:, pl_]
            rolled = pltpu.roll(cpair, SSD_P, axis=1)
            pair_out = []
            for hh in range(2):
                h = 2 * pidx + hh
                colv = jnp.where((lane >= SSD_P) if hh else (lane < SSD_P), cpair, rolled)
                seg = jnp.exp(jnp.where(causal, colv - cum_t[h:h + 1, :], NEG_BIG))
                pair_out.append(_dot((cb * seg).astype(BF16), xd_bf[:, pl_]))
            y_parts.append((None, jnp.where(lane < SSD_P, pair_out[0], pair_out[1])))
        st_scr[:, gl] = st_scr[:, gl] * jnp.exp(last_b[:, gl]) + _dot_tn(bg, xw[:, gl])

    y_intra = jnp.concatenate([p for g, p in y_parts if g is None], axis=1)
    y_inter = jnp.concatenate([p for g, p in y_parts if g is not None], axis=1)
    y = y_intra + y_inter * jnp.exp(cum_b) + dskip_ref[...] * xs
    y = _rms(y * _silu(z_ref[0]), gain_ref[...])
    y_ref[0] = y.astype(y_ref.dtype)

    @pl.when(c == pl.num_programs(1) - 1)
    def _():
        stn_ref[0] = st_scr[...]


def _split_dot_left(m, x):
    hi = x.astype(BF16)
    lo = (x - hi.astype(F32)).astype(BF16)
    return _dot(m, hi) + _dot(m, lo)


def _ssd(xbc, z, dt, conv_w, conv_b, dt_bias, a_log, d_skip, gain, st0, cbuf, valid_len):
    b, l, _ = xbc.shape
    tc = SSD_TILE
    assert l % tc == 0
    pad = LANES - SSD_HEADS
    tri = (jnp.arange(tc)[:, None] >= jnp.arange(tc)[None, :]).astype(BF16)
    expand = (jnp.arange(LANES)[:, None] == (jnp.arange(SSD_INNER) // SSD_P)[None, :]).astype(BF16)
    blk = lambda w: pl.BlockSpec((1, tc, w), lambda i, c: (i, c, 0))
    const2 = lambda i, c: (0, 0)
    cs = lambda a: pl.BlockSpec(a.shape, const2)
    st_spec = pl.BlockSpec((1, SSD_N, SSD_INNER), lambda i, c: (i, 0, 0))
    params = [conv_w, conv_b.reshape(1, -1),
              jnp.pad(dt_bias.astype(F32), (0, pad)).reshape(1, LANES),
              jnp.pad(a_log.astype(F32), (0, pad)).reshape(1, LANES),
              jnp.repeat(d_skip.astype(F32), SSD_P).reshape(1, SSD_INNER),
              gain.reshape(1, SSD_INNER)]
    return pl.pallas_call(
        functools.partial(_ssd_kernel, tc=tc, valid_len=valid_len),
        grid=(b, l // tc),
        in_specs=[blk(SSD_CONV_DIM), blk(SSD_INNER), blk(LANES)] + [cs(p) for p in params]
                 + [st_spec, pl.BlockSpec((1, SUBLANES, SSD_CONV_DIM), lambda i, c: (i, 0, 0)), cs(tri), cs(expand)],
        out_specs=[blk(SSD_INNER), st_spec],
        out_shape=[jax.ShapeDtypeStruct((b, l, SSD_INNER), BF16),
                   jax.ShapeDtypeStruct((b, SSD_N, SSD_INNER), F32)],
        scratch_shapes=[pltpu.VMEM((SSD_N, SSD_INNER), F32), pltpu.VMEM((SUBLANES + tc, SSD_CONV_DIM), F32)],
        compiler_params=_cparams("parallel", "arbitrary"),
        name="ssd",
    )(xbc, z, dt, *params, st0, cbuf, tri, expand)


def _rot_tables(pos, rows):
    half = RET_DK // 2
    inv = jnp.power(ROPE_BASE, -jnp.arange(half, dtype=F32) / half)
    ang = pos.astype(F32)[:, None] * inv[None, :]
    cos = jnp.tile(jnp.cos(ang), (rows // pos.shape[0], 2 * RET_HEADS))
    sin = jnp.sin(ang)
    sin = jnp.tile(jnp.concatenate([-sin, sin], axis=1), (rows // pos.shape[0], RET_HEADS))
    return cos, sin


def _swap_halves(w, heads, dh):
    d = w.shape[0]
    return jnp.flip(w.reshape(d, heads, 2, dh // 2), axis=2).reshape(d, heads * dh)


def _ab_layer(x, b, l, pos0, g, w_ext, ret_gain, past):
    t = b * l
    rows = max(l, min(ROW_TILE, t))
    pos = pos0 + jnp.arange(l, dtype=jnp.int32)
    rot = _rot_tables(pos, rows)
    o = [0]
    for wdt in (RET_QK_W, RET_QK_W, RET_V_W, RET_V_W, SB_W, SB_W, SB_W):
        o.append(o[-1] + wdt)
    specs = [(o[0], RET_QK_W, HEAD_SCALE, o[7]), (o[1], RET_QK_W, 1.0, o[7] + RET_QK_W),
             (o[2], RET_V_W, 1.0, None), (o[3], RET_V_W, 1.0, None),
             (o[4], SB_W, HEAD_SCALE, None), (o[5], SB_W, 1.0, None), (o[6], SB_W, 1.0, None)]
    rq, rk, rv, rg, sq, sk, sv = _proj(x, g, w_ext, specs, (BF16, BF16, BF16, F32, BF16, F32, F32), rot)
    r3 = lambda a: a.reshape(b, l, a.shape[-1])
    if past is None:
        s0 = jnp.zeros((b, RET_QK_W, RET_DV), F32)
        tc = min(RET_TILE, l)
    else:
        s0 = past[0].astype(F32).reshape(b, RET_QK_W, RET_DV)
        tc = l
    ro, s_new = _retention(r3(rq), r3(rk), r3(rv), r3(rg), ret_gain, s0, tc)
    sk3, sv3 = r3(sk), r3(sv)
    if past is None:
        so = _stick_breaking(r3(sq), sk3, sv3, sk3, sv3, prompt=True)
    else:
        ck = past[1].reshape(b, -1, SB_W)
        cv = past[2].reshape(b, -1, SB_W)
        padn = (-l) % LANES
        so = _stick_breaking(r3(sq), jnp.pad(sk3, ((0, 0), (0, padn), (0, 0))),
                             jnp.pad(sv3, ((0, 0), (0, padn), (0, 0))), ck, cv, prompt=False)
    outs = (s_new.reshape(b, RET_HEADS, RET_DK, RET_DV), sk3.reshape(b, l, SB_HEADS, SB_DH),
            sv3.reshape(b, l, SB_HEADS, SB_DH))
    return ro.reshape(t, RET_V_W), so.reshape(t, SB_W), outs


def _cd_layer(x, b, l, pos0, g, w_ext, rel_bias, conv_w, conv_b, dt_bias, a_log, d_skip, ssd_gain, past):
    t = b * l
    o = [0]
    for wdt in (BAND_W, BAND_W, BAND_W, SSD_INNER, SSD_CONV_DIM, LANES):
        o.append(o[-1] + wdt)
    specs = [(o[0], BAND_W, HEAD_SCALE, None), (o[1], BAND_W, 1.0, None), (o[2], BAND_W, 1.0, None),
             (o[3], SSD_INNER, 1.0, None), (o[4], SSD_CONV_DIM, 1.0, None), (o[5], LANES, 1.0, None)]
    bq, bk, bv, z, xbc, dt = _proj(x, g, w_ext, specs, (BF16, F32, F32, F32, F32, F32))
    r3 = lambda a: a.reshape(b, l, a.shape[-1])
    bk3, bv3, xbc3 = r3(bk), r3(bv), r3(xbc)
    if past is None:
        tq = min(BAND_TILE, l)
        span = BAND_WINDOW + tq
        nd = BAND_WINDOW // tq + 1
        bm = jnp.stack([_band_bias(rel_bias, d * tq, max(d * tq - BAND_WINDOW, 0), tq, span, span)
                        for d in range(nd)])
        bo = _band(r3(bq), bk3, bv3, bm, tq, span)
        keep = min(BAND_WINDOW, l)
        new_k, new_v = bk3[:, l - keep:], bv3[:, l - keep:]
        st0 = jnp.zeros((b, SSD_N, SSD_INNER), F32)
        cbuf = jnp.zeros((b, SUBLANES, SSD_CONV_DIM), F32)
        lp = l
    else:
        ck = past[0].reshape(b, -1, BAND_W)
        cv = past[1].reshape(b, -1, BAND_W)
        wlen = ck.shape[1]
        span = -(-(wlen + l) // LANES) * LANES
        padk = span - wlen - l
        k_all = jnp.concatenate([ck, bk3, jnp.zeros((b, padk, BAND_W), F32)], axis=1)
        v_all = jnp.concatenate([cv, bv3, jnp.zeros((b, padk, BAND_W), F32)], axis=1)
        bm = _band_bias(rel_bias, pos0, pos0 - wlen, l, span, wlen + l)[None]
        bo = _band(r3(bq), k_all, v_all, bm, l, span)
        new_k, new_v = bk3, bv3
        st0 = jnp.transpose(past[2].astype(F32), (0, 3, 1, 2)).reshape(b, SSD_N, SSD_INNER)
        cbuf = jnp.pad(past[3].astype(F32), ((0, 0), (SUBLANES - (SSD_CONV - 1), 0), (0, 0)))
        lp = -(-l // SSD_TILE) * SSD_TILE
    assert l >= SSD_CONV - 1
    conv_new = xbc3[:, l - (SSD_CONV - 1):]
    padl = lp - l
    pad3 = lambda a: jnp.pad(a, ((0, 0), (0, padl), (0, 0))) if padl else a
    y, st_new = _ssd(pad3(xbc3), pad3(r3(z)), pad3(r3(dt)), conv_w, conv_b, dt_bias, a_log, d_skip, ssd_gain,
                     st0, cbuf, l)
    y = y[:, :l]
    ssm_new = jnp.transpose(st_new.reshape(b, SSD_N, SSD_HEADS, SSD_P), (0, 2, 3, 1))
    outs = (new_k.reshape(b, -1, BAND_HEADS, BAND_DH), new_v.reshape(b, -1, BAND_HEADS, BAND_DH), ssm_new, conv_new)
    return bo.reshape(t, BAND_W), y.reshape(t, SSD_INNER), outs


def _trunk(x3, pos0, wts, past):
    (norm_g, win, wout, final_g, w_ab, ret_gain, wo_ab, w_cd, rel_bias, conv_w, conv_b, dt_bias, a_log,
     d_skip, ssd_gain, wo_cd) = wts
    b, l, d = x3.shape
    depth = norm_g.shape[0]
    x = x3.reshape(b * l, d)
    outs = ([], [], [], [], [], [], [])
    x = _ffn(x, norm_g[0, 0], win[0, 0], wout[0, 0])
    for layer in range(depth):
        j = layer // 2
        if layer % 2 == 0:
            pst = None if past is None else (past[0][j], past[1][j], past[2][j])
            ma, mb, (r, k, v) = _ab_layer(x, b, l, pos0, norm_g[layer, 1], w_ab[j], ret_gain[j], pst)
            outs[0].append(r)
            outs[1].append(k)
            outs[2].append(v)
            wo = wo_ab[j]
        else:
            pst = None if past is None else (past[3][j], past[4][j], past[5][j], past[6][j])
            ma, mb, (k, v, s, cb) = _cd_layer(x, b, l, pos0, norm_g[layer, 1], w_cd[j], rel_bias[j], conv_w[j],
                                              conv_b[j], dt_bias[j], a_log[j], d_skip[j], ssd_gain[j], pst)
            outs[3].append(k)
            outs[4].append(v)
            outs[5].append(s)
            outs[6].append(cb)
            wo = wo_cd[j]
        half = ma.shape[1]
        mix = (ma, mb, wo[:half], wo[half:])
        x = _ffn(x, norm_g[layer, 2], win[layer, 1], wout[layer, 1], mix=mix,
                 final_g=final_g if layer + 1 == depth else None)
        if layer + 1 < depth:
            x = _ffn(x, norm_g[layer + 1, 0], win[layer + 1, 0], wout[layer + 1, 0])
    return x.reshape(b, l, d), tuple(jnp.stack(o) for o in outs)


def kernel(x_prompt, x_sample, state_ret, cache_sb_k, cache_sb_v, cache_band_k, cache_band_v, state_ssm, state_conv, norm_g, ffn_w_in, ffn_w_out, final_g, w_in_ab, ret_gain, w_out_ab, w_in_cd, rel_bias, conv_w, conv_b, dt_bias, a_log, d_skip, ssd_gain, w_out_cd):
    d = x_prompt.shape[-1]
    depth, _, _, f2 = ffn_w_in.shape
    f = f2 // 2
    tf = FFN_CHUNK
    nf = f // tf
    assert nf * tf == f
    win = ffn_w_in.astype(BF16).reshape(depth, 2, d, 2, nf, tf)
    win = jnp.transpose(win, (0, 1, 4, 2, 3, 5)).reshape(depth, 2, nf, d, 2 * tf)
    wout = ffn_w_out.astype(BF16).reshape(depth, 2, nf, tf, d)
    w_ab = jnp.concatenate([w_in_ab, _swap3(w_in_ab[:, :, :RET_QK_W]), _swap3(w_in_ab[:, :, RET_QK_W:2 * RET_QK_W])],
                           axis=-1).astype(BF16)
    n_main = 3 * BAND_W + SSD_INNER + SSD_CONV_DIM
    w_cd = jnp.concatenate([w_in_cd, jnp.zeros(w_in_cd.shape[:2] + (LANES - SSD_HEADS,), w_in_cd.dtype)],
                           axis=-1).astype(BF16)
    assert w_cd.shape[-1] == n_main + LANES
    wts = (norm_g, win, wout, final_g, w_ab, ret_gain, w_out_ab.astype(BF16), w_cd, rel_bias, conv_w, conv_b,
           dt_bias, a_log, d_skip, ssd_gain, w_out_cd.astype(BF16))
    y_p, outs_p = _trunk(x_prompt, 0, wts, None)
    past = (state_ret, cache_sb_k, cache_sb_v, cache_band_k, cache_band_v, state_ssm, state_conv)
    y_s, outs_s = _trunk(x_sample, cache_sb_k.shape[2], wts, past)
    return (y_p, y_s) + outs_p + outs_s


def _swap3(w):
    return jnp.stack([_swap_halves(w[i], RET_HEADS, RET_DK) for i in range(w.shape[0])])
```

```python
import functools
import math

import jax
import jax.numpy as jnp
import numpy as np
from jax import lax
from jax.experimental import pallas as pl
from jax.experimental.pallas import tpu as pltpu

F32 = jnp.float32
BF16 = jnp.bfloat16

CHUNK = 64
EPS = 1e-6
ROPE_BASE = 10000.0
RET_HEADS, RET_DK, RET_DV = 4, 64, 128
SB_HEADS, SB_DH = 8, 64
BAND_HEADS, BAND_DH, BAND_CHUNKS, REL_CLIP = 8, 64, 8, 128
BAND_WINDOW = BAND_CHUNKS * CHUNK
SSD_HEADS, SSD_P, SSD_GROUPS, SSD_N, SSD_CONV = 8, 64, 2, 128, 4
SSD_INNER = SSD_HEADS * SSD_P
SSD_CONV_DIM = SSD_INNER + 2 * SSD_GROUPS * SSD_N
RET_QK_W = RET_HEADS * RET_DK
RET_V_W = RET_HEADS * RET_DV
SB_W = SB_HEADS * SB_DH
BAND_W = BAND_HEADS * BAND_DH
HEAD_SCALE = 0.125

LANES = 128
SUBLANES = 8
VMEM_LIMIT_BYTES = 56 * 1024 * 1024

NEG_BIG = -1e30

ROW_TILE = 512
FFN_CHUNK = 512
RET_TILE = 256
SB_TILE = 256
BAND_TILE = 128
SSD_TILE = 128


def _cparams(*sem):
    return pltpu.CompilerParams(dimension_semantics=sem, vmem_limit_bytes=VMEM_LIMIT_BYTES)


def _rms(x, g):
    return x * lax.rsqrt(jnp.mean(x * x, axis=-1, keepdims=True) + EPS) * g


def _silu(x):
    return x * jax.nn.sigmoid(x)


def _softplus(x):
    return jnp.maximum(x, 0.0) + jnp.log(1.0 + jnp.exp(-jnp.abs(x)))


def _dot(a, b):
    return jnp.dot(a, b, preferred_element_type=F32)


def _dot_nt(a, b):
    return lax.dot_general(a, b, (((1,), (1,)), ((), ())), preferred_element_type=F32)


def _dot_tn(a, b):
    return lax.dot_general(a, b, (((0,), (0,)), ((), ())), preferred_element_type=F32)


def _split_dot(x, m):
    hi = x.astype(BF16)
    lo = (x - hi.astype(F32)).astype(BF16)
    return _dot(hi, m) + _dot(lo, m)


def _lane_iota(shape):
    return lax.broadcasted_iota(jnp.int32, shape, len(shape) - 1)


def _ffn_kernel(*refs, f, has_mix, final):
    refs = list(refs)
    x_ref = refs.pop(0)
    if has_mix:
        ma_ref, mb_ref, wa_ref, wb_ref = refs[:4]
        refs = refs[4:]
    g_ref, win_ref, wout_ref = refs[:3]
    refs = refs[3:]
    if final:
        fg_ref = refs.pop(0)
    o_ref, x1_ref, a_ref = refs

    x = x_ref[...]
    if has_mix:
        x = x + _dot(ma_ref[...], wa_ref[...]) + _dot(mb_ref[...], wb_ref[...])
    x1_ref[...] = x
    h = _rms(x, g_ref[...]).astype(BF16)
    for f0 in range(0, f, FFN_CHUNK):
        f1 = min(f0 + FFN_CHUNK, f)
        gate = _dot(h, win_ref[:, f0:f1])
        up = _dot(h, win_ref[:, f + f0:f + f1])
        a_ref[:, f0:f1] = (_silu(gate) * up).astype(BF16)
    y = x1_ref[...] + 0.5 * _dot(a_ref[...], wout_ref[...])
    if final:
        y = _rms(y, fg_ref[...])
    o_ref[...] = y


def _resident(shape):
    return pl.BlockSpec(shape, lambda *_: (0,) * len(shape), pipeline_mode=pl.Buffered(1))


def _ffn(x, g, win, wout, mix=None, final_g=None):
    t, d = x.shape
    f = wout.shape[0]
    tm = min(ROW_TILE, t)
    assert t % tm == 0 and f % LANES == 0
    row = lambda i: (i, 0)
    args = [x]
    specs = [pl.BlockSpec((tm, d), row)]
    if mix is not None:
        ma, mb, wa, wb = mix
        args += [ma, mb, wa, wb]
        specs += [pl.BlockSpec((tm, ma.shape[1]), row), pl.BlockSpec((tm, mb.shape[1]), row),
                  _resident(wa.shape), _resident(wb.shape)]
    args += [g.reshape(1, d), win, wout]
    specs += [_resident((1, d)), _resident(win.shape), _resident(wout.shape)]
    if final_g is not None:
        args.append(final_g.reshape(1, d))
        specs.append(_resident((1, d)))
    return pl.pallas_call(
        functools.partial(_ffn_kernel, f=f, has_mix=mix is not None, final=final_g is not None),
        grid=(t // tm,),
        in_specs=specs,
        out_specs=pl.BlockSpec((tm, d), row),
        out_shape=jax.ShapeDtypeStruct((t, d), F32),
        scratch_shapes=[pltpu.VMEM((tm, d), F32), pltpu.VMEM((tm, f), BF16)],
        compiler_params=_cparams("parallel"),
        name="ffn",
    )(*args)


def _proj_kernel(*refs, specs, has_rot):
    refs = list(refs)
    x_ref, g_ref, w_ref = refs[:3]
    refs = refs[3:]
    if has_rot:
        cos_ref, sin_ref = refs[:2]
        refs = refs[2:]
    h = _rms(x_ref[...], g_ref[...]).astype(BF16)
    for c0, width, scale, c_swap, dts in specs:
        y = _dot(h, w_ref[:, c0:c0 + width])
        if c_swap is not None:
            y = y * cos_ref[...] + _dot(h, w_ref[:, c_swap:c_swap + width]) * sin_ref[...]
        if scale != 1.0:
            y = y * scale
        for _ in dts:
            o_ref = refs.pop(0)
            o_ref[...] = y.astype(o_ref.dtype)


def _proj(x, g, w, specs, rot=None):
    t, d = x.shape
    tm = min(ROW_TILE, t)
    assert t % tm == 0
    row = lambda i: (i, 0)
    const2 = lambda i: (0, 0)
    args = [x, g.reshape(1, d), w]
    in_specs = [pl.BlockSpec((tm, d), row), pl.BlockSpec((1, d), const2), _resident(w.shape)]
    outs = [(s[1], dt) for s in specs for dt in s[4]]
    if rot is not None:
        cos, sin = rot
        nrot = cos.shape[0] // tm
        rot_map = lambda i: (i % nrot, 0)
        args += [cos, sin]
        in_specs += [pl.BlockSpec((tm, cos.shape[1]), rot_map), pl.BlockSpec((tm, sin.shape[1]), rot_map)]
    return pl.pallas_call(
        functools.partial(_proj_kernel, specs=tuple(specs), has_rot=rot is not None),
        grid=(t // tm,),
        in_specs=in_specs,
        out_specs=[pl.BlockSpec((tm, wd), row) for wd, _ in outs],
        out_shape=[jax.ShapeDtypeStruct((t, wd), dt) for wd, dt in outs],
        compiler_params=_cparams("parallel"),
        name="proj",
    )(*args)


def _ret_kernel(q_ref, k_ref, v_ref, rg_ref, gain_ref, s0_ref, dm_ref, qd_ref, kd_ref, cdec_ref,
                o_ref, sn_ref, s_scr):
    c = pl.program_id(1)

    @pl.when(c == 0)
    def _():
        s_scr[...] = s0_ref[0]

    q = q_ref[0]
    k = k_ref[0]
    v = v_ref[0]
    qs = (q.astype(F32) * qd_ref[...]).astype(BF16)
    ks = (k.astype(F32) * kd_ref[...]).astype(BF16)
    lane = _lane_iota((1, RET_QK_W))
    s_all = s_scr[...]
    s_bf = s_all.astype(BF16)
    upd = _dot_tn(ks, v)
    zero = jnp.zeros_like(q)
    for h in range(RET_HEADS):
        hm = (lane >= h * RET_DK) & (lane < (h + 1) * RET_DK)
        vh = v[:, h * RET_DV:(h + 1) * RET_DV]
        att = _dot_nt(jnp.where(hm, q, zero), k) * dm_ref[h]
        o = _dot(att.astype(BF16), vh) + _dot(jnp.where(hm, qs, zero), s_bf)
        o = o * lax.rsqrt(jnp.mean(o * o, axis=-1, keepdims=True) + EPS)
        sl = slice(h * RET_DV, (h + 1) * RET_DV)
        o_ref[0, :, sl] = (o * gain_ref[:, sl] * _silu(rg_ref[0, :, sl])).astype(o_ref.dtype)
        rows = slice(h * RET_DK, (h + 1) * RET_DK)
        s_scr[rows, :] = s_all[rows, :] * cdec_ref[rows, :] + upd[rows, sl]

    @pl.when(c == pl.num_programs(1) - 1)
    def _():
        sn_ref[0] = s_scr[...]


def _retention(q, k, v, rg, gain, s0, tc):
    b, l, _ = q.shape
    assert l % tc == 0
    lg = jnp.log1p(-jnp.exp2(-5.0 - jnp.arange(RET_HEADS, dtype=F32)))
    idx = jnp.arange(tc, dtype=F32)
    diff = idx[:, None] - idx[None, :]
    dm = jnp.where(diff >= 0, jnp.exp(lg[:, None, None] * jnp.maximum(diff, 0.0)), 0.0)
    qd = jnp.repeat(jnp.exp(lg[None, :] * (idx[:, None] + 1.0)), RET_DK, axis=1)
    kd = jnp.repeat(jnp.exp(lg[None, :] * (tc - 1.0 - idx[:, None])), RET_DK, axis=1)
    cdec = jnp.broadcast_to(jnp.repeat(jnp.exp(lg * tc), RET_DK)[:, None], (RET_QK_W, RET_DV))
    blk = lambda w: pl.BlockSpec((1, tc, w), lambda i, c: (i, c, 0))
    const2 = lambda i, c: (0, 0)
    st_spec = pl.BlockSpec((1, RET_QK_W, RET_DV), lambda i, c: (i, 0, 0))
    return pl.pallas_call(
        _ret_kernel,
        grid=(b, l // tc),
        in_specs=[blk(RET_QK_W), blk(RET_QK_W), blk(RET_V_W), blk(RET_V_W),
                  pl.BlockSpec((1, RET_V_W), const2), st_spec,
                  pl.BlockSpec((RET_HEADS, tc, tc), lambda i, c: (0, 0, 0)),
                  pl.BlockSpec((tc, RET_QK_W), const2), pl.BlockSpec((tc, RET_QK_W), const2),
                  pl.BlockSpec((RET_QK_W, RET_DV), const2)],
        out_specs=[blk(RET_V_W), st_spec],
        out_shape=[jax.ShapeDtypeStruct((b, l, RET_V_W), BF16),
                   jax.ShapeDtypeStruct((b, RET_QK_W, RET_DV), F32)],
        scratch_shapes=[pltpu.VMEM((RET_QK_W, RET_DV), F32)],
        compiler_params=_cparams("parallel", "arbitrary"),
        name="retention",
    )(q, k, v, rg, gain.reshape(1, RET_V_W), s0, dm, qd, kd, cdec)


def _cum_matrix(n):
    return (jnp.arange(n)[:, None] >= jnp.arange(n)[None, :]).astype(BF16)


SB_UNDERFLOW = -110.0


def _sb_block(qms, kb, vb, cum_m, carries, mask):
    zs = [_dot_nt(qm, kb) for qm in qms]
    lgs = [-_softplus(z) for z in zs]
    if mask is not None:
        lgs = [jnp.where(mask, lg, 0.0) for lg in lgs]
    his = [lg.astype(BF16) for lg in lgs]
    los = [(lg - hi.astype(F32)).astype(BF16) for lg, hi in zip(lgs, his)]
    incls = [_dot(hi, cum_m) + _dot(lo, cum_m) for hi, lo in zip(his, los)]
    if carries is None:
        ws = [jnp.exp(z + incl) for z, incl in zip(zs, incls)]
    else:
        ws = [jnp.exp(z + incl + c) for z, incl, c in zip(zs, incls, carries)]
    if mask is not None:
        ws = [jnp.where(mask, w, 0.0) for w in ws]
    return [_dot(w.astype(BF16), vb) for w in ws], [incl[:, 0:1] for incl in incls]


def _sb_kernel(q_ref, kd_ref, vd_ref, kp_ref, vp_ref, md_ref, mp_ref, o_ref, *, tq, tkd, tk, blocks_per_q,
               n_past):
    i = pl.program_id(2)
    lane = _lane_iota((1, LANES))
    q = q_ref[0]
    zero = jnp.zeros_like(q)
    qms = [jnp.where(lane < SB_DH, q, zero), jnp.where(lane >= SB_DH, q, zero)]
    row = lax.broadcasted_iota(jnp.int32, (tq, tkd), 0)
    col = lax.broadcasted_iota(jnp.int32, (tq, tkd), 1)
    dmask = col < row
    kd = kd_ref[0].astype(BF16)
    vd = vd_ref[0].astype(BF16)
    nblk = i * blocks_per_q if n_past is None else n_past

    accs, carries = _sb_block(qms, kd, vd, md_ref[...], None, dmask)

    def live(c0, c1):
        return jnp.max(jnp.maximum(c0, c1)) > SB_UNDERFLOW

    def cond(st):
        return (st[0] >= 0) & st[1]

    def body(st):
        j, _, a0, a1, c0, c1 = st
        off = pl.multiple_of(j * tk, tk)
        kb = kp_ref[0, pl.ds(off, tk), :].astype(BF16)
        vb = vp_ref[0, pl.ds(off, tk), :].astype(BF16)
        (d0, d1), (t0, t1) = _sb_block(qms, kb, vb, mp_ref[...], (c0, c1), None)
        c0 = c0 + t0
        c1 = c1 + t1
        return j - 1, live(c0, c1), a0 + d0, a1 + d1, c0, c1

    st = (jnp.asarray(nblk - 1, jnp.int32), live(*carries), accs[0], accs[1], carries[0], carries[1])
    st = lax.while_loop(cond, body, st)
    o_ref[0] = jnp.where(lane < SB_DH, st[2], st[3]).astype(o_ref.dtype)


def _stick_breaking(q, k_own, v_own, k_past, v_past, prompt):
    b, lq, w = q.shape
    hp = w // LANES
    if prompt:
        tq = tkd = tk = min(SB_TILE, lq)
        blocks_per_q, n_past = 1, None
    else:
        tq = lq
        tkd = k_own.shape[1]
        tk = SB_TILE
        assert k_past.shape[1] % tk == 0
        blocks_per_q, n_past = 0, k_past.shape[1] // tk
    assert lq % tq == 0
    lp = k_past.shape[1]
    qspec = pl.BlockSpec((1, tq, LANES), lambda i, p, j: (i, j, p))
    dspec = pl.BlockSpec((1, tkd, LANES), lambda i, p, j: (i, j, p))
    pspec = pl.BlockSpec((1, lp, LANES), lambda i, p, j: (i, 0, p))
    md = _cum_matrix(tkd)
    mp = _cum_matrix(tk)
    const2 = lambda i, p, j: (0, 0)
    return pl.pallas_call(
        functools.partial(_sb_kernel, tq=tq, tkd=tkd, tk=tk, blocks_per_q=blocks_per_q, n_past=n_past),
        grid=(b, hp, lq // tq),
        in_specs=[qspec, dspec, dspec, pspec, pspec, pl.BlockSpec(md.shape, const2), pl.BlockSpec(mp.shape, const2)],
        out_specs=qspec,
        out_shape=jax.ShapeDtypeStruct((b, lq, w), BF16),
        compiler_params=_cparams("parallel", "parallel", "arbitrary"),
        name="stick_breaking",
    )(q, k_own, v_own, k_past, v_past, md, mp)


def _band_kernel(q_ref, k_ref, v_ref, bm_ref, o_ref, *, tq, span):
    i = pl.program_id(1)
    ws = pl.multiple_of(jnp.maximum(i * tq - (span - tq), 0), LANES)
    lane = _lane_iota((1, LANES))
    npair = BAND_W // LANES
    sls = [slice(pr * LANES, (pr + 1) * LANES) for pr in range(npair)]
    scores = []
    for pr in range(npair):
        kw = k_ref[0, pl.ds(ws, span), sls[pr]]
        q = q_ref[0, :, sls[pr]]
        zero = jnp.zeros_like(q)
        for h in range(2):
            qm = jnp.where((lane >= BAND_DH) if h else (lane < BAND_DH), q, zero)
            scores.append(_dot_nt(qm, kw) + bm_ref[0, 2 * pr + h])
    probs = [jnp.exp(s - jnp.max(s, axis=-1, keepdims=True)) for s in scores]
    pairs = []
    for pr in range(npair):
        vw = v_ref[0, pl.ds(ws, span), sls[pr]]
        outs = [_dot(p.astype(BF16), vw) / jnp.sum(p, axis=-1, keepdims=True) for p in probs[2 * pr:2 * pr + 2]]
        pairs.append(jnp.where(lane < BAND_DH, outs[0], outs[1]))
    o_ref[0] = jnp.concatenate(pairs, axis=1).astype(o_ref.dtype)


def _band_bias(rel_bias, q0, k0, tq, span, n_keys):
    d0 = q0 - k0
    n = span + tq
    padw = n + abs(d0)
    tab = rel_bias.astype(F32)
    heads = tab.shape[0]
    ext = jnp.concatenate([jnp.broadcast_to(tab[:, :1], (heads, padw)), tab,
                           jnp.broadcast_to(tab[:, -1:], (heads, padw))], axis=1)
    base = padw + REL_CLIP + d0
    g = jnp.concatenate([jnp.flip(ext[:, base - span + 1:base + 1], axis=1),
                         jnp.flip(ext[:, base + 1:base + tq + 1], axis=1)], axis=1)
    toe = jnp.tile(g, (1, tq))[:, :tq * (n - 1)].reshape(heads, tq, n - 1)[:, :, :span]
    q_pos = q0 + np.arange(tq)[:, None]
    k_pos = k0 + np.arange(span)[None, :]
    qc, kc = q_pos // CHUNK, k_pos // CHUNK
    mask = (np.arange(span)[None, :] < n_keys) & (k_pos >= 0) & (kc <= qc) & (kc >= qc - BAND_CHUNKS)
    return jnp.where(jnp.asarray(mask)[None], toe, NEG_BIG)


def _band(q, k, v, bm, tq, span):
    b, lq, w = q.shape
    nd, heads = bm.shape[:2]
    lk = k.shape[1]
    qspec = pl.BlockSpec((1, tq, w), lambda i, j: (i, j, 0))
    kspec = pl.BlockSpec((1, lk, w), lambda i, j: (i, 0, 0))
    bspec = pl.BlockSpec((1, heads, tq, span), lambda i, j: (jnp.minimum(j, nd - 1), 0, 0, 0))
    return pl.pallas_call(
        functools.partial(_band_kernel, tq=tq, span=span),
        grid=(b, lq // tq),
        in_specs=[qspec, kspec, kspec, bspec],
        out_specs=qspec,
        out_shape=jax.ShapeDtypeStruct((b, lq, w), BF16),
        compiler_params=_cparams("parallel", "arbitrary"),
        name="band",
    )(q, k, v, bm)


def _ssd_kernel(xbc_ref, z_ref, dt_ref, cw_ref, cb_ref, dtb_ref, alog_ref, dskip_ref, gain_ref, st0_ref,
                cbuf_ref, tri_ref, e_ref, y_ref, stn_ref, st_scr, xwin, *, tc, valid_len):
    c = pl.program_id(1)

    @pl.when(c == 0)
    def _():
        st_scr[...] = st0_ref[0]
        xwin[0:SUBLANES, :] = cbuf_ref[0]

    u = xbc_ref[0]
    xwin[SUBLANES:SUBLANES + tc, :] = u
    acc = cb_ref[...] + cw_ref[SSD_CONV - 1:SSD_CONV, :] * u
    for i in range(SSD_CONV - 1):
        off = SUBLANES - (SSD_CONV - 1) + i
        acc = acc + cw_ref[i:i + 1, :] * xwin[off:off + tc, :]
    xwin[0:SUBLANES, :] = u[tc - SUBLANES:, :]
    xc = _silu(acc)
    xs = xc[:, :SSD_INNER]

    rowid = c * tc + lax.broadcasted_iota(jnp.int32, (tc, LANES), 0)
    dt = jnp.where(rowid < valid_len, _softplus(dt_ref[0] + dtb_ref[...]), 0.0)
    da = dt * (-jnp.exp(alog_ref[...]))
    cum = _split_dot_left(tri_ref[...], da)
    dt_b = _split_dot(dt, e_ref[...])
    cum_b = _split_dot(cum, e_ref[...])
    last_b = cum_b[tc - 1:tc, :]
    xd = xs * dt_b
    xw = (xd * jnp.exp(last_b - cum_b)).astype(BF16)
    xd_bf = xd.astype(BF16)
    cum_t = cum.T

    lane = _lane_iota((1, LANES))
    ri = lax.broadcasted_iota(jnp.int32, (tc, tc), 0)
    ci = lax.broadcasted_iota(jnp.int32, (tc, tc), 1)
    causal = ri >= ci
    heads_per_group = SSD_HEADS // SSD_GROUPS
    gw = heads_per_group * SSD_P
    y_parts = []
    for g in range(SSD_GROUPS):
        bg = xc[:, SSD_INNER + g * SSD_N:SSD_INNER + (g + 1) * SSD_N].astype(BF16)
        cg = xc[:, SSD_INNER + (SSD_GROUPS + g) * SSD_N:SSD_INNER + (SSD_GROUPS + g + 1) * SSD_N].astype(BF16)
        cb = _dot_nt(cg, bg)
        gl = slice(g * gw, (g + 1) * gw)
        y_parts.append((g, _dot(cg, st_scr[:, gl].astype(BF16))))
        for pr in range(heads_per_group // 2):
            pidx = g * (heads_per_group // 2) + pr
            pl_ = slice(pidx * LANES, (pidx + 1) * LANES)
            cpair = cum_b[:, pl_]
            rolled = pltpu.roll(cpair, SSD_P, axis=1)
            pair_out = []
            for hh in range(2):
                h = 2 * pidx + hh
                colv = jnp.where((lane >= SSD_P) if hh else (lane < SSD_P), cpair, rolled)
                seg = jnp.exp(jnp.where(causal, colv - cum_t[h:h + 1, :], NEG_BIG))
                pair_out.append(_dot((cb * seg).astype(BF16), xd_bf[:, pl_]))
            y_parts.append((None, jnp.where(lane < SSD_P, pair_out[0], pair_out[1])))
        st_scr[:, gl] = st_scr[:, gl] * jnp.exp(last_b[:, gl]) + _dot_tn(bg, xw[:, gl])

    y_intra = jnp.concatenate([p for g, p in y_parts if g is None], axis=1)
    y_inter = jnp.concatenate([p for g, p in y_parts if g is not None], axis=1)
    y = y_intra + y_inter * jnp.exp(cum_b) + dskip_ref[...] * xs
    y = _rms(y * _silu(z_ref[0]), gain_ref[...])
    y_ref[0] = y.astype(y_ref.dtype)

    @pl.when(c == pl.num_programs(1) - 1)
    def _():
        stn_ref[0] = st_scr[...]


def _split_dot_left(m, x):
    hi = x.astype(BF16)
    lo = (x - hi.astype(F32)).astype(BF16)
    return _dot(m, hi) + _dot(m, lo)


def _ssd(xbc, z, dt, conv_w, conv_b, dt_bias, a_log, d_skip, gain, st0, cbuf, valid_len):
    b, l, _ = xbc.shape
    tc = SSD_TILE
    assert l % tc == 0
    pad = LANES - SSD_HEADS
    tri = (jnp.arange(tc)[:, None] >= jnp.arange(tc)[None, :]).astype(BF16)
    expand = (jnp.arange(LANES)[:, None] == (jnp.arange(SSD_INNER) // SSD_P)[None, :]).astype(BF16)
    blk = lambda w: pl.BlockSpec((1, tc, w), lambda i, c: (i, c, 0))
    const2 = lambda i, c: (0, 0)
    cs = lambda a: pl.BlockSpec(a.shape, const2)
    st_spec = pl.BlockSpec((1, SSD_N, SSD_INNER), lambda i, c: (i, 0, 0))
    params = [conv_w, conv_b.reshape(1, -1),
              jnp.pad(dt_bias.astype(F32), (0, pad)).reshape(1, LANES),
              jnp.pad(a_log.astype(F32), (0, pad)).reshape(1, LANES),
              jnp.repeat(d_skip.astype(F32), SSD_P).reshape(1, SSD_INNER),
              gain.reshape(1, SSD_INNER)]
    return pl.pallas_call(
        functools.partial(_ssd_kernel, tc=tc, valid_len=valid_len),
        grid=(b, l // tc),
        in_specs=[blk(SSD_CONV_DIM), blk(SSD_INNER), blk(LANES)] + [cs(p) for p in params]
                 + [st_spec, pl.BlockSpec((1, SUBLANES, SSD_CONV_DIM), lambda i, c: (i, 0, 0)), cs(tri), cs(expand)],
        out_specs=[blk(SSD_INNER), st_spec],
        out_shape=[jax.ShapeDtypeStruct((b, l, SSD_INNER), BF16),
                   jax.ShapeDtypeStruct((b, SSD_N, SSD_INNER), F32)],
        scratch_shapes=[pltpu.VMEM((SSD_N, SSD_INNER), F32), pltpu.VMEM((SUBLANES + tc, SSD_CONV_DIM), F32)],
        compiler_params=_cparams("parallel", "arbitrary"),
        name="ssd",
    )(xbc, z, dt, *params, st0, cbuf, tri, expand)


def _rot_tables(pos, rows):
    half = RET_DK // 2
    inv = jnp.power(ROPE_BASE, -jnp.arange(half, dtype=F32) / half)
    ang = pos.astype(F32)[:, None] * inv[None, :]
    cos = jnp.tile(jnp.cos(ang), (rows // pos.shape[0], 2 * RET_HEADS))
    sin = jnp.sin(ang)
    sin = jnp.tile(jnp.concatenate([-sin, sin], axis=1), (rows // pos.shape[0], RET_HEADS))
    return cos, sin


def _swap_halves(w, heads, dh):
    d = w.shape[0]
    return jnp.flip(w.reshape(d, heads, 2, dh // 2), axis=2).reshape(d, heads * dh)


def _ab_layer(x, b, l, pos0, g, w_ext, ret_gain, past):
    t = b * l
    rows = max(l, min(ROW_TILE, t))
    pos = pos0 + jnp.arange(l, dtype=jnp.int32)
    rot = _rot_tables(pos, rows)
    o = [0]
    for wdt in (RET_QK_W, RET_QK_W, RET_V_W, RET_V_W, SB_W, SB_W, SB_W):
        o.append(o[-1] + wdt)
    specs = [(o[0], RET_QK_W, HEAD_SCALE, o[7], (BF16,)), (o[1], RET_QK_W, 1.0, o[7] + RET_QK_W, (BF16,)),
             (o[2], RET_V_W, 1.0, None, (BF16,)), (o[3], RET_V_W, 1.0, None, (F32,)),
             (o[4], SB_W, HEAD_SCALE, None, (BF16,)), (o[5], SB_W, 1.0, None, (F32,)),
             (o[6], SB_W, 1.0, None, (F32,))]
    rq, rk, rv, rg, sq, sk, sv = _proj(x, g, w_ext, specs, rot)
    r3 = lambda a: a.reshape(b, l, a.shape[-1])
    if past is None:
        s0 = jnp.zeros((b, RET_QK_W, RET_DV), F32)
        tc = min(RET_TILE, l)
    else:
        s0 = past[0].astype(F32).reshape(b, RET_QK_W, RET_DV)
        tc = l
    ro, s_new = _retention(r3(rq), r3(rk), r3(rv), r3(rg), ret_gain, s0, tc)
    sk3, sv3 = r3(sk), r3(sv)
    if past is None:
        so = _stick_breaking(r3(sq), sk3, sv3, sk3, sv3, prompt=True)
    else:
        ck = past[1].reshape(b, -1, SB_W)
        cv = past[2].reshape(b, -1, SB_W)
        padn = (-l) % LANES
        so = _stick_breaking(r3(sq), jnp.pad(sk3, ((0, 0), (0, padn), (0, 0))),
                             jnp.pad(sv3, ((0, 0), (0, padn), (0, 0))), ck, cv, prompt=False)
    outs = (s_new.reshape(b, RET_HEADS, RET_DK, RET_DV), sk3.reshape(b, l, SB_HEADS, SB_DH),
            sv3.reshape(b, l, SB_HEADS, SB_DH))
    return ro.reshape(t, RET_V_W), so.reshape(t, SB_W), outs


def _cd_layer(x, b, l, pos0, g, w_ext, rel_bias, conv_w, conv_b, dt_bias, a_log, d_skip, ssd_gain, past):
    t = b * l
    o = [0]
    for wdt in (BAND_W, BAND_W, BAND_W, SSD_INNER, SSD_CONV_DIM, LANES):
        o.append(o[-1] + wdt)
    specs = [(o[0], BAND_W, HEAD_SCALE, None, (BF16,)), (o[1], BAND_W, 1.0, None, (F32, BF16)),
             (o[2], BAND_W, 1.0, None, (F32, BF16)), (o[3], SSD_INNER, 1.0, None, (F32,)),
             (o[4], SSD_CONV_DIM, 1.0, None, (F32,)), (o[5], LANES, 1.0, None, (F32,))]
    bq, bk, bk16, bv, bv16, z, xbc, dt = _proj(x, g, w_ext, specs)
    r3 = lambda a: a.reshape(b, l, a.shape[-1])
    bk3, bv3, xbc3 = r3(bk), r3(bv), r3(xbc)
    if past is None:
        tq = min(BAND_TILE, l)
        span = BAND_WINDOW + tq
        nd = BAND_WINDOW // tq + 1
        bm = jnp.stack([_band_bias(rel_bias, d * tq, max(d * tq - BAND_WINDOW, 0), tq, span, span)
                        for d in range(nd)])
        bo = _band(r3(bq), r3(bk16), r3(bv16), bm, tq, span)
        keep = min(BAND_WINDOW, l)
        new_k, new_v = bk3[:, l - keep:], bv3[:, l - keep:]
        st0 = jnp.zeros((b, SSD_N, SSD_INNER), F32)
        cbuf = jnp.zeros((b, SUBLANES, SSD_CONV_DIM), F32)
        lp = l
    else:
        ck = past[0].reshape(b, -1, BAND_W)
        cv = past[1].reshape(b, -1, BAND_W)
        wlen = ck.shape[1]
        span = -(-(wlen + l) // LANES) * LANES
        padk = span - wlen - l
        k_all = jnp.concatenate([ck.astype(BF16), r3(bk16), jnp.zeros((b, padk, BAND_W), BF16)], axis=1)
        v_all = jnp.concatenate([cv.astype(BF16), r3(bv16), jnp.zeros((b, padk, BAND_W), BF16)], axis=1)
        bm = _band_bias(rel_bias, pos0, pos0 - wlen, l, span, wlen + l)[None]
        bo = _band(r3(bq), k_all, v_all, bm, l, span)
        new_k, new_v = bk3, bv3
        st0 = jnp.transpose(past[2].astype(F32), (0, 3, 1, 2)).reshape(b, SSD_N, SSD_INNER)
        cbuf = jnp.pad(past[3].astype(F32), ((0, 0), (SUBLANES - (SSD_CONV - 1), 0), (0, 0)))
        lp = -(-l // SSD_TILE) * SSD_TILE
    assert l >= SSD_CONV - 1
    conv_new = xbc3[:, l - (SSD_CONV - 1):]
    padl = lp - l
    pad3 = lambda a: jnp.pad(a, ((0, 0), (0, padl), (0, 0))) if padl else a
    y, st_new = _ssd(pad3(xbc3), pad3(r3(z)), pad3(r3(dt)), conv_w, conv_b, dt_bias, a_log, d_skip, ssd_gain,
                     st0, cbuf, l)
    y = y[:, :l]
    ssm_new = jnp.transpose(st_new.reshape(b, SSD_N, SSD_HEADS, SSD_P), (0, 2, 3, 1))
    outs = (new_k.reshape(b, -1, BAND_HEADS, BAND_DH), new_v.reshape(b, -1, BAND_HEADS, BAND_DH), ssm_new, conv_new)
    return bo.reshape(t, BAND_W), y.reshape(t, SSD_INNER), outs


def _trunk(x3, pos0, wts, past):
    (norm_g, win, wout, final_g, w_ab, ret_gain, wo_ab, w_cd, rel_bias, conv_w, conv_b, dt_bias, a_log,
     d_skip, ssd_gain, wo_cd) = wts
    b, l, d = x3.shape
    depth = norm_g.shape[0]
    x = x3.reshape(b * l, d)
    outs = ([], [], [], [], [], [], [])
    x = _ffn(x, norm_g[0, 0], win[0, 0], wout[0, 0])
    for layer in range(depth):
        j = layer // 2
        if layer % 2 == 0:
            pst = None if past is None else (past[0][j], past[1][j], past[2][j])
            ma, mb, (r, k, v) = _ab_layer(x, b, l, pos0, norm_g[layer, 1], w_ab[j], ret_gain[j], pst)
            outs[0].append(r)
            outs[1].append(k)
            outs[2].append(v)
            wo = wo_ab[j]
        else:
            pst = None if past is None else (past[3][j], past[4][j], past[5][j], past[6][j])
            ma, mb, (k, v, s, cb) = _cd_layer(x, b, l, pos0, norm_g[layer, 1], w_cd[j], rel_bias[j], conv_w[j],
                                              conv_b[j], dt_bias[j], a_log[j], d_skip[j], ssd_gain[j], pst)
            outs[3].append(k)
            outs[4].append(v)
            outs[5].append(s)
            outs[6].append(cb)
            wo = wo_cd[j]
        half = ma.shape[1]
        mix = (ma, mb, wo[:half], wo[half:])
        x = _ffn(x, norm_g[layer, 2], win[layer, 1], wout[layer, 1], mix=mix,
                 final_g=final_g if layer + 1 == depth else None)
        if layer + 1 < depth:
            x = _ffn(x, norm_g[layer + 1, 0], win[layer + 1, 0], wout[layer + 1, 0])
    return x.reshape(b, l, d), tuple(jnp.stack(o) for o in outs)


def kernel(x_prompt, x_sample, state_ret, cache_sb_k, cache_sb_v, cache_band_k, cache_band_v, state_ssm, state_conv, norm_g, ffn_w_in, ffn_w_out, final_g, w_in_ab, ret_gain, w_out_ab, w_in_cd, rel_bias, conv_w, conv_b, dt_bias, a_log, d_skip, ssd_gain, w_out_cd):
    win = ffn_w_in.astype(BF16)
    wout = ffn_w_out.astype(BF16)
    w_ab = jnp.concatenate([w_in_ab, _swap3(w_in_ab[:, :, :RET_QK_W]), _swap3(w_in_ab[:, :, RET_QK_W:2 * RET_QK_W])],
                           axis=-1).astype(BF16)
    n_main = 3 * BAND_W + SSD_INNER + SSD_CONV_DIM
    w_cd = jnp.concatenate([w_in_cd, jnp.zeros(w_in_cd.shape[:2] + (LANES - SSD_HEADS,), w_in_cd.dtype)],
                           axis=-1).astype(BF16)
    assert w_cd.shape[-1] == n_main + LANES
    wts = (norm_g, win, wout, final_g, w_ab, ret_gain, w_out_ab.astype(BF16), w_cd, rel_bias, conv_w, conv_b,
           dt_bias, a_log, d_skip, ssd_gain, w_out_cd.astype(BF16))
    y_p, outs_p = _trunk(x_prompt, 0, wts, None)
    past = (state_ret, cache_sb_k, cache_sb_v, cache_band_k, cache_band_v, state_ssm, state_conv)
    y_s, outs_s = _trunk(x_sample, cache_sb_k.shape[2], wts, past)
    return (y_p, y_s) + outs_p + outs_s


def _swap3(w):
    return jnp.stack([_swap_halves(w[i], RET_HEADS, RET_DK) for i in range(w.shape[0])])
```

```python
import functools
import math

import jax
import jax.numpy as jnp
import numpy as np
from jax import lax
from jax.experimental import pallas as pl
from jax.experimental.pallas import tpu as pltpu

F32 = jnp.float32
BF16 = jnp.bfloat16

CHUNK = 64
EPS = 1e-6
ROPE_BASE = 10000.0
RET_HEADS, RET_DK, RET_DV = 4, 64, 128
SB_HEADS, SB_DH = 8, 64
BAND_HEADS, BAND_DH, BAND_CHUNKS, REL_CLIP = 8, 64, 8, 128
BAND_WINDOW = BAND_CHUNKS * CHUNK
SSD_HEADS, SSD_P, SSD_GROUPS, SSD_N, SSD_CONV = 8, 64, 2, 128, 4
SSD_INNER = SSD_HEADS * SSD_P
SSD_CONV_DIM = SSD_INNER + 2 * SSD_GROUPS * SSD_N
RET_QK_W = RET_HEADS * RET_DK
RET_V_W = RET_HEADS * RET_DV
SB_W = SB_HEADS * SB_DH
BAND_W = BAND_HEADS * BAND_DH
HEAD_SCALE = 0.125

LANES = 128
SUBLANES = 8
VMEM_LIMIT_BYTES = 56 * 1024 * 1024

NEG_BIG = -1e30

ROW_TILE = 512
FFN_CHUNK = 512
RET_TILE = 256
SB_TILE = 256
BAND_TILE = 128
SSD_TILE = 128


def _cparams(*sem):
    return pltpu.CompilerParams(dimension_semantics=sem, vmem_limit_bytes=VMEM_LIMIT_BYTES)


def _rms(x, g):
    return x * lax.rsqrt(jnp.mean(x * x, axis=-1, keepdims=True) + EPS) * g


def _silu(x):
    return x * jax.nn.sigmoid(x)


def _softplus(x):
    return jnp.maximum(x, 0.0) + jnp.log(1.0 + jnp.exp(-jnp.abs(x)))


def _dot(a, b):
    return jnp.dot(a, b, preferred_element_type=F32)


def _dot_nt(a, b):
    return lax.dot_general(a, b, (((1,), (1,)), ((), ())), preferred_element_type=F32)


def _dot_tn(a, b):
    return lax.dot_general(a, b, (((0,), (0,)), ((), ())), preferred_element_type=F32)


def _split_dot(x, m):
    hi = x.astype(BF16)
    lo = (x - hi.astype(F32)).astype(BF16)
    return _dot(hi, m) + _dot(lo, m)


def _lane_iota(shape):
    return lax.broadcasted_iota(jnp.int32, shape, len(shape) - 1)


def _ffn_kernel(*refs, f, has_mix, final):
    refs = list(refs)
    x_ref = refs.pop(0)
    if has_mix:
        ma_ref, mb_ref, wa_ref, wb_ref = refs[:4]
        refs = refs[4:]
    g_ref, win_ref, wout_ref = refs[:3]
    refs = refs[3:]
    if final:
        fg_ref = refs.pop(0)
    o_ref, x1_ref, a_ref = refs

    x = x_ref[...]
    if has_mix:
        x = x + _dot(ma_ref[...], wa_ref[...]) + _dot(mb_ref[...], wb_ref[...])
    x1_ref[...] = x
    h = _rms(x, g_ref[...]).astype(BF16)
    for f0 in range(0, f, FFN_CHUNK):
        f1 = min(f0 + FFN_CHUNK, f)
        gate = _dot(h, win_ref[:, f0:f1])
        up = _dot(h, win_ref[:, f + f0:f + f1])
        a_ref[:, f0:f1] = (_silu(gate) * up).astype(BF16)
    y = x1_ref[...] + 0.5 * _dot(a_ref[...], wout_ref[...])
    if final:
        y = _rms(y, fg_ref[...])
    o_ref[...] = y


def _resident(shape):
    return pl.BlockSpec(shape, lambda *_: (0,) * len(shape), pipeline_mode=pl.Buffered(1))


def _ffn(x, g, win, wout, mix=None, final_g=None):
    t, d = x.shape
    f = wout.shape[0]
    tm = min(ROW_TILE, t)
    assert t % tm == 0 and f % LANES == 0
    row = lambda i: (i, 0)
    args = [x]
    specs = [pl.BlockSpec((tm, d), row)]
    if mix is not None:
        ma, mb, wa, wb = mix
        args += [ma, mb, wa, wb]
        specs += [pl.BlockSpec((tm, ma.shape[1]), row), pl.BlockSpec((tm, mb.shape[1]), row),
                  _resident(wa.shape), _resident(wb.shape)]
    args += [g.reshape(1, d), win, wout]
    specs += [_resident((1, d)), _resident(win.shape), _resident(wout.shape)]
    if final_g is not None:
        args.append(final_g.reshape(1, d))
        specs.append(_resident((1, d)))
    return pl.pallas_call(
        functools.partial(_ffn_kernel, f=f, has_mix=mix is not None, final=final_g is not None),
        grid=(t // tm,),
        in_specs=specs,
        out_specs=pl.BlockSpec((tm, d), row),
        out_shape=jax.ShapeDtypeStruct((t, d), F32),
        scratch_shapes=[pltpu.VMEM((tm, d), F32), pltpu.VMEM((tm, f), BF16)],
        compiler_params=_cparams("parallel"),
        name="ffn",
    )(*args)


def _proj_kernel(*refs, specs, has_rot):
    refs = list(refs)
    x_ref, g_ref, w_ref = refs[:3]
    refs = refs[3:]
    if has_rot:
        cos_ref, sin_ref = refs[:2]
        refs = refs[2:]
    h = _rms(x_ref[...], g_ref[...]).astype(BF16)
    for c0, width, scale, c_swap, dts in specs:
        y = _dot(h, w_ref[:, c0:c0 + width])
        if c_swap is not None:
            y = y * cos_ref[...] + _dot(h, w_ref[:, c_swap:c_swap + width]) * sin_ref[...]
        if scale != 1.0:
            y = y * scale
        for _ in dts:
            o_ref = refs.pop(0)
            o_ref[...] = y.astype(o_ref.dtype)


def _proj(x, g, w, specs, rot=None):
    t, d = x.shape
    tm = min(ROW_TILE, t)
    assert t % tm == 0
    row = lambda i: (i, 0)
    const2 = lambda i: (0, 0)
    args = [x, g.reshape(1, d), w]
    in_specs = [pl.BlockSpec((tm, d), row), pl.BlockSpec((1, d), const2), _resident(w.shape)]
    outs = [(s[1], dt) for s in specs for dt in s[4]]
    if rot is not None:
        cos, sin = rot
        nrot = cos.shape[0] // tm
        rot_map = lambda i: (i % nrot, 0)
        args += [cos, sin]
        in_specs += [pl.BlockSpec((tm, cos.shape[1]), rot_map), pl.BlockSpec((tm, sin.shape[1]), rot_map)]
    return pl.pallas_call(
        functools.partial(_proj_kernel, specs=tuple(specs), has_rot=rot is not None),
        grid=(t // tm,),
        in_specs=in_specs,
        out_specs=[pl.BlockSpec((tm, wd), row) for wd, _ in outs],
        out_shape=[jax.ShapeDtypeStruct((t, wd), dt) for wd, dt in outs],
        compiler_params=_cparams("parallel"),
        name="proj",
    )(*args)


def _ret_kernel(q_ref, k_ref, v_ref, rg_ref, gain_ref, s0_ref, dm_ref, qd_ref, kd_ref, cdec_ref,
                o_ref, sn_ref, s_scr):
    c = pl.program_id(1)

    @pl.when(c == 0)
    def _():
        s_scr[...] = s0_ref[0]

    q = q_ref[0]
    k = k_ref[0]
    v = v_ref[0]
    qs = (q.astype(F32) * qd_ref[...]).astype(BF16)
    ks = (k.astype(F32) * kd_ref[...]).astype(BF16)
    lane = _lane_iota((1, RET_QK_W))
    s_all = s_scr[...]
    s_bf = s_all.astype(BF16)
    upd = _dot_tn(ks, v)
    zero = jnp.zeros_like(q)
    for h in range(RET_HEADS):
        hm = (lane >= h * RET_DK) & (lane < (h + 1) * RET_DK)
        vh = v[:, h * RET_DV:(h + 1) * RET_DV]
        att = _dot_nt(jnp.where(hm, q, zero), k) * dm_ref[h]
        o = _dot(att.astype(BF16), vh) + _dot(jnp.where(hm, qs, zero), s_bf)
        o = o * lax.rsqrt(jnp.mean(o * o, axis=-1, keepdims=True) + EPS)
        sl = slice(h * RET_DV, (h + 1) * RET_DV)
        o_ref[0, :, sl] = (o * gain_ref[:, sl] * _silu(rg_ref[0, :, sl])).astype(o_ref.dtype)
        rows = slice(h * RET_DK, (h + 1) * RET_DK)
        s_scr[rows, :] = s_all[rows, :] * cdec_ref[rows, :] + upd[rows, sl]

    @pl.when(c == pl.num_programs(1) - 1)
    def _():
        sn_ref[0] = s_scr[...]


def _retention(q, k, v, rg, gain, s0, tc):
    b, l, _ = q.shape
    assert l % tc == 0
    lg = jnp.log1p(-jnp.exp2(-5.0 - jnp.arange(RET_HEADS, dtype=F32)))
    idx = jnp.arange(tc, dtype=F32)
    diff = idx[:, None] - idx[None, :]
    dm = jnp.where(diff >= 0, jnp.exp(lg[:, None, None] * jnp.maximum(diff, 0.0)), 0.0)
    qd = jnp.repeat(jnp.exp(lg[None, :] * (idx[:, None] + 1.0)), RET_DK, axis=1)
    kd = jnp.repeat(jnp.exp(lg[None, :] * (tc - 1.0 - idx[:, None])), RET_DK, axis=1)
    cdec = jnp.broadcast_to(jnp.repeat(jnp.exp(lg * tc), RET_DK)[:, None], (RET_QK_W, RET_DV))
    blk = lambda w: pl.BlockSpec((1, tc, w), lambda i, c: (i, c, 0))
    const2 = lambda i, c: (0, 0)
    st_spec = pl.BlockSpec((1, RET_QK_W, RET_DV), lambda i, c: (i, 0, 0))
    return pl.pallas_call(
        _ret_kernel,
        grid=(b, l // tc),
        in_specs=[blk(RET_QK_W), blk(RET_QK_W), blk(RET_V_W), blk(RET_V_W),
                  pl.BlockSpec((1, RET_V_W), const2), st_spec,
                  pl.BlockSpec((RET_HEADS, tc, tc), lambda i, c: (0, 0, 0)),
                  pl.BlockSpec((tc, RET_QK_W), const2), pl.BlockSpec((tc, RET_QK_W), const2),
                  pl.BlockSpec((RET_QK_W, RET_DV), const2)],
        out_specs=[blk(RET_V_W), st_spec],
        out_shape=[jax.ShapeDtypeStruct((b, l, RET_V_W), BF16),
                   jax.ShapeDtypeStruct((b, RET_QK_W, RET_DV), F32)],
        scratch_shapes=[pltpu.VMEM((RET_QK_W, RET_DV), F32)],
        compiler_params=_cparams("parallel", "arbitrary"),
        name="retention",
    )(q, k, v, rg, gain.reshape(1, RET_V_W), s0, dm, qd, kd, cdec)


def _cum_matrix(n):
    tri = (jnp.arange(n)[:, None] >= jnp.arange(n)[None, :]).astype(BF16)
    return jnp.concatenate([tri, tri], axis=0)


LOG2E = 1.4426950408889634
SB_UNDERFLOW = -160.0


def _sb_block(qms, kbs, vbs, cum_m, carries, mask, transposed=False):
    if transposed:
        zs = [_dot(qm, kb) for qm, kb in zip(qms, kbs)]
    else:
        zs = [_dot_nt(qm, kb) for qm, kb in zip(qms, kbs)]
    nzs = [-z for z in zs]
    lgs = [jnp.minimum(nz, 0.0) - jnp.log2(1.0 + jnp.exp2(jnp.minimum(z, nz))) for z, nz in zip(zs, nzs)]
    if mask is not None:
        lgs = [jnp.where(mask, lg, 0.0) for lg in lgs]
    his = [lg.astype(BF16) for lg in lgs]
    los = [(lg - hi.astype(F32)).astype(BF16) for lg, hi in zip(lgs, his)]
    incls = [_dot(jnp.concatenate([hi, lo], axis=1), cum_m) for hi, lo in zip(his, los)]
    if carries is None:
        ws = [jnp.exp2(z + incl) for z, incl in zip(zs, incls)]
    else:
        ws = [jnp.exp2(z + incl + c) for z, incl, c in zip(zs, incls, carries)]
    if mask is not None:
        ws = [jnp.where(mask, w, 0.0) for w in ws]
    if transposed:
        outs = [_dot_nt(w.astype(BF16), vb) for w, vb in zip(ws, vbs)]
    else:
        outs = [_dot(w.astype(BF16), vb) for w, vb in zip(ws, vbs)]
    return outs, [incl[:, 0:1] for incl in incls]


SB_PAIRS = 2


def _sb_kernel(q_ref, kd_ref, vd_ref, kp_ref, vp_ref, md_ref, mp_ref, o_ref, *, tq, tkd, tk, blocks_per_q,
               n_past, past_t):
    i = pl.program_id(2)
    lane = _lane_iota((1, LANES))
    sls = [slice(p * LANES, (p + 1) * LANES) for p in range(SB_PAIRS)]
    nh = 2 * SB_PAIRS
    qms = []
    for sl in sls:
        q = q_ref[0, :, sl]
        zero = jnp.zeros_like(q)
        qms += [jnp.where(lane < SB_DH, q, zero), jnp.where(lane >= SB_DH, q, zero)]
    row = lax.broadcasted_iota(jnp.int32, (tq, tkd), 0)
    col = lax.broadcasted_iota(jnp.int32, (tq, tkd), 1)
    dmask = col < row
    nblk = i * blocks_per_q if n_past is None else n_past

    per_head = lambda xs: [x for x in xs for _ in range(2)]
    kds = per_head([kd_ref[0, :, sl].astype(BF16) for sl in sls])
    vds = per_head([vd_ref[0, :, sl].astype(BF16) for sl in sls])
    accs, carries = _sb_block(qms, kds, vds, md_ref[...], None, dmask)

    def live(cs):
        return jnp.max(functools.reduce(jnp.maximum, cs)) > SB_UNDERFLOW

    def cond(st):
        return (st[0] >= 0) & st[1]

    def body(st):
        j, accs, cs = st[0], st[2:2 + nh], st[2 + nh:]
        off = pl.multiple_of(j * tk, tk)
        if past_t:
            kbs = per_head([kp_ref[0, sl, pl.ds(off, tk)].astype(BF16) for sl in sls])
            vbs = per_head([vp_ref[0, sl, pl.ds(off, tk)].astype(BF16) for sl in sls])
        else:
            kbs = per_head([kp_ref[0, pl.ds(off, tk), sl].astype(BF16) for sl in sls])
            vbs = per_head([vp_ref[0, pl.ds(off, tk), sl].astype(BF16) for sl in sls])
        ds, ts = _sb_block(qms, kbs, vbs, mp_ref[...], cs, None, transposed=past_t)
        cs = [c + t for c, t in zip(cs, ts)]
        return (j - 1, live(cs), *[a + d for a, d in zip(accs, ds)], *cs)

    st = lax.while_loop(cond, body, (jnp.asarray(nblk - 1, jnp.int32), live(carries), *accs, *carries))
    pairs = [jnp.where(lane < SB_DH, st[2 + 2 * p], st[3 + 2 * p]) for p in range(SB_PAIRS)]
    o_ref[0] = jnp.concatenate(pairs, axis=1).astype(o_ref.dtype)


def _stick_breaking(q, k_own, v_own, k_past, v_past, prompt):
    b, lq, w = q.shape
    wb = SB_PAIRS * LANES
    if prompt:
        tq = tkd = tk = min(SB_TILE, lq)
        blocks_per_q, n_past = 1, None
        lp = k_past.shape[1]
        pspec = pl.BlockSpec((1, lp, wb), lambda i, p, j: (i, 0, p))
    else:
        tq = lq
        tkd = k_own.shape[1]
        tk = SB_TILE
        lp = k_past.shape[2]
        assert lp % tk == 0
        blocks_per_q, n_past = 0, lp // tk
        pspec = pl.BlockSpec((1, wb, lp), lambda i, p, j: (i, p, 0))
    assert lq % tq == 0
    qspec = pl.BlockSpec((1, tq, wb), lambda i, p, j: (i, j, p))
    dspec = pl.BlockSpec((1, tkd, wb), lambda i, p, j: (i, j, p))
    md = _cum_matrix(tkd)
    mp = _cum_matrix(tk)
    const2 = lambda i, p, j: (0, 0)
    return pl.pallas_call(
        functools.partial(_sb_kernel, tq=tq, tkd=tkd, tk=tk, blocks_per_q=blocks_per_q, n_past=n_past,
                          past_t=not prompt),
        grid=(b, w // wb, lq // tq),
        in_specs=[qspec, dspec, dspec, pspec, pspec, pl.BlockSpec(md.shape, const2), pl.BlockSpec(mp.shape, const2)],
        out_specs=qspec,
        out_shape=jax.ShapeDtypeStruct((b, lq, w), BF16),
        compiler_params=_cparams("parallel", "parallel", "arbitrary"),
        name="stick_breaking",
    )(q, k_own, v_own, k_past, v_past, md, mp)


def _band_kernel(q_ref, k_ref, v_ref, bm_ref, o_ref, *, tq, span):
    i = pl.program_id(1)
    ws = pl.multiple_of(jnp.maximum(i * tq - (span - tq), 0), LANES)
    lane = _lane_iota((1, LANES))
    npair = BAND_W // LANES
    sls = [slice(pr * LANES, (pr + 1) * LANES) for pr in range(npair)]
    scores = []
    for pr in range(npair):
        kw = k_ref[0, pl.ds(ws, span), sls[pr]]
        q = q_ref[0, :, sls[pr]]
        zero = jnp.zeros_like(q)
        for h in range(2):
            qm = jnp.where((lane >= BAND_DH) if h else (lane < BAND_DH), q, zero)
            scores.append(_dot_nt(qm, kw) + bm_ref[0, 2 * pr + h])
    probs = [jnp.exp(s - jnp.max(s, axis=-1, keepdims=True)) for s in scores]
    pairs = []
    for pr in range(npair):
        vw = v_ref[0, pl.ds(ws, span), sls[pr]]
        outs = [_dot(p.astype(BF16), vw) / jnp.sum(p, axis=-1, keepdims=True) for p in probs[2 * pr:2 * pr + 2]]
        pairs.append(jnp.where(lane < BAND_DH, outs[0], outs[1]))
    o_ref[0] = jnp.concatenate(pairs, axis=1).astype(o_ref.dtype)


def _band_bias(rel_bias, q0, k0, tq, span, n_keys):
    d0 = q0 - k0
    n = span + tq
    padw = n + abs(d0)
    tab = rel_bias.astype(F32)
    heads = tab.shape[0]
    ext = jnp.concatenate([jnp.broadcast_to(tab[:, :1], (heads, padw)), tab,
                           jnp.broadcast_to(tab[:, -1:], (heads, padw))], axis=1)
    base = padw + REL_CLIP + d0
    g = jnp.concatenate([jnp.flip(ext[:, base - span + 1:base + 1], axis=1),
                         jnp.flip(ext[:, base + 1:base + tq + 1], axis=1)], axis=1)
    toe = jnp.tile(g, (1, tq))[:, :tq * (n - 1)].reshape(heads, tq, n - 1)[:, :, :span]
    q_pos = q0 + np.arange(tq)[:, None]
    k_pos = k0 + np.arange(span)[None, :]
    qc, kc = q_pos // CHUNK, k_pos // CHUNK
    mask = (np.arange(span)[None, :] < n_keys) & (k_pos >= 0) & (kc <= qc) & (kc >= qc - BAND_CHUNKS)
    return jnp.where(jnp.asarray(mask)[None], toe, NEG_BIG)


def _band(q, k, v, bm, tq, span):
    b, lq, w = q.shape
    nd, heads = bm.shape[:2]
    lk = k.shape[1]
    qspec = pl.BlockSpec((1, tq, w), lambda i, j: (i, j, 0))
    kspec = pl.BlockSpec((1, lk, w), lambda i, j: (i, 0, 0))
    bspec = pl.BlockSpec((1, heads, tq, span), lambda i, j: (jnp.minimum(j, nd - 1), 0, 0, 0))
    return pl.pallas_call(
        functools.partial(_band_kernel, tq=tq, span=span),
        grid=(b, lq // tq),
        in_specs=[qspec, kspec, kspec, bspec],
        out_specs=qspec,
        out_shape=jax.ShapeDtypeStruct((b, lq, w), BF16),
        compiler_params=_cparams("parallel", "arbitrary"),
        name="band",
    )(q, k, v, bm)


def _ssd_kernel(xbc_ref, z_ref, dt_ref, cw_ref, cb_ref, dtb_ref, alog_ref, dskip_ref, gain_ref, st0_ref,
                cbuf_ref, tri_ref, e_ref, y_ref, stn_ref, st_scr, xwin, *, tc, valid_len):
    c = pl.program_id(1)

    @pl.when(c == 0)
    def _():
        st_scr[...] = st0_ref[0]
        xwin[0:SUBLANES, :] = cbuf_ref[0]

    u = xbc_ref[0]
    xwin[SUBLANES:SUBLANES + tc, :] = u
    acc = cb_ref[...] + cw_ref[SSD_CONV - 1:SSD_CONV, :] * u
    for i in range(SSD_CONV - 1):
        off = SUBLANES - (SSD_CONV - 1) + i
        acc = acc + cw_ref[i:i + 1, :] * xwin[off:off + tc, :]
    xwin[0:SUBLANES, :] = u[tc - SUBLANES:, :]
    xc = _silu(acc)
    xs = xc[:, :SSD_INNER]

    rowid = c * tc + lax.broadcasted_iota(jnp.int32, (tc, LANES), 0)
    dt = jnp.where(rowid < valid_len, _softplus(dt_ref[0] + dtb_ref[...]), 0.0)
    da = dt * (-jnp.exp(alog_ref[...]))
    cum = _split_dot_left(tri_ref[...], da)
    dt_b = _split_dot(dt, e_ref[...])
    cum_b = _split_dot(cum, e_ref[...])
    last_b = cum_b[tc - 1:tc, :]
    xd = xs * dt_b
    xw = (xd * jnp.exp(last_b - cum_b)).astype(BF16)
    xd_bf = xd.astype(BF16)
    cum_t = cum.T

    lane = _lane_iota((1, LANES))
    ri = lax.broadcasted_iota(jnp.int32, (tc, tc), 0)
    ci = lax.broadcasted_iota(jnp.int32, (tc, tc), 1)
    causal = ri >= ci
    heads_per_group = SSD_HEADS // SSD_GROUPS
    gw = heads_per_group * SSD_P
    gls = [slice(g * gw, (g + 1) * gw) for g in range(SSD_GROUPS)]
    bgs = [xc[:, SSD_INNER + g * SSD_N:SSD_INNER + (g + 1) * SSD_N].astype(BF16) for g in range(SSD_GROUPS)]
    cgs = [xc[:, SSD_INNER + (SSD_GROUPS + g) * SSD_N:SSD_INNER + (SSD_GROUPS + g + 1) * SSD_N].astype(BF16)
           for g in range(SSD_GROUPS)]
    cbs = [_dot_nt(cg, bg) for cg, bg in zip(cgs, bgs)]
    y_inter = jnp.concatenate([_dot(cg, st_scr[:, gl].astype(BF16)) for cg, gl in zip(cgs, gls)], axis=1)
    s_upd = [_dot_tn(bg, xw[:, gl]) for bg, gl in zip(bgs, gls)]
    atts = []
    for pidx in range(SSD_HEADS // 2):
        cpair = cum_b[:, pidx * LANES:(pidx + 1) * LANES]
        rolled = pltpu.roll(cpair, SSD_P, axis=1)
        for hh in range(2):
            h = 2 * pidx + hh
            colv = jnp.where((lane >= SSD_P) if hh else (lane < SSD_P), cpair, rolled)
            seg = jnp.exp(jnp.where(causal, colv - cum_t[h:h + 1, :], NEG_BIG))
            atts.append((cbs[h // heads_per_group] * seg).astype(BF16))
    pairs = []
    for pidx in range(SSD_HEADS // 2):
        xdp = xd_bf[:, pidx * LANES:(pidx + 1) * LANES]
        pairs.append(jnp.where(lane < SSD_P, _dot(atts[2 * pidx], xdp), _dot(atts[2 * pidx + 1], xdp)))
    y_intra = jnp.concatenate(pairs, axis=1)
    for gl, upd in zip(gls, s_upd):
        st_scr[:, gl] = st_scr[:, gl] * jnp.exp(last_b[:, gl]) + upd
    y = y_intra + y_inter * jnp.exp(cum_b) + dskip_ref[...] * xs
    y = _rms(y * _silu(z_ref[0]), gain_ref[...])
    y_ref[0] = y.astype(y_ref.dtype)

    @pl.when(c == pl.num_programs(1) - 1)
    def _():
        stn_ref[0] = st_scr[...]


def _split_dot_left(m, x):
    hi = x.astype(BF16)
    lo = (x - hi.astype(F32)).astype(BF16)
    return _dot(m, hi) + _dot(m, lo)


def _ssd(xbc, z, dt, conv_w, conv_b, dt_bias, a_log, d_skip, gain, st0, cbuf, valid_len):
    b, l, _ = xbc.shape
    tc = SSD_TILE
    assert l % tc == 0
    pad = LANES - SSD_HEADS
    tri = (jnp.arange(tc)[:, None] >= jnp.arange(tc)[None, :]).astype(BF16)
    expand = (jnp.arange(LANES)[:, None] == (jnp.arange(SSD_INNER) // SSD_P)[None, :]).astype(BF16)
    blk = lambda w: pl.BlockSpec((1, tc, w), lambda i, c: (i, c, 0))
    const2 = lambda i, c: (0, 0)
    cs = lambda a: pl.BlockSpec(a.shape, const2)
    st_spec = pl.BlockSpec((1, SSD_N, SSD_INNER), lambda i, c: (i, 0, 0))
    params = [conv_w, conv_b.reshape(1, -1),
              jnp.pad(dt_bias.astype(F32), (0, pad)).reshape(1, LANES),
              jnp.pad(a_log.astype(F32), (0, pad)).reshape(1, LANES),
              jnp.repeat(d_skip.astype(F32), SSD_P).reshape(1, SSD_INNER),
              gain.reshape(1, SSD_INNER)]
    return pl.pallas_call(
        functools.partial(_ssd_kernel, tc=tc, valid_len=valid_len),
        grid=(b, l // tc),
        in_specs=[blk(SSD_CONV_DIM), blk(SSD_INNER), blk(LANES)] + [cs(p) for p in params]
                 + [st_spec, pl.BlockSpec((1, SUBLANES, SSD_CONV_DIM), lambda i, c: (i, 0, 0)), cs(tri), cs(expand)],
        out_specs=[blk(SSD_INNER), st_spec],
        out_shape=[jax.ShapeDtypeStruct((b, l, SSD_INNER), BF16),
                   jax.ShapeDtypeStruct((b, SSD_N, SSD_INNER), F32)],
        scratch_shapes=[pltpu.VMEM((SSD_N, SSD_INNER), F32), pltpu.VMEM((SUBLANES + tc, SSD_CONV_DIM), F32)],
        compiler_params=_cparams("parallel", "arbitrary"),
        name="ssd",
    )(xbc, z, dt, *params, st0, cbuf, tri, expand)


def _rot_tables(pos, rows):
    half = RET_DK // 2
    inv = jnp.power(ROPE_BASE, -jnp.arange(half, dtype=F32) / half)
    ang = pos.astype(F32)[:, None] * inv[None, :]
    cos = jnp.tile(jnp.cos(ang), (rows // pos.shape[0], 2 * RET_HEADS))
    sin = jnp.sin(ang)
    sin = jnp.tile(jnp.concatenate([-sin, sin], axis=1), (rows // pos.shape[0], RET_HEADS))
    return cos, sin


def _swap_halves(w, heads, dh):
    d = w.shape[0]
    return jnp.flip(w.reshape(d, heads, 2, dh // 2), axis=2).reshape(d, heads * dh)


def _ab_layer(x, b, l, pos0, g, w_ext, ret_gain, past):
    t = b * l
    rows = max(l, min(ROW_TILE, t))
    pos = pos0 + jnp.arange(l, dtype=jnp.int32)
    rot = _rot_tables(pos, rows)
    o = [0]
    for wdt in (RET_QK_W, RET_QK_W, RET_V_W, RET_V_W, SB_W, SB_W, SB_W):
        o.append(o[-1] + wdt)
    specs = [(o[0], RET_QK_W, HEAD_SCALE, o[7], (BF16,)), (o[1], RET_QK_W, 1.0, o[7] + RET_QK_W, (BF16,)),
             (o[2], RET_V_W, 1.0, None, (BF16,)), (o[3], RET_V_W, 1.0, None, (F32,)),
             (o[4], SB_W, HEAD_SCALE * LOG2E, None, (BF16,)), (o[5], SB_W, 1.0, None, (F32,)),
             (o[6], SB_W, 1.0, None, (F32,))]
    rq, rk, rv, rg, sq, sk, sv = _proj(x, g, w_ext, specs, rot)
    r3 = lambda a: a.reshape(b, l, a.shape[-1])
    if past is None:
        s0 = jnp.zeros((b, RET_QK_W, RET_DV), F32)
        tc = min(RET_TILE, l)
    else:
        s0 = past[0].astype(F32).reshape(b, RET_QK_W, RET_DV)
        tc = l
    ro, s_new = _retention(r3(rq), r3(rk), r3(rv), r3(rg), ret_gain, s0, tc)
    sk3, sv3 = r3(sk), r3(sv)
    if past is None:
        so = _stick_breaking(r3(sq), sk3, sv3, sk3, sv3, prompt=True)
    else:
        ck = jnp.transpose(past[1], (0, 2, 3, 1)).reshape(b, SB_W, -1)
        cv = jnp.transpose(past[2], (0, 2, 3, 1)).reshape(b, SB_W, -1)
        padn = (-l) % LANES
        so = _stick_breaking(r3(sq), jnp.pad(sk3, ((0, 0), (0, padn), (0, 0))),
                             jnp.pad(sv3, ((0, 0), (0, padn), (0, 0))), ck, cv, prompt=False)
    outs = (s_new.reshape(b, RET_HEADS, RET_DK, RET_DV), sk3.reshape(b, l, SB_HEADS, SB_DH),
            sv3.reshape(b, l, SB_HEADS, SB_DH))
    return ro.reshape(t, RET_V_W), so.reshape(t, SB_W), outs


def _cd_layer(x, b, l, pos0, g, w_ext, rel_bias, conv_w, conv_b, dt_bias, a_log, d_skip, ssd_gain, past):
    t = b * l
    o = [0]
    for wdt in (BAND_W, BAND_W, BAND_W, SSD_INNER, SSD_CONV_DIM, LANES):
        o.append(o[-1] + wdt)
    specs = [(o[0], BAND_W, HEAD_SCALE, None, (BF16,)), (o[1], BAND_W, 1.0, None, (F32, BF16)),
             (o[2], BAND_W, 1.0, None, (F32, BF16)), (o[3], SSD_INNER, 1.0, None, (F32,)),
             (o[4], SSD_CONV_DIM, 1.0, None, (F32,)), (o[5], LANES, 1.0, None, (F32,))]
    bq, bk, bk16, bv, bv16, z, xbc, dt = _proj(x, g, w_ext, specs)
    r3 = lambda a: a.reshape(b, l, a.shape[-1])
    bk3, bv3, xbc3 = r3(bk), r3(bv), r3(xbc)
    if past is None:
        tq = min(BAND_TILE, l)
        span = BAND_WINDOW + tq
        nd = BAND_WINDOW // tq + 1
        bm = jnp.stack([_band_bias(rel_bias, d * tq, max(d * tq - BAND_WINDOW, 0), tq, span, span)
                        for d in range(nd)])
        bo = _band(r3(bq), r3(bk16), r3(bv16), bm, tq, span)
        keep = min(BAND_WINDOW, l)
        new_k, new_v = bk3[:, l - keep:], bv3[:, l - keep:]
        st0 = jnp.zeros((b, SSD_N, SSD_INNER), F32)
        cbuf = jnp.zeros((b, SUBLANES, SSD_CONV_DIM), F32)
        lp = l
    else:
        ck = past[0].reshape(b, -1, BAND_W)
        cv = past[1].reshape(b, -1, BAND_W)
        wlen = ck.shape[1]
        span = -(-(wlen + l) // LANES) * LANES
        padk = span - wlen - l
        k_all = jnp.concatenate([ck.astype(BF16), r3(bk16), jnp.zeros((b, padk, BAND_W), BF16)], axis=1)
        v_all = jnp.concatenate([cv.astype(BF16), r3(bv16), jnp.zeros((b, padk, BAND_W), BF16)], axis=1)
        bm = _band_bias(rel_bias, pos0, pos0 - wlen, l, span, wlen + l)[None]
        bo = _band(r3(bq), k_all, v_all, bm, l, span)
        new_k, new_v = bk3, bv3
        st0 = jnp.transpose(past[2].astype(F32), (0, 3, 1, 2)).reshape(b, SSD_N, SSD_INNER)
        cbuf = jnp.pad(past[3].astype(F32), ((0, 0), (SUBLANES - (SSD_CONV - 1), 0), (0, 0)))
        lp = -(-l // SSD_TILE) * SSD_TILE
    assert l >= SSD_CONV - 1
    conv_new = xbc3[:, l - (SSD_CONV - 1):]
    padl = lp - l
    pad3 = lambda a: jnp.pad(a, ((0, 0), (0, padl), (0, 0))) if padl else a
    y, st_new = _ssd(pad3(xbc3), pad3(r3(z)), pad3(r3(dt)), conv_w, conv_b, dt_bias, a_log, d_skip, ssd_gain,
                     st0, cbuf, l)
    y = y[:, :l]
    ssm_new = jnp.transpose(st_new.reshape(b, SSD_N, SSD_HEADS, SSD_P), (0, 2, 3, 1))
    outs = (new_k.reshape(b, -1, BAND_HEADS, BAND_DH), new_v.reshape(b, -1, BAND_HEADS, BAND_DH), ssm_new, conv_new)
    return bo.reshape(t, BAND_W), y.reshape(t, SSD_INNER), outs


def _trunk(x3, pos0, wts, past):
    (norm_g, win, wout, final_g, w_ab, ret_gain, wo_ab, w_cd, rel_bias, conv_w, conv_b, dt_bias, a_log,
     d_skip, ssd_gain, wo_cd) = wts
    b, l, d = x3.shape
    depth = norm_g.shape[0]
    x = x3.reshape(b * l, d)
    outs = ([], [], [], [], [], [], [])
    x = _ffn(x, norm_g[0, 0], win[0, 0], wout[0, 0])
    for layer in range(depth):
        j = layer // 2
        if layer % 2 == 0:
            pst = None if past is None else (past[0][j], past[1][j], past[2][j])
            ma, mb, (r, k, v) = _ab_layer(x, b, l, pos0, norm_g[layer, 1], w_ab[j], ret_gain[j], pst)
            outs[0].append(r)
            outs[1].append(k)
            outs[2].append(v)
            wo = wo_ab[j]
        else:
            pst = None if past is None else (past[3][j], past[4][j], past[5][j], past[6][j])
            ma, mb, (k, v, s, cb) = _cd_layer(x, b, l, pos0, norm_g[layer, 1], w_cd[j], rel_bias[j], conv_w[j],
                                              conv_b[j], dt_bias[j], a_log[j], d_skip[j], ssd_gain[j], pst)
            outs[3].append(k)
            outs[4].append(v)
            outs[5].append(s)
            outs[6].append(cb)
            wo = wo_cd[j]
        half = ma.shape[1]
        mix = (ma, mb, wo[:half], wo[half:])
        x = _ffn(x, norm_g[layer, 2], win[layer, 1], wout[layer, 1], mix=mix,
                 final_g=final_g if layer + 1 == depth else None)
        if layer + 1 < depth:
            x = _ffn(x, norm_g[layer + 1, 0], win[layer + 1, 0], wout[layer + 1, 0])
    return x.reshape(b, l, d), tuple(jnp.stack(o) for o in outs)


def kernel(x_prompt, x_sample, state_ret, cache_sb_k, cache_sb_v, cache_band_k, cache_band_v, state_ssm, state_conv, norm_g, ffn_w_in, ffn_w_out, final_g, w_in_ab, ret_gain, w_out_ab, w_in_cd, rel_bias, conv_w, conv_b, dt_bias, a_log, d_skip, ssd_gain, w_out_cd):
    win = ffn_w_in.astype(BF16)
    wout = ffn_w_out.astype(BF16)
    w_ab = jnp.concatenate([w_in_ab, _swap3(w_in_ab[:, :, :RET_QK_W]), _swap3(w_in_ab[:, :, RET_QK_W:2 * RET_QK_W])],
                           axis=-1).astype(BF16)
    n_main = 3 * BAND_W + SSD_INNER + SSD_CONV_DIM
    w_cd = jnp.concatenate([w_in_cd, jnp.zeros(w_in_cd.shape[:2] + (LANES - SSD_HEADS,), w_in_cd.dtype)],
                           axis=-1).astype(BF16)
    assert w_cd.shape[-1] == n_main + LANES
    wts = (norm_g, win, wout, final_g, w_ab, ret_gain, w_out_ab.astype(BF16), w_cd, rel_bias, conv_w, conv_b,
           dt_bias, a_log, d_skip, ssd_gain, w_out_cd.astype(BF16))
    y_p, outs_p = _trunk(x_prompt, 0, wts, None)
    past = (state_ret, cache_sb_k, cache_sb_v, cache_band_k, cache_band_v, state_ssm, state_conv)
    y_s, outs_s = _trunk(x_sample, cache_sb_k.shape[2], wts, past)
    return (y_p, y_s) + outs_p + outs_s


def _swap3(w):
    return jnp.stack([_swap_halves(w[i], RET_HEADS, RET_DK) for i in range(w.shape[0])])
```

```python
import functools
import math

import jax
import jax.numpy as jnp
import numpy as np
from jax import lax
from jax.experimental import pallas as pl
from jax.experimental.pallas import tpu as pltpu

F32 = jnp.float32
BF16 = jnp.bfloat16

CHUNK = 64
EPS = 1e-6
ROPE_BASE = 10000.0
RET_HEADS, RET_DK, RET_DV = 4, 64, 128
SB_HEADS, SB_DH = 8, 64
BAND_HEADS, BAND_DH, BAND_CHUNKS, REL_CLIP = 8, 64, 8, 128
BAND_WINDOW = BAND_CHUNKS * CHUNK
SSD_HEADS, SSD_P, SSD_GROUPS, SSD_N, SSD_CONV = 8, 64, 2, 128, 4
SSD_INNER = SSD_HEADS * SSD_P
SSD_CONV_DIM = SSD_INNER + 2 * SSD_GROUPS * SSD_N
RET_QK_W = RET_HEADS * RET_DK
RET_V_W = RET_HEADS * RET_DV
SB_W = SB_HEADS * SB_DH
BAND_W = BAND_HEADS * BAND_DH
HEAD_SCALE = 0.125

LANES = 128
SUBLANES = 8
VMEM_LIMIT_BYTES = 56 * 1024 * 1024

NEG_BIG = -1e30

ROW_TILE = 1024
FFN_CHUNK = 512
RET_TILE = 256
SB_TILE = 256
BAND_TILE = 128
SSD_TILE = 128


def _cparams(*sem):
    return pltpu.CompilerParams(dimension_semantics=sem, vmem_limit_bytes=VMEM_LIMIT_BYTES)


def _rms(x, g):
    return x * lax.rsqrt(jnp.mean(x * x, axis=-1, keepdims=True) + EPS) * g


def _silu(x):
    return x * jax.nn.sigmoid(x)


def _softplus(x):
    return jnp.maximum(x, 0.0) + jnp.log(1.0 + jnp.exp(-jnp.abs(x)))


def _dot(a, b):
    return jnp.dot(a, b, preferred_element_type=F32)


def _dot_nt(a, b):
    return lax.dot_general(a, b, (((1,), (1,)), ((), ())), preferred_element_type=F32)


def _dot_tn(a, b):
    return lax.dot_general(a, b, (((0,), (0,)), ((), ())), preferred_element_type=F32)


def _split_dot(x, m):
    hi = x.astype(BF16)
    lo = (x - hi.astype(F32)).astype(BF16)
    return _dot(hi, m) + _dot(lo, m)


def _lane_iota(shape):
    return lax.broadcasted_iota(jnp.int32, shape, len(shape) - 1)


def _ffn_kernel(*refs, f, has_mix, final):
    refs = list(refs)
    x_ref = refs.pop(0)
    if has_mix:
        ma_ref, mb_ref, wa_ref, wb_ref = refs[:4]
        refs = refs[4:]
    g_ref, win_ref, wout_ref = refs[:3]
    refs = refs[3:]
    if final:
        fg_ref = refs.pop(0)
    o_ref, x1_ref, a_ref = refs

    x = x_ref[...]
    if has_mix:
        x = x + _dot(ma_ref[...], wa_ref[...]) + _dot(mb_ref[...], wb_ref[...])
    x1_ref[...] = x
    h = _rms(x, g_ref[...]).astype(BF16)
    for f0 in range(0, f, FFN_CHUNK):
        f1 = min(f0 + FFN_CHUNK, f)
        gate = _dot(h, win_ref[:, f0:f1])
        up = _dot(h, win_ref[:, f + f0:f + f1])
        a_ref[:, f0:f1] = (_silu(gate) * up).astype(BF16)
    y = x1_ref[...] + 0.5 * _dot(a_ref[...], wout_ref[...])
    if final:
        y = _rms(y, fg_ref[...])
    o_ref[...] = y


def _resident(shape):
    return pl.BlockSpec(shape, lambda *_: (0,) * len(shape), pipeline_mode=pl.Buffered(1))


def _ffn(x, g, win, wout, mix=None, final_g=None):
    t, d = x.shape
    f = wout.shape[0]
    tm = min(ROW_TILE, t)
    assert t % tm == 0 and f % LANES == 0
    row = lambda i: (i, 0)
    args = [x]
    specs = [pl.BlockSpec((tm, d), row)]
    if mix is not None:
        ma, mb, wa, wb = mix
        args += [ma, mb, wa, wb]
        specs += [pl.BlockSpec((tm, ma.shape[1]), row), pl.BlockSpec((tm, mb.shape[1]), row),
                  _resident(wa.shape), _resident(wb.shape)]
    args += [g.reshape(1, d), win, wout]
    specs += [_resident((1, d)), _resident(win.shape), _resident(wout.shape)]
    if final_g is not None:
        args.append(final_g.reshape(1, d))
        specs.append(_resident((1, d)))
    return pl.pallas_call(
        functools.partial(_ffn_kernel, f=f, has_mix=mix is not None, final=final_g is not None),
        grid=(t // tm,),
        in_specs=specs,
        out_specs=pl.BlockSpec((tm, d), row),
        out_shape=jax.ShapeDtypeStruct((t, d), F32),
        scratch_shapes=[pltpu.VMEM((tm, d), F32), pltpu.VMEM((tm, f), BF16)],
        compiler_params=_cparams("parallel"),
        name="ffn",
    )(*args)


def _proj_kernel(*refs, specs, has_rot):
    refs = list(refs)
    x_ref, g_ref, w_ref = refs[:3]
    refs = refs[3:]
    if has_rot:
        cos_ref, sin_ref = refs[:2]
        refs = refs[2:]
    h = _rms(x_ref[...], g_ref[...]).astype(BF16)
    for c0, width, scale, c_swap, dts in specs:
        y = _dot(h, w_ref[:, c0:c0 + width])
        if c_swap is not None:
            y = y * cos_ref[...] + _dot(h, w_ref[:, c_swap:c_swap + width]) * sin_ref[...]
        if scale != 1.0:
            y = y * scale
        for _ in dts:
            o_ref = refs.pop(0)
            o_ref[...] = y.astype(o_ref.dtype)


def _proj(x, g, w, specs, rot=None):
    t, d = x.shape
    tm = min(ROW_TILE, t)
    assert t % tm == 0
    row = lambda i: (i, 0)
    const2 = lambda i: (0, 0)
    args = [x, g.reshape(1, d), w]
    in_specs = [pl.BlockSpec((tm, d), row), pl.BlockSpec((1, d), const2), _resident(w.shape)]
    outs = [(s[1], dt) for s in specs for dt in s[4]]
    if rot is not None:
        cos, sin = rot
        nrot = cos.shape[0] // tm
        rot_map = lambda i: (i % nrot, 0)
        args += [cos, sin]
        in_specs += [pl.BlockSpec((tm, cos.shape[1]), rot_map), pl.BlockSpec((tm, sin.shape[1]), rot_map)]
    return pl.pallas_call(
        functools.partial(_proj_kernel, specs=tuple(specs), has_rot=rot is not None),
        grid=(t // tm,),
        in_specs=in_specs,
        out_specs=[pl.BlockSpec((tm, wd), row) for wd, _ in outs],
        out_shape=[jax.ShapeDtypeStruct((t, wd), dt) for wd, dt in outs],
        compiler_params=_cparams("parallel"),
        name="proj",
    )(*args)


def _ret_kernel(q_ref, k_ref, v_ref, rg_ref, gain_ref, s0_ref, dm_ref, qd_ref, kd_ref, cdec_ref,
                o_ref, sn_ref, s_scr):
    c = pl.program_id(1)

    @pl.when(c == 0)
    def _():
        s_scr[...] = s0_ref[0]

    q = q_ref[0]
    k = k_ref[0]
    v = v_ref[0]
    qs = (q.astype(F32) * qd_ref[...]).astype(BF16)
    ks = (k.astype(F32) * kd_ref[...]).astype(BF16)
    lane = _lane_iota((1, RET_QK_W))
    s_all = s_scr[...]
    s_bf = s_all.astype(BF16)
    upd = _dot_tn(ks, v)
    zero = jnp.zeros_like(q)
    heads = range(RET_HEADS)
    hms = [(lane >= h * RET_DK) & (lane < (h + 1) * RET_DK) for h in heads]
    sls = [slice(h * RET_DV, (h + 1) * RET_DV) for h in heads]
    atts = [_dot_nt(jnp.where(hm, q, zero), k) * dm_ref[h] for h, hm in zip(heads, hms)]
    cross = [_dot(jnp.where(hm, qs, zero), s_bf) for hm in hms]
    outs = [_dot(att.astype(BF16), v[:, sl]) + cr for att, sl, cr in zip(atts, sls, cross)]
    outs = [o * lax.rsqrt(jnp.mean(o * o, axis=-1, keepdims=True) + EPS) for o in outs]
    o = jnp.concatenate(outs, axis=1)
    o_ref[0] = (o * gain_ref[...] * _silu(rg_ref[0])).astype(o_ref.dtype)
    for h, sl in zip(heads, sls):
        rows = slice(h * RET_DK, (h + 1) * RET_DK)
        s_scr[rows, :] = s_all[rows, :] * cdec_ref[rows, :] + upd[rows, sl]

    @pl.when(c == pl.num_programs(1) - 1)
    def _():
        sn_ref[0] = s_scr[...]


def _retention(q, k, v, rg, gain, s0, tc):
    b, l, _ = q.shape
    assert l % tc == 0
    f32 = np.float32
    lg = np.log1p(-np.exp2(-5.0 - np.arange(RET_HEADS, dtype=f32))).astype(f32)
    idx = np.arange(tc, dtype=f32)
    diff = idx[:, None] - idx[None, :]
    dm = np.where(diff >= 0, np.exp(lg[:, None, None] * np.maximum(diff, 0.0)), 0.0).astype(f32)
    qd = np.repeat(np.exp(lg[None, :] * (idx[:, None] + 1.0)), RET_DK, axis=1).astype(f32)
    kd = np.repeat(np.exp(lg[None, :] * (tc - 1.0 - idx[:, None])), RET_DK, axis=1).astype(f32)
    cdec = np.broadcast_to(np.repeat(np.exp(lg * f32(tc)), RET_DK)[:, None], (RET_QK_W, RET_DV)).astype(f32)
    blk = lambda w: pl.BlockSpec((1, tc, w), lambda i, c: (i, c, 0))
    const2 = lambda i, c: (0, 0)
    st_spec = pl.BlockSpec((1, RET_QK_W, RET_DV), lambda i, c: (i, 0, 0))
    return pl.pallas_call(
        _ret_kernel,
        grid=(b, l // tc),
        in_specs=[blk(RET_QK_W), blk(RET_QK_W), blk(RET_V_W), blk(RET_V_W),
                  pl.BlockSpec((1, RET_V_W), const2), st_spec,
                  pl.BlockSpec((RET_HEADS, tc, tc), lambda i, c: (0, 0, 0)),
                  pl.BlockSpec((tc, RET_QK_W), const2), pl.BlockSpec((tc, RET_QK_W), const2),
                  pl.BlockSpec((RET_QK_W, RET_DV), const2)],
        out_specs=[blk(RET_V_W), st_spec],
        out_shape=[jax.ShapeDtypeStruct((b, l, RET_V_W), BF16),
                   jax.ShapeDtypeStruct((b, RET_QK_W, RET_DV), F32)],
        scratch_shapes=[pltpu.VMEM((RET_QK_W, RET_DV), F32)],
        compiler_params=_cparams("parallel", "arbitrary"),
        name="retention",
    )(q, k, v, rg, gain.reshape(1, RET_V_W), s0, dm, qd, kd, cdec)


def _cum_matrix(n):
    tri = (np.arange(n)[:, None] >= np.arange(n)[None, :]).astype(BF16)
    return np.concatenate([tri, tri], axis=0)


LOG2E = 1.4426950408889634
SB_UNDERFLOW = -160.0


def _sb_block(qms, kbs, vbs, cum_m, carries, mask, transposed=False):
    if transposed:
        zs = [_dot(qm, kb) for qm, kb in zip(qms, kbs)]
    else:
        zs = [_dot_nt(qm, kb) for qm, kb in zip(qms, kbs)]
    nzs = [-z for z in zs]
    lgs = [jnp.minimum(nz, 0.0) - jnp.log2(1.0 + jnp.exp2(jnp.minimum(z, nz))) for z, nz in zip(zs, nzs)]
    if mask is not None:
        lgs = [jnp.where(mask, lg, 0.0) for lg in lgs]
    his = [lg.astype(BF16) for lg in lgs]
    los = [(lg - hi.astype(F32)).astype(BF16) for lg, hi in zip(lgs, his)]
    incls = [_dot(jnp.concatenate([hi, lo], axis=1), cum_m) for hi, lo in zip(his, los)]
    if carries is None:
        ws = [jnp.exp2(z + incl) for z, incl in zip(zs, incls)]
    else:
        ws = [jnp.exp2(z + incl + c) for z, incl, c in zip(zs, incls, carries)]
    if mask is not None:
        ws = [jnp.where(mask, w, 0.0) for w in ws]
    if transposed:
        outs = [_dot_nt(w.astype(BF16), vb) for w, vb in zip(ws, vbs)]
    else:
        outs = [_dot(w.astype(BF16), vb) for w, vb in zip(ws, vbs)]
    return outs, [incl[:, 0:1] for incl in incls]


SB_PAIRS = 2


def _sb_kernel(q_ref, kd_ref, vd_ref, kp_ref, vp_ref, md_ref, mp_ref, o_ref, *, tq, tkd, tk, blocks_per_q,
               n_past, past_t):
    i = pl.program_id(2)
    lane = _lane_iota((1, LANES))
    sls = [slice(p * LANES, (p + 1) * LANES) for p in range(SB_PAIRS)]
    nh = 2 * SB_PAIRS
    qms = []
    for sl in sls:
        q = q_ref[0, :, sl]
        zero = jnp.zeros_like(q)
        qms += [jnp.where(lane < SB_DH, q, zero), jnp.where(lane >= SB_DH, q, zero)]
    row = lax.broadcasted_iota(jnp.int32, (tq, tkd), 0)
    col = lax.broadcasted_iota(jnp.int32, (tq, tkd), 1)
    dmask = col < row
    nblk = i * blocks_per_q if n_past is None else n_past

    per_head = lambda xs: [x for x in xs for _ in range(2)]
    kds = per_head([kd_ref[0, :, sl].astype(BF16) for sl in sls])
    vds = per_head([vd_ref[0, :, sl].astype(BF16) for sl in sls])
    accs, carries = _sb_block(qms, kds, vds, md_ref[...], None, dmask)

    def live(cs):
        return jnp.max(functools.reduce(jnp.maximum, cs)) > SB_UNDERFLOW

    def cond(st):
        return (st[0] >= 0) & st[1]

    def body(st):
        j, accs, cs = st[0], st[2:2 + nh], st[2 + nh:]
        off = pl.multiple_of(j * tk, tk)
        if past_t:
            kbs = per_head([kp_ref[0, sl, pl.ds(off, tk)].astype(BF16) for sl in sls])
            vbs = per_head([vp_ref[0, sl, pl.ds(off, tk)].astype(BF16) for sl in sls])
        else:
            kbs = per_head([kp_ref[0, pl.ds(off, tk), sl].astype(BF16) for sl in sls])
            vbs = per_head([vp_ref[0, pl.ds(off, tk), sl].astype(BF16) for sl in sls])
        ds, ts = _sb_block(qms, kbs, vbs, mp_ref[...], cs, None, transposed=past_t)
        cs = [c + t for c, t in zip(cs, ts)]
        return (j - 1, live(cs), *[a + d for a, d in zip(accs, ds)], *cs)

    st = lax.while_loop(cond, body, (jnp.asarray(nblk - 1, jnp.int32), live(carries), *accs, *carries))
    pairs = [jnp.where(lane < SB_DH, st[2 + 2 * p], st[3 + 2 * p]) for p in range(SB_PAIRS)]
    o_ref[0] = jnp.concatenate(pairs, axis=1).astype(o_ref.dtype)


def _stick_breaking(q, k_own, v_own, k_past, v_past, prompt, layer=0):
    b, lq, w = q.shape
    wb = SB_PAIRS * LANES
    if prompt:
        tq = tkd = tk = min(SB_TILE, lq)
        blocks_per_q, n_past = 1, None
        lp = k_past.shape[1]
        pspec = pl.BlockSpec((1, lp, wb), lambda i, p, j: (i, 0, p))
    else:
        tq = lq
        tkd = k_own.shape[1]
        tk = SB_TILE
        lp = k_past.shape[3]
        assert lp % tk == 0
        blocks_per_q, n_past = 0, lp // tk
        pspec = pl.BlockSpec((None, 1, wb, lp), lambda i, p, j: (layer, i, p, 0))
    assert lq % tq == 0
    qspec = pl.BlockSpec((1, tq, wb), lambda i, p, j: (i, j, p))
    dspec = pl.BlockSpec((1, tkd, wb), lambda i, p, j: (i, j, p))
    md = _cum_matrix(tkd)
    mp = _cum_matrix(tk)
    const2 = lambda i, p, j: (0, 0)
    return pl.pallas_call(
        functools.partial(_sb_kernel, tq=tq, tkd=tkd, tk=tk, blocks_per_q=blocks_per_q, n_past=n_past,
                          past_t=not prompt),
        grid=(b, w // wb, lq // tq),
        in_specs=[qspec, dspec, dspec, pspec, pspec, pl.BlockSpec(md.shape, const2), pl.BlockSpec(mp.shape, const2)],
        out_specs=qspec,
        out_shape=jax.ShapeDtypeStruct((b, lq, w), BF16),
        compiler_params=_cparams("parallel", "parallel", "arbitrary"),
        name="stick_breaking",
    )(q, k_own, v_own, k_past, v_past, md, mp)


def _band_kernel(q_ref, k_ref, v_ref, bm_ref, o_ref, *, tq, span):
    i = pl.program_id(1)
    ws = pl.multiple_of(jnp.maximum(i * tq - (span - tq), 0), LANES)
    lane = _lane_iota((1, LANES))
    npair = BAND_W // LANES
    sls = [slice(pr * LANES, (pr + 1) * LANES) for pr in range(npair)]
    scores = []
    for pr in range(npair):
        kw = k_ref[0, pl.ds(ws, span), sls[pr]]
        q = q_ref[0, :, sls[pr]]
        zero = jnp.zeros_like(q)
        for h in range(2):
            qm = jnp.where((lane >= BAND_DH) if h else (lane < BAND_DH), q, zero)
            scores.append(_dot_nt(qm, kw) + bm_ref[0, 2 * pr + h])
    probs = [jnp.exp(s - jnp.max(s, axis=-1, keepdims=True)) for s in scores]
    pairs = []
    for pr in range(npair):
        vw = v_ref[0, pl.ds(ws, span), sls[pr]]
        outs = [_dot(p.astype(BF16), vw) / jnp.sum(p, axis=-1, keepdims=True) for p in probs[2 * pr:2 * pr + 2]]
        pairs.append(jnp.where(lane < BAND_DH, outs[0], outs[1]))
    o_ref[0] = jnp.concatenate(pairs, axis=1).astype(o_ref.dtype)


def _band_bias(rel_bias, q0, k0, tq, span, n_keys):
    d0 = q0 - k0
    n = span + tq
    padw = n + abs(d0)
    tab = rel_bias.astype(F32)
    heads = tab.shape[0]
    ext = jnp.concatenate([jnp.broadcast_to(tab[:, :1], (heads, padw)), tab,
                           jnp.broadcast_to(tab[:, -1:], (heads, padw))], axis=1)
    base = padw + REL_CLIP + d0
    g = jnp.concatenate([jnp.flip(ext[:, base - span + 1:base + 1], axis=1),
                         jnp.flip(ext[:, base + 1:base + tq + 1], axis=1)], axis=1)
    toe = jnp.tile(g, (1, tq))[:, :tq * (n - 1)].reshape(heads, tq, n - 1)[:, :, :span]
    q_pos = q0 + np.arange(tq)[:, None]
    k_pos = k0 + np.arange(span)[None, :]
    qc, kc = q_pos // CHUNK, k_pos // CHUNK
    mask = (np.arange(span)[None, :] < n_keys) & (k_pos >= 0) & (kc <= qc) & (kc >= qc - BAND_CHUNKS)
    return jnp.where(jnp.asarray(mask)[None], toe, NEG_BIG)


def _band(q, k, v, bm, tq, span):
    b, lq, w = q.shape
    nd, heads = bm.shape[:2]
    lk = k.shape[1]
    qspec = pl.BlockSpec((1, tq, w), lambda i, j: (i, j, 0))
    kspec = pl.BlockSpec((1, lk, w), lambda i, j: (i, 0, 0))
    bspec = pl.BlockSpec((1, heads, tq, span), lambda i, j: (jnp.minimum(j, nd - 1), 0, 0, 0))
    return pl.pallas_call(
        functools.partial(_band_kernel, tq=tq, span=span),
        grid=(b, lq // tq),
        in_specs=[qspec, kspec, kspec, bspec],
        out_specs=qspec,
        out_shape=jax.ShapeDtypeStruct((b, lq, w), BF16),
        compiler_params=_cparams("parallel", "arbitrary"),
        name="band",
    )(q, k, v, bm)


def _ssd_kernel(xbc_ref, z_ref, dt_ref, cw_ref, cb_ref, dtb_ref, alog_ref, dskip_ref, gain_ref, st0_ref,
                cbuf_ref, tri_ref, e_ref, y_ref, stn_ref, st_scr, xwin, *, tc, valid_len):
    c = pl.program_id(1)

    @pl.when(c == 0)
    def _():
        st_scr[...] = st0_ref[0]
        xwin[0:SUBLANES, :] = cbuf_ref[0]

    u = xbc_ref[0]
    xwin[SUBLANES:SUBLANES + tc, :] = u
    acc = cb_ref[...] + cw_ref[SSD_CONV - 1:SSD_CONV, :] * u
    for i in range(SSD_CONV - 1):
        off = SUBLANES - (SSD_CONV - 1) + i
        acc = acc + cw_ref[i:i + 1, :] * xwin[off:off + tc, :]
    xwin[0:SUBLANES, :] = u[tc - SUBLANES:, :]
    xc = _silu(acc)
    xs = xc[:, :SSD_INNER]

    rowid = c * tc + lax.broadcasted_iota(jnp.int32, (tc, LANES), 0)
    dt = jnp.where(rowid < valid_len, _softplus(dt_ref[0] + dtb_ref[...]), 0.0)
    da = dt * (-jnp.exp(alog_ref[...]))
    cum = _split_dot_left(tri_ref[...], da)
    dt_b = _split_dot(dt, e_ref[...])
    cum_b = _split_dot(cum, e_ref[...])
    last_b = cum_b[tc - 1:tc, :]
    xd = xs * dt_b
    xw = (xd * jnp.exp(last_b - cum_b)).astype(BF16)
    xd_bf = xd.astype(BF16)
    cum_t = cum.T

    lane = _lane_iota((1, LANES))
    ri = lax.broadcasted_iota(jnp.int32, (tc, tc), 0)
    ci = lax.broadcasted_iota(jnp.int32, (tc, tc), 1)
    causal = ri >= ci
    heads_per_group = SSD_HEADS // SSD_GROUPS
    gw = heads_per_group * SSD_P
    gls = [slice(g * gw, (g + 1) * gw) for g in range(SSD_GROUPS)]
    bgs = [xc[:, SSD_INNER + g * SSD_N:SSD_INNER + (g + 1) * SSD_N].astype(BF16) for g in range(SSD_GROUPS)]
    cgs = [xc[:, SSD_INNER + (SSD_GROUPS + g) * SSD_N:SSD_INNER + (SSD_GROUPS + g + 1) * SSD_N].astype(BF16)
           for g in range(SSD_GROUPS)]
    cbs = [_dot_nt(cg, bg) for cg, bg in zip(cgs, bgs)]
    y_inter = jnp.concatenate([_dot(cg, st_scr[:, gl].astype(BF16)) for cg, gl in zip(cgs, gls)], axis=1)
    s_upd = [_dot_tn(bg, xw[:, gl]) for bg, gl in zip(bgs, gls)]
    atts = []
    for pidx in range(SSD_HEADS // 2):
        cpair = cum_b[:, pidx * LANES:(pidx + 1) * LANES]
        rolled = pltpu.roll(cpair, SSD_P, axis=1)
        for hh in range(2):
            h = 2 * pidx + hh
            colv = jnp.where((lane >= SSD_P) if hh else (lane < SSD_P), cpair, rolled)
            seg = jnp.exp(jnp.where(causal, colv - cum_t[h:h + 1, :], NEG_BIG))
            atts.append((cbs[h // heads_per_group] * seg).astype(BF16))
    pairs = []
    for pidx in range(SSD_HEADS // 2):
        xdp = xd_bf[:, pidx * LANES:(pidx + 1) * LANES]
        pairs.append(jnp.where(lane < SSD_P, _dot(atts[2 * pidx], xdp), _dot(atts[2 * pidx + 1], xdp)))
    y_intra = jnp.concatenate(pairs, axis=1)
    for gl, upd in zip(gls, s_upd):
        st_scr[:, gl] = st_scr[:, gl] * jnp.exp(last_b[:, gl]) + upd
    y = y_intra + y_inter * jnp.exp(cum_b) + dskip_ref[...] * xs
    y = _rms(y * _silu(z_ref[0]), gain_ref[...])
    y_ref[0] = y.astype(y_ref.dtype)

    @pl.when(c == pl.num_programs(1) - 1)
    def _():
        stn_ref[0] = st_scr[...]


def _split_dot_left(m, x):
    hi = x.astype(BF16)
    lo = (x - hi.astype(F32)).astype(BF16)
    return _dot(m, hi) + _dot(m, lo)


def _ssd(xbc, z, dt, conv_w, conv_b, dt_bias, a_log, d_skip, gain, st0, cbuf, valid_len):
    b, l, _ = xbc.shape
    tc = SSD_TILE
    assert l % tc == 0
    pad = LANES - SSD_HEADS
    tri = (np.arange(tc)[:, None] >= np.arange(tc)[None, :]).astype(BF16)
    expand = (np.arange(LANES)[:, None] == (np.arange(SSD_INNER) // SSD_P)[None, :]).astype(BF16)
    blk = lambda w: pl.BlockSpec((1, tc, w), lambda i, c: (i, c, 0))
    const2 = lambda i, c: (0, 0)
    cs = lambda a: pl.BlockSpec(a.shape, const2)
    st_spec = pl.BlockSpec((1, SSD_N, SSD_INNER), lambda i, c: (i, 0, 0))
    params = [conv_w, conv_b.reshape(1, -1),
              jnp.pad(dt_bias.astype(F32), (0, pad)).reshape(1, LANES),
              jnp.pad(a_log.astype(F32), (0, pad)).reshape(1, LANES),
              jnp.repeat(d_skip.astype(F32), SSD_P).reshape(1, SSD_INNER),
              gain.reshape(1, SSD_INNER)]
    return pl.pallas_call(
        functools.partial(_ssd_kernel, tc=tc, valid_len=valid_len),
        grid=(b, l // tc),
        in_specs=[blk(SSD_CONV_DIM), blk(SSD_INNER), blk(LANES)] + [cs(p) for p in params]
                 + [st_spec, pl.BlockSpec((1, SUBLANES, SSD_CONV_DIM), lambda i, c: (i, 0, 0)), cs(tri), cs(expand)],
        out_specs=[blk(SSD_INNER), st_spec],
        out_shape=[jax.ShapeDtypeStruct((b, l, SSD_INNER), BF16),
                   jax.ShapeDtypeStruct((b, SSD_N, SSD_INNER), F32)],
        scratch_shapes=[pltpu.VMEM((SSD_N, SSD_INNER), F32), pltpu.VMEM((SUBLANES + tc, SSD_CONV_DIM), F32)],
        compiler_params=_cparams("parallel", "arbitrary"),
        name="ssd",
    )(xbc, z, dt, *params, st0, cbuf, tri, expand)


def _rot_tables(pos, rows):
    half = RET_DK // 2
    inv = np.power(np.float32(ROPE_BASE), -np.arange(half, dtype=np.float32) / np.float32(half))
    ang = pos.astype(np.float32)[:, None] * inv[None, :]
    cos = np.tile(np.cos(ang), (rows // pos.shape[0], 2 * RET_HEADS))
    sin = np.sin(ang)
    sin = np.tile(np.concatenate([-sin, sin], axis=1), (rows // pos.shape[0], RET_HEADS))
    return cos.astype(np.float32), sin.astype(np.float32)


def _swap_halves(w, heads, dh):
    d = w.shape[0]
    return jnp.flip(w.reshape(d, heads, 2, dh // 2), axis=2).reshape(d, heads * dh)


def _ab_layer(x, b, l, pos0, g, w_ext, ret_gain, past):
    t = b * l
    rows = max(l, min(ROW_TILE, t))
    rot = _rot_tables(pos0 + np.arange(l, dtype=np.int32), rows)
    o = [0]
    for wdt in (RET_QK_W, RET_QK_W, RET_V_W, RET_V_W, SB_W, SB_W, SB_W):
        o.append(o[-1] + wdt)
    specs = [(o[0], RET_QK_W, HEAD_SCALE, o[7], (BF16,)), (o[1], RET_QK_W, 1.0, o[7] + RET_QK_W, (BF16,)),
             (o[2], RET_V_W, 1.0, None, (BF16,)), (o[3], RET_V_W, 1.0, None, (F32,)),
             (o[4], SB_W, HEAD_SCALE * LOG2E, None, (BF16,)), (o[5], SB_W, 1.0, None, (F32,)),
             (o[6], SB_W, 1.0, None, (F32,))]
    rq, rk, rv, rg, sq, sk, sv = _proj(x, g, w_ext, specs, rot)
    r3 = lambda a: a.reshape(b, l, a.shape[-1])
    if past is None:
        s0 = jnp.zeros((b, RET_QK_W, RET_DV), F32)
        tc = min(RET_TILE, l)
    else:
        s0 = past[0].astype(F32).reshape(b, RET_QK_W, RET_DV)
        tc = l
    ro, s_new = _retention(r3(rq), r3(rk), r3(rv), r3(rg), ret_gain, s0, tc)
    sk3, sv3 = r3(sk), r3(sv)
    if past is None:
        so = _stick_breaking(r3(sq), sk3, sv3, sk3, sv3, prompt=True)
    else:
        (ck_all, layer), (cv_all, _) = past[1], past[2]
        ck = jnp.transpose(ck_all, (0, 1, 3, 4, 2)).reshape(ck_all.shape[0], b, SB_W, -1)
        cv = jnp.transpose(cv_all, (0, 1, 3, 4, 2)).reshape(cv_all.shape[0], b, SB_W, -1)
        padn = (-l) % LANES
        so = _stick_breaking(r3(sq), jnp.pad(sk3, ((0, 0), (0, padn), (0, 0))),
                             jnp.pad(sv3, ((0, 0), (0, padn), (0, 0))), ck, cv, prompt=False, layer=layer)
    outs = (s_new.reshape(b, RET_HEADS, RET_DK, RET_DV), sk3.reshape(b, l, SB_HEADS, SB_DH),
            sv3.reshape(b, l, SB_HEADS, SB_DH))
    return ro.reshape(t, RET_V_W), so.reshape(t, SB_W), outs


def _cd_layer(x, b, l, pos0, g, w_ext, rel_bias, conv_w, conv_b, dt_bias, a_log, d_skip, ssd_gain, past):
    t = b * l
    o = [0]
    for wdt in (BAND_W, BAND_W, BAND_W, SSD_INNER, SSD_CONV_DIM, LANES):
        o.append(o[-1] + wdt)
    specs = [(o[0], BAND_W, HEAD_SCALE, None, (BF16,)), (o[1], BAND_W, 1.0, None, (F32, BF16)),
             (o[2], BAND_W, 1.0, None, (F32, BF16)), (o[3], SSD_INNER, 1.0, None, (F32,)),
             (o[4], SSD_CONV_DIM, 1.0, None, (F32,)), (o[5], LANES, 1.0, None, (F32,))]
    bq, bk, bk16, bv, bv16, z, xbc, dt = _proj(x, g, w_ext, specs)
    r3 = lambda a: a.reshape(b, l, a.shape[-1])
    bk3, bv3, xbc3 = r3(bk), r3(bv), r3(xbc)
    if past is None:
        tq = min(BAND_TILE, l)
        span = BAND_WINDOW + tq
        nd = BAND_WINDOW // tq + 1
        bm = jnp.stack([_band_bias(rel_bias, d * tq, max(d * tq - BAND_WINDOW, 0), tq, span, span)
                        for d in range(nd)])
        bo = _band(r3(bq), r3(bk16), r3(bv16), bm, tq, span)
        keep = min(BAND_WINDOW, l)
        new_k, new_v = bk3[:, l - keep:], bv3[:, l - keep:]
        st0 = jnp.zeros((b, SSD_N, SSD_INNER), F32)
        cbuf = jnp.zeros((b, SUBLANES, SSD_CONV_DIM), F32)
        lp = l
    else:
        ck = past[0].reshape(b, -1, BAND_W)
        cv = past[1].reshape(b, -1, BAND_W)
        wlen = ck.shape[1]
        span = -(-(wlen + l) // LANES) * LANES
        padk = span - wlen - l
        k_all = jnp.concatenate([ck.astype(BF16), r3(bk16), jnp.zeros((b, padk, BAND_W), BF16)], axis=1)
        v_all = jnp.concatenate([cv.astype(BF16), r3(bv16), jnp.zeros((b, padk, BAND_W), BF16)], axis=1)
        bm = _band_bias(rel_bias, pos0, pos0 - wlen, l, span, wlen + l)[None]
        bo = _band(r3(bq), k_all, v_all, bm, l, span)
        new_k, new_v = bk3, bv3
        st0 = jnp.transpose(past[2].astype(F32), (0, 3, 1, 2)).reshape(b, SSD_N, SSD_INNER)
        cbuf = jnp.pad(past[3].astype(F32), ((0, 0), (SUBLANES - (SSD_CONV - 1), 0), (0, 0)))
        lp = -(-l // SSD_TILE) * SSD_TILE
    assert l >= SSD_CONV - 1
    conv_new = xbc3[:, l - (SSD_CONV - 1):]
    padl = lp - l
    pad3 = lambda a: jnp.pad(a, ((0, 0), (0, padl), (0, 0))) if padl else a
    y, st_new = _ssd(pad3(xbc3), pad3(r3(z)), pad3(r3(dt)), conv_w, conv_b, dt_bias, a_log, d_skip, ssd_gain,
                     st0, cbuf, l)
    y = y[:, :l]
    ssm_new = jnp.transpose(st_new.reshape(b, SSD_N, SSD_HEADS, SSD_P), (0, 2, 3, 1))
    outs = (new_k.reshape(b, -1, BAND_HEADS, BAND_DH), new_v.reshape(b, -1, BAND_HEADS, BAND_DH), ssm_new, conv_new)
    return bo.reshape(t, BAND_W), y.reshape(t, SSD_INNER), outs


def _trunk(x3, pos0, wts, past):
    (norm_g, win, wout, final_g, w_ab, ret_gain, wo_ab, w_cd, rel_bias, conv_w, conv_b, dt_bias, a_log,
     d_skip, ssd_gain, wo_cd) = wts
    b, l, d = x3.shape
    depth = norm_g.shape[0]
    x = x3.reshape(b * l, d)
    outs = ([], [], [], [], [], [], [])
    x = _ffn(x, norm_g[0, 0], win[0, 0], wout[0, 0])
    for layer in range(depth):
        j = layer // 2
        if layer % 2 == 0:
            pst = None if past is None else (past[0][j], (past[1], j), (past[2], j))
            ma, mb, (r, k, v) = _ab_layer(x, b, l, pos0, norm_g[layer, 1], w_ab[j], ret_gain[j], pst)
            outs[0].append(r)
            outs[1].append(k)
            outs[2].append(v)
            wo = wo_ab[j]
        else:
            pst = None if past is None else (past[3][j], past[4][j], past[5][j], past[6][j])
            ma, mb, (k, v, s, cb) = _cd_layer(x, b, l, pos0, norm_g[layer, 1], w_cd[j], rel_bias[j], conv_w[j],
                                              conv_b[j], dt_bias[j], a_log[j], d_skip[j], ssd_gain[j], pst)
            outs[3].append(k)
            outs[4].append(v)
            outs[5].append(s)
            outs[6].append(cb)
            wo = wo_cd[j]
        half = ma.shape[1]
        mix = (ma, mb, wo[:half], wo[half:])
        x = _ffn(x, norm_g[layer, 2], win[layer, 1], wout[layer, 1], mix=mix,
                 final_g=final_g if layer + 1 == depth else None)
        if layer + 1 < depth:
            x = _ffn(x, norm_g[layer + 1, 0], win[layer + 1, 0], wout[layer + 1, 0])
    return x.reshape(b, l, d), tuple(jnp.stack(o) for o in outs)


def kernel(x_prompt, x_sample, state_ret, cache_sb_k, cache_sb_v, cache_band_k, cache_band_v, state_ssm, state_conv, norm_g, ffn_w_in, ffn_w_out, final_g, w_in_ab, ret_gain, w_out_ab, w_in_cd, rel_bias, conv_w, conv_b, dt_bias, a_log, d_skip, ssd_gain, w_out_cd):
    win = ffn_w_in.astype(BF16)
    wout = ffn_w_out.astype(BF16)
    w_ab = jnp.concatenate([w_in_ab, _swap3(w_in_ab[:, :, :RET_QK_W]), _swap3(w_in_ab[:, :, RET_QK_W:2 * RET_QK_W])],
                           axis=-1).astype(BF16)
    n_main = 3 * BAND_W + SSD_INNER + SSD_CONV_DIM
    w_cd = jnp.concatenate([w_in_cd, jnp.zeros(w_in_cd.shape[:2] + (LANES - SSD_HEADS,), w_in_cd.dtype)],
                           axis=-1).astype(BF16)
    assert w_cd.shape[-1] == n_main + LANES
    wts = (norm_g, win, wout, final_g, w_ab, ret_gain, w_out_ab.astype(BF16), w_cd, rel_bias, conv_w, conv_b,
           dt_bias, a_log, d_skip, ssd_gain, w_out_cd.astype(BF16))
    y_p, outs_p = _trunk(x_prompt, 0, wts, None)
    past = (state_ret, cache_sb_k, cache_sb_v, cache_band_k, cache_band_v, state_ssm, state_conv)
    y_s, outs_s = _trunk(x_sample, cache_sb_k.shape[2], wts, past)
    return (y_p, y_s) + outs_p + outs_s


def _swap3(w):
    return jnp.stack([_swap_halves(w[i], RET_HEADS, RET_DK) for i in range(w.shape[0])])
```

```python
import functools
import math

import jax
import jax.numpy as jnp
import numpy as np
from jax import lax
from jax.experimental import pallas as pl
from jax.experimental.pallas import tpu as pltpu

F32 = jnp.float32
BF16 = jnp.bfloat16

CHUNK = 64
EPS = 1e-6
ROPE_BASE = 10000.0
RET_HEADS, RET_DK, RET_DV = 4, 64, 128
SB_HEADS, SB_DH = 8, 64
BAND_HEADS, BAND_DH, BAND_CHUNKS, REL_CLIP = 8, 64, 8, 128
BAND_WINDOW = BAND_CHUNKS * CHUNK
SSD_HEADS, SSD_P, SSD_GROUPS, SSD_N, SSD_CONV = 8, 64, 2, 128, 4
SSD_INNER = SSD_HEADS * SSD_P
SSD_CONV_DIM = SSD_INNER + 2 * SSD_GROUPS * SSD_N
RET_QK_W = RET_HEADS * RET_DK
RET_V_W = RET_HEADS * RET_DV
SB_W = SB_HEADS * SB_DH
BAND_W = BAND_HEADS * BAND_DH
HEAD_SCALE = 0.125

LANES = 128
SUBLANES = 8
VMEM_LIMIT_BYTES = 56 * 1024 * 1024

NEG_BIG = -1e30

ROW_TILE = 1024
PROJ_TILE = 512
FFN_CHUNK = 512
RET_TILE = 256
SB_TILE = 256
BAND_TILE = 128
SSD_TILE = 128


def _cparams(*sem):
    return pltpu.CompilerParams(dimension_semantics=sem, vmem_limit_bytes=VMEM_LIMIT_BYTES)


def _rms(x, g):
    return x * lax.rsqrt(jnp.mean(x * x, axis=-1, keepdims=True) + EPS) * g


def _silu(x):
    return x * jax.nn.sigmoid(x)


def _softplus(x):
    return jnp.maximum(x, 0.0) + jnp.log(1.0 + jnp.exp(-jnp.abs(x)))


def _dot(a, b):
    return jnp.dot(a, b, preferred_element_type=F32)


def _dot_nt(a, b):
    return lax.dot_general(a, b, (((1,), (1,)), ((), ())), preferred_element_type=F32)


def _dot_tn(a, b):
    return lax.dot_general(a, b, (((0,), (0,)), ((), ())), preferred_element_type=F32)


def _split_dot(x, m):
    hi = x.astype(BF16)
    lo = (x - hi.astype(F32)).astype(BF16)
    return _dot(hi, m) + _dot(lo, m)


def _lane_iota(shape):
    return lax.broadcasted_iota(jnp.int32, shape, len(shape) - 1)


def _ffn_kernel(*refs, f, has_mix, final):
    refs = list(refs)
    x_ref = refs.pop(0)
    if has_mix:
        ma_ref, mb_ref, wa_ref, wb_ref = refs[:4]
        refs = refs[4:]
    g_ref, win_ref, wout_ref = refs[:3]
    refs = refs[3:]
    if final:
        fg_ref = refs.pop(0)
    o_ref, x1_ref, a_ref = refs

    x = x_ref[...]
    if has_mix:
        x = x + _dot(ma_ref[...], wa_ref[...]) + _dot(mb_ref[...], wb_ref[...])
    x1_ref[...] = x
    h = _rms(x, g_ref[...]).astype(BF16)
    for f0 in range(0, f, FFN_CHUNK):
        f1 = min(f0 + FFN_CHUNK, f)
        gate = _dot(h, win_ref[:, f0:f1])
        up = _dot(h, win_ref[:, f + f0:f + f1])
        a_ref[:, f0:f1] = (_silu(gate) * up).astype(BF16)
    y = x1_ref[...] + 0.5 * _dot(a_ref[...], wout_ref[...])
    if final:
        y = _rms(y, fg_ref[...])
    o_ref[...] = y


def _resident(shape):
    return pl.BlockSpec(shape, lambda *_: (0,) * len(shape), pipeline_mode=pl.Buffered(1))


def _ffn(x, g, win, wout, mix=None, final_g=None):
    t, d = x.shape
    f = wout.shape[0]
    tm = min(ROW_TILE, t)
    assert t % tm == 0 and f % LANES == 0
    row = lambda i: (i, 0)
    args = [x]
    specs = [pl.BlockSpec((tm, d), row)]
    if mix is not None:
        ma, mb, wa, wb = mix
        args += [ma, mb, wa, wb]
        specs += [pl.BlockSpec((tm, ma.shape[1]), row), pl.BlockSpec((tm, mb.shape[1]), row),
                  _resident(wa.shape), _resident(wb.shape)]
    args += [g.reshape(1, d), win, wout]
    specs += [_resident((1, d)), _resident(win.shape), _resident(wout.shape)]
    if final_g is not None:
        args.append(final_g.reshape(1, d))
        specs.append(_resident((1, d)))
    return pl.pallas_call(
        functools.partial(_ffn_kernel, f=f, has_mix=mix is not None, final=final_g is not None),
        grid=(t // tm,),
        in_specs=specs,
        out_specs=pl.BlockSpec((tm, d), row),
        out_shape=jax.ShapeDtypeStruct((t, d), F32),
        scratch_shapes=[pltpu.VMEM((tm, d), F32), pltpu.VMEM((tm, f), BF16)],
        compiler_params=_cparams("parallel"),
        name="ffn",
    )(*args)


def _proj_kernel(*refs, specs, has_rot):
    refs = list(refs)
    x_ref, g_ref, w_ref = refs[:3]
    refs = refs[3:]
    if has_rot:
        cos_ref, sin_ref = refs[:2]
        refs = refs[2:]
    prev_refs = [refs.pop(0) for s in specs if s[5]]
    h = _rms(x_ref[...], g_ref[...]).astype(BF16)
    for c0, width, scale, c_swap, dts, n_prev in specs:
        y = _dot(h, w_ref[:, c0:c0 + width])
        if c_swap is not None:
            y = y * cos_ref[...] + _dot(h, w_ref[:, c_swap:c_swap + width]) * sin_ref[...]
        if scale != 1.0:
            y = y * scale
        for n, _ in enumerate(dts):
            o_ref = refs.pop(0)
            if n == 0 and n_prev is not None:
                if n_prev:
                    o_ref[0:n_prev] = prev_refs.pop(0)[...]
                o_ref[n_prev] = y.astype(o_ref.dtype)
            else:
                o_ref[...] = y.astype(o_ref.dtype)


def _proj(x, g, w, specs, rot=None):
    t, d = x.shape
    tm = min(PROJ_TILE, t)
    assert t % tm == 0
    row = lambda i: (i, 0)
    row3 = lambda i: (0, i, 0)
    const2 = lambda i: (0, 0)
    args = [x, g.reshape(1, d), w]
    in_specs = [pl.BlockSpec((tm, d), row), pl.BlockSpec((1, d), const2), _resident(w.shape)]
    if rot is not None:
        cos, sin = rot
        nrot = cos.shape[0] // tm
        rot_map = lambda i: (i % nrot, 0)
        args += [cos, sin]
        in_specs += [pl.BlockSpec((tm, cos.shape[1]), rot_map), pl.BlockSpec((tm, sin.shape[1]), rot_map)]
    kspecs, out_specs, out_shape = [], [], []
    for c0, width, scale, c_swap, dts, prev in specs:
        n_prev = None
        if prev is not None:
            n_prev = 0 if prev is () else prev.shape[0]
            if n_prev:
                args.append(prev)
                in_specs.append(pl.BlockSpec((n_prev, tm, width), row3))
        kspecs.append((c0, width, scale, c_swap, dts, n_prev))
        for n, dt in enumerate(dts):
            if n == 0 and n_prev is not None:
                out_specs.append(pl.BlockSpec((n_prev + 1, tm, width), row3))
                out_shape.append(jax.ShapeDtypeStruct((n_prev + 1, t, width), dt))
            else:
                out_specs.append(pl.BlockSpec((tm, width), row))
                out_shape.append(jax.ShapeDtypeStruct((t, width), dt))
    return pl.pallas_call(
        functools.partial(_proj_kernel, specs=tuple(kspecs), has_rot=rot is not None),
        grid=(t // tm,),
        in_specs=in_specs,
        out_specs=out_specs,
        out_shape=out_shape,
        compiler_params=_cparams("parallel"),
        name="proj",
    )(*args)


def _ret_kernel(q_ref, k_ref, v_ref, rg_ref, gain_ref, s0_ref, dm_ref, qd_ref, kd_ref, cdec_ref,
                o_ref, sn_ref, s_scr):
    c = pl.program_id(1)

    @pl.when(c == 0)
    def _():
        s_scr[...] = s0_ref[0]

    q = q_ref[0]
    k = k_ref[0]
    v = v_ref[0]
    qs = (q.astype(F32) * qd_ref[...]).astype(BF16)
    ks = (k.astype(F32) * kd_ref[...]).astype(BF16)
    lane = _lane_iota((1, RET_QK_W))
    s_all = s_scr[...]
    s_bf = s_all.astype(BF16)
    upd = _dot_tn(ks, v)
    zero = jnp.zeros_like(q)
    heads = range(RET_HEADS)
    hms = [(lane >= h * RET_DK) & (lane < (h + 1) * RET_DK) for h in heads]
    sls = [slice(h * RET_DV, (h + 1) * RET_DV) for h in heads]
    atts = [_dot_nt(jnp.where(hm, q, zero), k) * dm_ref[h] for h, hm in zip(heads, hms)]
    cross = [_dot(jnp.where(hm, qs, zero), s_bf) for hm in hms]
    outs = [_dot(att.astype(BF16), v[:, sl]) + cr for att, sl, cr in zip(atts, sls, cross)]
    outs = [o * lax.rsqrt(jnp.mean(o * o, axis=-1, keepdims=True) + EPS) for o in outs]
    o = jnp.concatenate(outs, axis=1)
    o_ref[0] = (o * gain_ref[...] * _silu(rg_ref[0])).astype(o_ref.dtype)
    for h, sl in zip(heads, sls):
        rows = slice(h * RET_DK, (h + 1) * RET_DK)
        s_scr[rows, :] = s_all[rows, :] * cdec_ref[rows, :] + upd[rows, sl]

    @pl.when(c == pl.num_programs(1) - 1)
    def _():
        sn_ref[0] = s_scr[...]


def _retention(q, k, v, rg, gain, s0, tc):
    b, l, _ = q.shape
    assert l % tc == 0
    f32 = np.float32
    lg = np.log1p(-np.exp2(-5.0 - np.arange(RET_HEADS, dtype=f32))).astype(f32)
    idx = np.arange(tc, dtype=f32)
    diff = idx[:, None] - idx[None, :]
    dm = np.where(diff >= 0, np.exp(lg[:, None, None] * np.maximum(diff, 0.0)), 0.0).astype(f32)
    qd = np.repeat(np.exp(lg[None, :] * (idx[:, None] + 1.0)), RET_DK, axis=1).astype(f32)
    kd = np.repeat(np.exp(lg[None, :] * (tc - 1.0 - idx[:, None])), RET_DK, axis=1).astype(f32)
    cdec = np.broadcast_to(np.repeat(np.exp(lg * f32(tc)), RET_DK)[:, None], (RET_QK_W, RET_DV)).astype(f32)
    blk = lambda w: pl.BlockSpec((1, tc, w), lambda i, c: (i, c, 0))
    const2 = lambda i, c: (0, 0)
    st_spec = pl.BlockSpec((1, RET_QK_W, RET_DV), lambda i, c: (i, 0, 0))
    return pl.pallas_call(
        _ret_kernel,
        grid=(b, l // tc),
        in_specs=[blk(RET_QK_W), blk(RET_QK_W), blk(RET_V_W), blk(RET_V_W),
                  pl.BlockSpec((1, RET_V_W), const2), st_spec,
                  pl.BlockSpec((RET_HEADS, tc, tc), lambda i, c: (0, 0, 0)),
                  pl.BlockSpec((tc, RET_QK_W), const2), pl.BlockSpec((tc, RET_QK_W), const2),
                  pl.BlockSpec((RET_QK_W, RET_DV), const2)],
        out_specs=[blk(RET_V_W), st_spec],
        out_shape=[jax.ShapeDtypeStruct((b, l, RET_V_W), BF16),
                   jax.ShapeDtypeStruct((b, RET_QK_W, RET_DV), F32)],
        scratch_shapes=[pltpu.VMEM((RET_QK_W, RET_DV), F32)],
        compiler_params=_cparams("parallel", "arbitrary"),
        name="retention",
    )(q, k, v, rg, gain.reshape(1, RET_V_W), s0, dm, qd, kd, cdec)


def _cum_matrix(n):
    tri = (np.arange(n)[:, None] >= np.arange(n)[None, :]).astype(BF16)
    return np.concatenate([tri, tri], axis=0)


LOG2E = 1.4426950408889634
SB_UNDERFLOW = -160.0


def _sb_block(qms, kbs, vbs, cum_m, carries, mask, transposed=False):
    if transposed:
        zs = [_dot(qm, kb) for qm, kb in zip(qms, kbs)]
    else:
        zs = [_dot_nt(qm, kb) for qm, kb in zip(qms, kbs)]
    nzs = [-z for z in zs]
    lgs = [jnp.minimum(nz, 0.0) - jnp.log2(1.0 + jnp.exp2(jnp.minimum(z, nz))) for z, nz in zip(zs, nzs)]
    if mask is not None:
        lgs = [jnp.where(mask, lg, 0.0) for lg in lgs]
    his = [lg.astype(BF16) for lg in lgs]
    los = [(lg - hi.astype(F32)).astype(BF16) for lg, hi in zip(lgs, his)]
    incls = [_dot(jnp.concatenate([hi, lo], axis=1), cum_m) for hi, lo in zip(his, los)]
    if carries is None:
        ws = [jnp.exp2(z + incl) for z, incl in zip(zs, incls)]
    else:
        ws = [jnp.exp2(z + incl + c) for z, incl, c in zip(zs, incls, carries)]
    if mask is not None:
        ws = [jnp.where(mask, w, 0.0) for w in ws]
    if transposed:
        outs = [_dot_nt(w.astype(BF16), vb) for w, vb in zip(ws, vbs)]
    else:
        outs = [_dot(w.astype(BF16), vb) for w, vb in zip(ws, vbs)]
    return outs, [incl[:, 0:1] for incl in incls]


SB_PAIRS = 2


def _sb_kernel(q_ref, kd_ref, vd_ref, kp_ref, vp_ref, md_ref, mp_ref, o_ref, *, tq, tkd, tk, blocks_per_q,
               n_past, past_t):
    i = pl.program_id(2)
    lane = _lane_iota((1, LANES))
    sls = [slice(p * LANES, (p + 1) * LANES) for p in range(SB_PAIRS)]
    nh = 2 * SB_PAIRS
    qms = []
    for sl in sls:
        q = q_ref[0, :, sl]
        zero = jnp.zeros_like(q)
        qms += [jnp.where(lane < SB_DH, q, zero), jnp.where(lane >= SB_DH, q, zero)]
    row = lax.broadcasted_iota(jnp.int32, (tq, tkd), 0)
    col = lax.broadcasted_iota(jnp.int32, (tq, tkd), 1)
    dmask = col < row
    nblk = i * blocks_per_q if n_past is None else n_past

    per_head = lambda xs: [x for x in xs for _ in range(2)]
    kds = per_head([kd_ref[0, :, sl].astype(BF16) for sl in sls])
    vds = per_head([vd_ref[0, :, sl].astype(BF16) for sl in sls])
    accs, carries = _sb_block(qms, kds, vds, md_ref[...], None, dmask)

    def live(cs):
        return jnp.max(functools.reduce(jnp.maximum, cs)) > SB_UNDERFLOW

    def cond(st):
        return (st[0] >= 0) & st[1]

    def body(st):
        j, accs, cs = st[0], st[2:2 + nh], st[2 + nh:]
        off = pl.multiple_of(j * tk, tk)
        if past_t:
            kbs = per_head([kp_ref[0, sl, pl.ds(off, tk)].astype(BF16) for sl in sls])
            vbs = per_head([vp_ref[0, sl, pl.ds(off, tk)].astype(BF16) for sl in sls])
        else:
            kbs = per_head([kp_ref[0, pl.ds(off, tk), sl].astype(BF16) for sl in sls])
            vbs = per_head([vp_ref[0, pl.ds(off, tk), sl].astype(BF16) for sl in sls])
        ds, ts = _sb_block(qms, kbs, vbs, mp_ref[...], cs, None, transposed=past_t)
        cs = [c + t for c, t in zip(cs, ts)]
        return (j - 1, live(cs), *[a + d for a, d in zip(accs, ds)], *cs)

    st = lax.while_loop(cond, body, (jnp.asarray(nblk - 1, jnp.int32), live(carries), *accs, *carries))
    pairs = [jnp.where(lane < SB_DH, st[2 + 2 * p], st[3 + 2 * p]) for p in range(SB_PAIRS)]
    o_ref[0] = jnp.concatenate(pairs, axis=1).astype(o_ref.dtype)


def _stick_breaking(q, k_own, v_own, own_layer, k_past, v_past, past_layer, prompt):
    b, lq, w = q.shape
    wb = SB_PAIRS * LANES
    if prompt:
        tq = tkd = tk = min(SB_TILE, lq)
        blocks_per_q, n_past = 1, None
        lp = k_past.shape[2]
        pspec = pl.BlockSpec((None, 1, lp, wb), lambda i, p, j: (past_layer, i, 0, p))
    else:
        tq = lq
        tkd = k_own.shape[2]
        tk = SB_TILE
        lp = k_past.shape[3]
        assert lp % tk == 0
        blocks_per_q, n_past = 0, lp // tk
        pspec = pl.BlockSpec((None, 1, wb, lp), lambda i, p, j: (past_layer, i, p, 0))
    assert lq % tq == 0
    qspec = pl.BlockSpec((1, tq, wb), lambda i, p, j: (i, j, p))
    dspec = pl.BlockSpec((None, 1, tkd, wb), lambda i, p, j: (own_layer, i, j, p))
    md = _cum_matrix(tkd)
    mp = _cum_matrix(tk)
    const2 = lambda i, p, j: (0, 0)
    return pl.pallas_call(
        functools.partial(_sb_kernel, tq=tq, tkd=tkd, tk=tk, blocks_per_q=blocks_per_q, n_past=n_past,
                          past_t=not prompt),
        grid=(b, w // wb, lq // tq),
        in_specs=[qspec, dspec, dspec, pspec, pspec, pl.BlockSpec(md.shape, const2), pl.BlockSpec(mp.shape, const2)],
        out_specs=qspec,
        out_shape=jax.ShapeDtypeStruct((b, lq, w), BF16),
        compiler_params=_cparams("parallel", "parallel", "arbitrary"),
        name="stick_breaking",
    )(q, k_own, v_own, k_past, v_past, md, mp)


def _band_kernel(q_ref, k_ref, v_ref, bm_ref, o_ref, *, tq, span):
    i = pl.program_id(1)
    ws = pl.multiple_of(jnp.maximum(i * tq - (span - tq), 0), LANES)
    lane = _lane_iota((1, LANES))
    npair = BAND_W // LANES
    sls = [slice(pr * LANES, (pr + 1) * LANES) for pr in range(npair)]
    scores = []
    for pr in range(npair):
        kw = k_ref[0, pl.ds(ws, span), sls[pr]]
        q = q_ref[0, :, sls[pr]]
        zero = jnp.zeros_like(q)
        for h in range(2):
            qm = jnp.where((lane >= BAND_DH) if h else (lane < BAND_DH), q, zero)
            scores.append(_dot_nt(qm, kw) + bm_ref[0, 2 * pr + h])
    probs = [jnp.exp(s - jnp.max(s, axis=-1, keepdims=True)) for s in scores]
    pairs = []
    for pr in range(npair):
        vw = v_ref[0, pl.ds(ws, span), sls[pr]]
        outs = [_dot(p.astype(BF16), vw) / jnp.sum(p, axis=-1, keepdims=True) for p in probs[2 * pr:2 * pr + 2]]
        pairs.append(jnp.where(lane < BAND_DH, outs[0], outs[1]))
    o_ref[0] = jnp.concatenate(pairs, axis=1).astype(o_ref.dtype)


def _band_bias(rel_bias, q0, k0, tq, span, n_keys):
    d0 = q0 - k0
    n = span + tq
    padw = n + abs(d0)
    tab = rel_bias.astype(F32)
    heads = tab.shape[0]
    ext = jnp.concatenate([jnp.broadcast_to(tab[:, :1], (heads, padw)), tab,
                           jnp.broadcast_to(tab[:, -1:], (heads, padw))], axis=1)
    base = padw + REL_CLIP + d0
    g = jnp.concatenate([jnp.flip(ext[:, base - span + 1:base + 1], axis=1),
                         jnp.flip(ext[:, base + 1:base + tq + 1], axis=1)], axis=1)
    toe = jnp.tile(g, (1, tq))[:, :tq * (n - 1)].reshape(heads, tq, n - 1)[:, :, :span]
    q_pos = q0 + np.arange(tq)[:, None]
    k_pos = k0 + np.arange(span)[None, :]
    qc, kc = q_pos // CHUNK, k_pos // CHUNK
    mask = (np.arange(span)[None, :] < n_keys) & (k_pos >= 0) & (kc <= qc) & (kc >= qc - BAND_CHUNKS)
    return jnp.where(jnp.asarray(mask)[None], toe, NEG_BIG)


def _band(q, k, v, bm, tq, span):
    b, lq, w = q.shape
    nd, heads = bm.shape[:2]
    lk = k.shape[1]
    qspec = pl.BlockSpec((1, tq, w), lambda i, j: (i, j, 0))
    kspec = pl.BlockSpec((1, lk, w), lambda i, j: (i, 0, 0))
    bspec = pl.BlockSpec((1, heads, tq, span), lambda i, j: (jnp.minimum(j, nd - 1), 0, 0, 0))
    return pl.pallas_call(
        functools.partial(_band_kernel, tq=tq, span=span),
        grid=(b, lq // tq),
        in_specs=[qspec, kspec, kspec, bspec],
        out_specs=qspec,
        out_shape=jax.ShapeDtypeStruct((b, lq, w), BF16),
        compiler_params=_cparams("parallel", "arbitrary"),
        name="band",
    )(q, k, v, bm)


def _ssd_kernel(xbc_ref, z_ref, dt_ref, cw_ref, cb_ref, dtb_ref, alog_ref, dskip_ref, gain_ref, st0_ref,
                cbuf_ref, tri_ref, e_ref, y_ref, stn_ref, st_scr, xwin, *, tc, valid_len):
    c = pl.program_id(1)

    @pl.when(c == 0)
    def _():
        st_scr[...] = st0_ref[0]
        xwin[0:SUBLANES, :] = cbuf_ref[0]

    u = xbc_ref[0]
    xwin[SUBLANES:SUBLANES + tc, :] = u
    acc = cb_ref[...] + cw_ref[SSD_CONV - 1:SSD_CONV, :] * u
    for i in range(SSD_CONV - 1):
        off = SUBLANES - (SSD_CONV - 1) + i
        acc = acc + cw_ref[i:i + 1, :] * xwin[off:off + tc, :]
    xwin[0:SUBLANES, :] = u[tc - SUBLANES:, :]
    xc = _silu(acc)
    xs = xc[:, :SSD_INNER]

    rowid = c * tc + lax.broadcasted_iota(jnp.int32, (tc, LANES), 0)
    dt = jnp.where(rowid < valid_len, _softplus(dt_ref[0] + dtb_ref[...]), 0.0)
    da = dt * (-jnp.exp(alog_ref[...]))
    cum = _split_dot_left(tri_ref[...], da)
    dt_b = _split_dot(dt, e_ref[...])
    cum_b = _split_dot(cum, e_ref[...])
    last_b = cum_b[tc - 1:tc, :]
    xd = xs * dt_b
    xw = (xd * jnp.exp(last_b - cum_b)).astype(BF16)
    xd_bf = xd.astype(BF16)
    cum_t = cum.T

    lane = _lane_iota((1, LANES))
    ri = lax.broadcasted_iota(jnp.int32, (tc, tc), 0)
    ci = lax.broadcasted_iota(jnp.int32, (tc, tc), 1)
    causal = ri >= ci
    heads_per_group = SSD_HEADS // SSD_GROUPS
    gw = heads_per_group * SSD_P
    gls = [slice(g * gw, (g + 1) * gw) for g in range(SSD_GROUPS)]
    bgs = [xc[:, SSD_INNER + g * SSD_N:SSD_INNER + (g + 1) * SSD_N].astype(BF16) for g in range(SSD_GROUPS)]
    cgs = [xc[:, SSD_INNER + (SSD_GROUPS + g) * SSD_N:SSD_INNER + (SSD_GROUPS + g + 1) * SSD_N].astype(BF16)
           for g in range(SSD_GROUPS)]
    cbs = [_dot_nt(cg, bg) for cg, bg in zip(cgs, bgs)]
    y_inter = jnp.concatenate([_dot(cg, st_scr[:, gl].astype(BF16)) for cg, gl in zip(cgs, gls)], axis=1)
    s_upd = [_dot_tn(bg, xw[:, gl]) for bg, gl in zip(bgs, gls)]
    atts = []
    for pidx in range(SSD_HEADS // 2):
        cpair = cum_b[:, pidx * LANES:(pidx + 1) * LANES]
        rolled = pltpu.roll(cpair, SSD_P, axis=1)
        for hh in range(2):
            h = 2 * pidx + hh
            colv = jnp.where((lane >= SSD_P) if hh else (lane < SSD_P), cpair, rolled)
            seg = jnp.exp(jnp.where(causal, colv - cum_t[h:h + 1, :], NEG_BIG))
            atts.append((cbs[h // heads_per_group] * seg).astype(BF16))
    pairs = []
    for pidx in range(SSD_HEADS // 2):
        xdp = xd_bf[:, pidx * LANES:(pidx + 1) * LANES]
        pairs.append(jnp.where(lane < SSD_P, _dot(atts[2 * pidx], xdp), _dot(atts[2 * pidx + 1], xdp)))
    y_intra = jnp.concatenate(pairs, axis=1)
    for gl, upd in zip(gls, s_upd):
        st_scr[:, gl] = st_scr[:, gl] * jnp.exp(last_b[:, gl]) + upd
    y = y_intra + y_inter * jnp.exp(cum_b) + dskip_ref[...] * xs
    y = _rms(y * _silu(z_ref[0]), gain_ref[...])
    y_ref[0] = y.astype(y_ref.dtype)

    @pl.when(c == pl.num_programs(1) - 1)
    def _():
        stn_ref[0] = st_scr[...]


def _split_dot_left(m, x):
    hi = x.astype(BF16)
    lo = (x - hi.astype(F32)).astype(BF16)
    return _dot(m, hi) + _dot(m, lo)


def _ssd(xbc, z, dt, conv_w, conv_b, dt_bias, a_log, d_skip, gain, st0, cbuf, valid_len):
    b, l, _ = xbc.shape
    tc = SSD_TILE
    assert l % tc == 0
    pad = LANES - SSD_HEADS
    tri = (np.arange(tc)[:, None] >= np.arange(tc)[None, :]).astype(BF16)
    expand = (np.arange(LANES)[:, None] == (np.arange(SSD_INNER) // SSD_P)[None, :]).astype(BF16)
    blk = lambda w: pl.BlockSpec((1, tc, w), lambda i, c: (i, c, 0))
    const2 = lambda i, c: (0, 0)
    cs = lambda a: pl.BlockSpec(a.shape, const2)
    st_spec = pl.BlockSpec((1, SSD_N, SSD_INNER), lambda i, c: (i, 0, 0))
    params = [conv_w, conv_b.reshape(1, -1),
              jnp.pad(dt_bias.astype(F32), (0, pad)).reshape(1, LANES),
              jnp.pad(a_log.astype(F32), (0, pad)).reshape(1, LANES),
              jnp.repeat(d_skip.astype(F32), SSD_P).reshape(1, SSD_INNER),
              gain.reshape(1, SSD_INNER)]
    return pl.pallas_call(
        functools.partial(_ssd_kernel, tc=tc, valid_len=valid_len),
        grid=(b, l // tc),
        in_specs=[blk(SSD_CONV_DIM), blk(SSD_INNER), blk(LANES)] + [cs(p) for p in params]
                 + [st_spec, pl.BlockSpec((1, SUBLANES, SSD_CONV_DIM), lambda i, c: (i, 0, 0)), cs(tri), cs(expand)],
        out_specs=[blk(SSD_INNER), st_spec],
        out_shape=[jax.ShapeDtypeStruct((b, l, SSD_INNER), BF16),
                   jax.ShapeDtypeStruct((b, SSD_N, SSD_INNER), F32)],
        scratch_shapes=[pltpu.VMEM((SSD_N, SSD_INNER), F32), pltpu.VMEM((SUBLANES + tc, SSD_CONV_DIM), F32)],
        compiler_params=_cparams("parallel", "arbitrary"),
        name="ssd",
    )(xbc, z, dt, *params, st0, cbuf, tri, expand)


def _rot_tables(pos, rows):
    half = RET_DK // 2
    inv = np.power(np.float32(ROPE_BASE), -np.arange(half, dtype=np.float32) / np.float32(half))
    ang = pos.astype(np.float32)[:, None] * inv[None, :]
    cos = np.tile(np.cos(ang), (rows // pos.shape[0], 2 * RET_HEADS))
    sin = np.sin(ang)
    sin = np.tile(np.concatenate([-sin, sin], axis=1), (rows // pos.shape[0], RET_HEADS))
    return cos.astype(np.float32), sin.astype(np.float32)


def _swap_halves(w, heads, dh):
    d = w.shape[0]
    return jnp.flip(w.reshape(d, heads, 2, dh // 2), axis=2).reshape(d, heads * dh)


def _ab_layer(x, b, l, pos0, g, w_ext, ret_gain, past, sb_prev):
    t = b * l
    rows = max(l, min(PROJ_TILE, t))
    rot = _rot_tables(pos0 + np.arange(l, dtype=np.int32), rows)
    o = [0]
    for wdt in (RET_QK_W, RET_QK_W, RET_V_W, RET_V_W, SB_W, SB_W, SB_W):
        o.append(o[-1] + wdt)
    specs = [(o[0], RET_QK_W, HEAD_SCALE, o[7], (BF16,), None),
             (o[1], RET_QK_W, 1.0, o[7] + RET_QK_W, (BF16,), None),
             (o[2], RET_V_W, 1.0, None, (BF16,), None), (o[3], RET_V_W, 1.0, None, (F32,), None),
             (o[4], SB_W, HEAD_SCALE * LOG2E, None, (BF16,), None), (o[5], SB_W, 1.0, None, (F32,), sb_prev[0]),
             (o[6], SB_W, 1.0, None, (F32,), sb_prev[1])]
    rq, rk, rv, rg, sq, sk, sv = _proj(x, g, w_ext, specs, rot)
    n_sb = sk.shape[0]
    sk4, sv4 = sk.reshape(n_sb, b, l, SB_W), sv.reshape(n_sb, b, l, SB_W)
    r3 = lambda a: a.reshape(b, l, a.shape[-1])
    if past is None:
        s0 = jnp.zeros((b, RET_QK_W, RET_DV), F32)
        tc = min(RET_TILE, l)
    else:
        s0 = past[0].astype(F32).reshape(b, RET_QK_W, RET_DV)
        tc = l
    ro, s_new = _retention(r3(rq), r3(rk), r3(rv), r3(rg), ret_gain, s0, tc)
    if past is None:
        so = _stick_breaking(r3(sq), sk4, sv4, n_sb - 1, sk4, sv4, n_sb - 1, prompt=True)
    else:
        (ck_all, layer), (cv_all, _) = past[1], past[2]
        ck = jnp.transpose(ck_all, (0, 1, 3, 4, 2)).reshape(ck_all.shape[0], b, SB_W, -1)
        cv = jnp.transpose(cv_all, (0, 1, 3, 4, 2)).reshape(cv_all.shape[0], b, SB_W, -1)
        padn = (-l) % LANES
        pad_own = lambda a: jnp.pad(a[-1:], ((0, 0), (0, 0), (0, padn), (0, 0)))
        so = _stick_breaking(r3(sq), pad_own(sk4), pad_own(sv4), 0, ck, cv, layer, prompt=False)
    return ro.reshape(t, RET_V_W), so.reshape(t, SB_W), s_new.reshape(b, RET_HEADS, RET_DK, RET_DV), (sk, sv)


def _cd_layer(x, b, l, pos0, g, w_ext, rel_bias, conv_w, conv_b, dt_bias, a_log, d_skip, ssd_gain, past):
    t = b * l
    o = [0]
    for wdt in (BAND_W, BAND_W, BAND_W, SSD_INNER, SSD_CONV_DIM, LANES):
        o.append(o[-1] + wdt)
    specs = [(o[0], BAND_W, HEAD_SCALE, None, (BF16,), None), (o[1], BAND_W, 1.0, None, (F32, BF16), None),
             (o[2], BAND_W, 1.0, None, (F32, BF16), None), (o[3], SSD_INNER, 1.0, None, (F32,), None),
             (o[4], SSD_CONV_DIM, 1.0, None, (F32,), None), (o[5], LANES, 1.0, None, (F32,), None)]
    bq, bk, bk16, bv, bv16, z, xbc, dt = _proj(x, g, w_ext, specs)
    r3 = lambda a: a.reshape(b, l, a.shape[-1])
    bk3, bv3, xbc3 = r3(bk), r3(bv), r3(xbc)
    if past is None:
        tq = min(BAND_TILE, l)
        span = BAND_WINDOW + tq
        nd = BAND_WINDOW // tq + 1
        bm = jnp.stack([_band_bias(rel_bias, d * tq, max(d * tq - BAND_WINDOW, 0), tq, span, span)
                        for d in range(nd)])
        bo = _band(r3(bq), r3(bk16), r3(bv16), bm, tq, span)
        keep = min(BAND_WINDOW, l)
        new_k, new_v = bk3[:, l - keep:], bv3[:, l - keep:]
        st0 = jnp.zeros((b, SSD_N, SSD_INNER), F32)
        cbuf = jnp.zeros((b, SUBLANES, SSD_CONV_DIM), F32)
        lp = l
    else:
        ck = past[0].reshape(b, -1, BAND_W)
        cv = past[1].reshape(b, -1, BAND_W)
        wlen = ck.shape[1]
        span = -(-(wlen + l) // LANES) * LANES
        padk = span - wlen - l
        k_all = jnp.concatenate([ck.astype(BF16), r3(bk16), jnp.zeros((b, padk, BAND_W), BF16)], axis=1)
        v_all = jnp.concatenate([cv.astype(BF16), r3(bv16), jnp.zeros((b, padk, BAND_W), BF16)], axis=1)
        bm = _band_bias(rel_bias, pos0, pos0 - wlen, l, span, wlen + l)[None]
        bo = _band(r3(bq), k_all, v_all, bm, l, span)
        new_k, new_v = bk3, bv3
        st0 = jnp.transpose(past[2].astype(F32), (0, 3, 1, 2)).reshape(b, SSD_N, SSD_INNER)
        cbuf = jnp.pad(past[3].astype(F32), ((0, 0), (SUBLANES - (SSD_CONV - 1), 0), (0, 0)))
        lp = -(-l // SSD_TILE) * SSD_TILE
    assert l >= SSD_CONV - 1
    conv_new = xbc3[:, l - (SSD_CONV - 1):]
    padl = lp - l
    pad3 = lambda a: jnp.pad(a, ((0, 0), (0, padl), (0, 0))) if padl else a
    y, st_new = _ssd(pad3(xbc3), pad3(r3(z)), pad3(r3(dt)), conv_w, conv_b, dt_bias, a_log, d_skip, ssd_gain,
                     st0, cbuf, l)
    y = y[:, :l]
    ssm_new = jnp.transpose(st_new.reshape(b, SSD_N, SSD_HEADS, SSD_P), (0, 2, 3, 1))
    outs = (new_k.reshape(b, -1, BAND_HEADS, BAND_DH), new_v.reshape(b, -1, BAND_HEADS, BAND_DH), ssm_new, conv_new)
    return bo.reshape(t, BAND_W), y.reshape(t, SSD_INNER), outs


def _trunk(x3, pos0, wts, past):
    (norm_g, win, wout, final_g, w_ab, ret_gain, wo_ab, w_cd, rel_bias, conv_w, conv_b, dt_bias, a_log,
     d_skip, ssd_gain, wo_cd) = wts
    b, l, d = x3.shape
    depth = norm_g.shape[0]
    x = x3.reshape(b * l, d)
    outs = ([], None, None, [], [], [], [])
    sb_stack = ((), ())
    x = _ffn(x, norm_g[0, 0], win[0, 0], wout[0, 0])
    for layer in range(depth):
        j = layer // 2
        if layer % 2 == 0:
            pst = None if past is None else (past[0][j], (past[1], j), (past[2], j))
            ma, mb, r, sb_stack = _ab_layer(x, b, l, pos0, norm_g[layer, 1], w_ab[j], ret_gain[j], pst, sb_stack)
            outs[0].append(r)
            wo = wo_ab[j]
        else:
            pst = None if past is None else (past[3][j], past[4][j], past[5][j], past[6][j])
            ma, mb, (k, v, s, cb) = _cd_layer(x, b, l, pos0, norm_g[layer, 1], w_cd[j], rel_bias[j], conv_w[j],
                                              conv_b[j], dt_bias[j], a_log[j], d_skip[j], ssd_gain[j], pst)
            outs[3].append(k)
            outs[4].append(v)
            outs[5].append(s)
            outs[6].append(cb)
            wo = wo_cd[j]
        half = ma.shape[1]
        mix = (ma, mb, wo[:half], wo[half:])
        x = _ffn(x, norm_g[layer, 2], win[layer, 1], wout[layer, 1], mix=mix,
                 final_g=final_g if layer + 1 == depth else None)
        if layer + 1 < depth:
            x = _ffn(x, norm_g[layer + 1, 0], win[layer + 1, 0], wout[layer + 1, 0])
    sb_k, sb_v = (jnp.transpose(jnp.swapaxes(a.reshape(a.shape[0], b, l, SB_W), 2, 3)
                                .reshape(a.shape[0], b, SB_HEADS, SB_DH, l), (0, 1, 4, 2, 3)) for a in sb_stack)
    stacked = [jnp.stack(o) for o in outs if o is not None]
    return x.reshape(b, l, d), (stacked[0], sb_k, sb_v, *stacked[1:])


def kernel(x_prompt, x_sample, state_ret, cache_sb_k, cache_sb_v, cache_band_k, cache_band_v, state_ssm, state_conv, norm_g, ffn_w_in, ffn_w_out, final_g, w_in_ab, ret_gain, w_out_ab, w_in_cd, rel_bias, conv_w, conv_b, dt_bias, a_log, d_skip, ssd_gain, w_out_cd):
    win = ffn_w_in.astype(BF16)
    wout = ffn_w_out.astype(BF16)
    w_ab = jnp.concatenate([w_in_ab, _swap3(w_in_ab[:, :, :RET_QK_W]), _swap3(w_in_ab[:, :, RET_QK_W:2 * RET_QK_W])],
                           axis=-1).astype(BF16)
    n_main = 3 * BAND_W + SSD_INNER + SSD_CONV_DIM
    w_cd = jnp.concatenate([w_in_cd, jnp.zeros(w_in_cd.shape[:2] + (LANES - SSD_HEADS,), w_in_cd.dtype)],
                           axis=-1).astype(BF16)
    assert w_cd.shape[-1] == n_main + LANES
    wts = (norm_g, win, wout, final_g, w_ab, ret_gain, w_out_ab.astype(BF16), w_cd, rel_bias, conv_w, conv_b,
           dt_bias, a_log, d_skip, ssd_gain, w_out_cd.astype(BF16))
    y_p, outs_p = _trunk(x_prompt, 0, wts, None)
    past = (state_ret, cache_sb_k, cache_sb_v, cache_band_k, cache_band_v, state_ssm, state_conv)
    y_s, outs_s = _trunk(x_sample, cache_sb_k.shape[2], wts, past)
    return (y_p, y_s) + outs_p + outs_s


def _swap3(w):
    return jnp.stack([_swap_halves(w[i], RET_HEADS, RET_DK) for i in range(w.shape[0])])
```

```python
import functools
import math

import jax
import jax.numpy as jnp
import numpy as np
from jax import lax
from jax.experimental import pallas as pl
from jax.experimental.pallas import tpu as pltpu

F32 = jnp.float32
BF16 = jnp.bfloat16

CHUNK = 64
EPS = 1e-6
ROPE_BASE = 10000.0
RET_HEADS, RET_DK, RET_DV = 4, 64, 128
SB_HEADS, SB_DH = 8, 64
BAND_HEADS, BAND_DH, BAND_CHUNKS, REL_CLIP = 8, 64, 8, 128
BAND_WINDOW = BAND_CHUNKS * CHUNK
SSD_HEADS, SSD_P, SSD_GROUPS, SSD_N, SSD_CONV = 8, 64, 2, 128, 4
SSD_INNER = SSD_HEADS * SSD_P
SSD_CONV_DIM = SSD_INNER + 2 * SSD_GROUPS * SSD_N
RET_QK_W = RET_HEADS * RET_DK
RET_V_W = RET_HEADS * RET_DV
SB_W = SB_HEADS * SB_DH
BAND_W = BAND_HEADS * BAND_DH
HEAD_SCALE = 0.125

LANES = 128
SUBLANES = 8
VMEM_LIMIT_BYTES = 56 * 1024 * 1024

NEG_BIG = -1e30

ROW_TILE = 1024
PROJ_TILE = 512
FFN_CHUNK = 512
RET_TILE = 256
SB_TILE = 256
BAND_TILE = 128
BAND_GROUP = 2
BAND_SUB = 2
SSD_TILE = 128


def _cparams(*sem):
    return pltpu.CompilerParams(dimension_semantics=sem, vmem_limit_bytes=VMEM_LIMIT_BYTES)


def _rms(x, g):
    return x * lax.rsqrt(jnp.mean(x * x, axis=-1, keepdims=True) + EPS) * g


def _silu(x):
    return x * jax.nn.sigmoid(x)


def _softplus(x):
    return jnp.maximum(x, 0.0) + jnp.log(1.0 + jnp.exp(-jnp.abs(x)))


def _dot(a, b):
    return jnp.dot(a, b, preferred_element_type=F32)


def _dot_nt(a, b):
    return lax.dot_general(a, b, (((1,), (1,)), ((), ())), preferred_element_type=F32)


def _dot_tn(a, b):
    return lax.dot_general(a, b, (((0,), (0,)), ((), ())), preferred_element_type=F32)


def _split_dot(x, m):
    hi = x.astype(BF16)
    lo = (x - hi.astype(F32)).astype(BF16)
    return _dot(hi, m) + _dot(lo, m)


def _lane_iota(shape):
    return lax.broadcasted_iota(jnp.int32, shape, len(shape) - 1)


def _ffn_kernel(*refs, f, has_mix, final):
    refs = list(refs)
    x_ref = refs.pop(0)
    if has_mix:
        ma_ref, mb_ref, wa_ref, wb_ref = refs[:4]
        refs = refs[4:]
    g_ref, win_ref, wout_ref = refs[:3]
    refs = refs[3:]
    if final:
        fg_ref = refs.pop(0)
    o_ref, x1_ref, a_ref = refs

    x = x_ref[...]
    if has_mix:
        x = x + _dot(ma_ref[...], wa_ref[...]) + _dot(mb_ref[...], wb_ref[...])
    x1_ref[...] = x
    h = _rms(x, g_ref[...]).astype(BF16)
    for f0 in range(0, f, FFN_CHUNK):
        f1 = min(f0 + FFN_CHUNK, f)
        gate = _dot(h, win_ref[:, f0:f1])
        up = _dot(h, win_ref[:, f + f0:f + f1])
        a_ref[:, f0:f1] = (_silu(gate) * up).astype(BF16)
    y = x1_ref[...] + 0.5 * _dot(a_ref[...], wout_ref[...])
    if final:
        y = _rms(y, fg_ref[...])
    o_ref[...] = y


def _resident(shape):
    return pl.BlockSpec(shape, lambda *_: (0,) * len(shape), pipeline_mode=pl.Buffered(1))


def _ffn(x, g, win, wout, mix=None, final_g=None):
    t, d = x.shape
    f = wout.shape[0]
    tm = min(ROW_TILE, t)
    assert t % tm == 0 and f % LANES == 0
    row = lambda i: (i, 0)
    args = [x]
    specs = [pl.BlockSpec((tm, d), row)]
    if mix is not None:
        ma, mb, wa, wb = mix
        args += [ma, mb, wa, wb]
        specs += [pl.BlockSpec((tm, ma.shape[1]), row), pl.BlockSpec((tm, mb.shape[1]), row),
                  _resident(wa.shape), _resident(wb.shape)]
    args += [g.reshape(1, d), win, wout]
    specs += [_resident((1, d)), _resident(win.shape), _resident(wout.shape)]
    if final_g is not None:
        args.append(final_g.reshape(1, d))
        specs.append(_resident((1, d)))
    return pl.pallas_call(
        functools.partial(_ffn_kernel, f=f, has_mix=mix is not None, final=final_g is not None),
        grid=(t // tm,),
        in_specs=specs,
        out_specs=pl.BlockSpec((tm, d), row),
        out_shape=jax.ShapeDtypeStruct((t, d), F32),
        scratch_shapes=[pltpu.VMEM((tm, d), F32), pltpu.VMEM((tm, f), BF16)],
        compiler_params=_cparams("parallel"),
        name="ffn",
    )(*args)


def _proj_kernel(*refs, specs, has_rot):
    refs = list(refs)
    x_ref, g_ref, w_ref = refs[:3]
    refs = refs[3:]
    if has_rot:
        cos_ref, sin_ref = refs[:2]
        refs = refs[2:]
    prev_refs = [refs.pop(0) for s in specs if s[5]]
    h = _rms(x_ref[...], g_ref[...]).astype(BF16)
    for c0, width, scale, c_swap, dts, n_prev in specs:
        y = _dot(h, w_ref[:, c0:c0 + width])
        if c_swap:
            half = RET_DK // 2
            lane = _lane_iota((1, width))
            partner = jnp.where(lane % RET_DK < half, pltpu.roll(y, width - half, axis=1),
                                pltpu.roll(y, half, axis=1))
            y = y * cos_ref[...] + partner * sin_ref[...]
        if scale != 1.0:
            y = y * scale
        for n, _ in enumerate(dts):
            o_ref = refs.pop(0)
            if n == 0 and n_prev is not None:
                if n_prev:
                    o_ref[0:n_prev] = prev_refs.pop(0)[...]
                o_ref[n_prev] = y.astype(o_ref.dtype)
            else:
                o_ref[...] = y.astype(o_ref.dtype)


def _proj(x, g, w, specs, rot=None):
    t, d = x.shape
    tm = min(PROJ_TILE, t)
    assert t % tm == 0
    row = lambda i: (i, 0)
    row3 = lambda i: (0, i, 0)
    const2 = lambda i: (0, 0)
    args = [x, g.reshape(1, d), w]
    in_specs = [pl.BlockSpec((tm, d), row), pl.BlockSpec((1, d), const2), _resident(w.shape)]
    if rot is not None:
        cos, sin = rot
        nrot = cos.shape[0] // tm
        rot_map = lambda i: (i % nrot, 0)
        args += [cos, sin]
        in_specs += [pl.BlockSpec((tm, cos.shape[1]), rot_map), pl.BlockSpec((tm, sin.shape[1]), rot_map)]
    kspecs, out_specs, out_shape = [], [], []
    for c0, width, scale, c_swap, dts, prev in specs:
        n_prev = None
        if prev is not None:
            n_prev = 0 if prev is () else prev.shape[0]
            if n_prev:
                args.append(prev)
                in_specs.append(pl.BlockSpec((n_prev, tm, width), row3))
        kspecs.append((c0, width, scale, c_swap, dts, n_prev))
        for n, dt in enumerate(dts):
            if n == 0 and n_prev is not None:
                out_specs.append(pl.BlockSpec((n_prev + 1, tm, width), row3))
                out_shape.append(jax.ShapeDtypeStruct((n_prev + 1, t, width), dt))
            else:
                out_specs.append(pl.BlockSpec((tm, width), row))
                out_shape.append(jax.ShapeDtypeStruct((t, width), dt))
    return pl.pallas_call(
        functools.partial(_proj_kernel, specs=tuple(kspecs), has_rot=rot is not None),
        grid=(t // tm,),
        in_specs=in_specs,
        out_specs=out_specs,
        out_shape=out_shape,
        compiler_params=_cparams("parallel"),
        name="proj",
    )(*args)


def _ret_kernel(q_ref, k_ref, v_ref, rg_ref, gain_ref, s0_ref, dm_ref, qd_ref, kd_ref, cdec_ref,
                o_ref, sn_ref, s_scr):
    c = pl.program_id(1)

    @pl.when(c == 0)
    def _():
        s_scr[...] = s0_ref[0]

    q = q_ref[0]
    k = k_ref[0]
    v = v_ref[0]
    qs = (q.astype(F32) * qd_ref[...]).astype(BF16)
    ks = (k.astype(F32) * kd_ref[...]).astype(BF16)
    lane = _lane_iota((1, RET_QK_W))
    s_all = s_scr[...]
    s_bf = s_all.astype(BF16)
    upd = _dot_tn(ks, v)
    zero = jnp.zeros_like(q)
    heads = range(RET_HEADS)
    hms = [(lane >= h * RET_DK) & (lane < (h + 1) * RET_DK) for h in heads]
    sls = [slice(h * RET_DV, (h + 1) * RET_DV) for h in heads]
    atts = [_dot_nt(jnp.where(hm, q, zero), k) * dm_ref[h] for h, hm in zip(heads, hms)]
    cross = [_dot(jnp.where(hm, qs, zero), s_bf) for hm in hms]
    outs = [_dot(att.astype(BF16), v[:, sl]) + cr for att, sl, cr in zip(atts, sls, cross)]
    outs = [o * lax.rsqrt(jnp.mean(o * o, axis=-1, keepdims=True) + EPS) for o in outs]
    o = jnp.concatenate(outs, axis=1)
    o_ref[0] = (o * gain_ref[...] * _silu(rg_ref[0])).astype(o_ref.dtype)
    for h, sl in zip(heads, sls):
        rows = slice(h * RET_DK, (h + 1) * RET_DK)
        s_scr[rows, :] = s_all[rows, :] * cdec_ref[rows, :] + upd[rows, sl]

    @pl.when(c == pl.num_programs(1) - 1)
    def _():
        sn_ref[0] = s_scr[...]


def _retention(q, k, v, rg, gain, s0, tc):
    b, l, _ = q.shape
    assert l % tc == 0
    f32 = np.float32
    lg = np.log1p(-np.exp2(-5.0 - np.arange(RET_HEADS, dtype=f32))).astype(f32)
    idx = np.arange(tc, dtype=f32)
    diff = idx[:, None] - idx[None, :]
    dm = np.where(diff >= 0, np.exp(lg[:, None, None] * np.maximum(diff, 0.0)), 0.0).astype(f32)
    qd = np.repeat(np.exp(lg[None, :] * (idx[:, None] + 1.0)), RET_DK, axis=1).astype(f32)
    kd = np.repeat(np.exp(lg[None, :] * (tc - 1.0 - idx[:, None])), RET_DK, axis=1).astype(f32)
    cdec = np.broadcast_to(np.repeat(np.exp(lg * f32(tc)), RET_DK)[:, None], (RET_QK_W, RET_DV)).astype(f32)
    blk = lambda w: pl.BlockSpec((1, tc, w), lambda i, c: (i, c, 0))
    const2 = lambda i, c: (0, 0)
    st_spec = pl.BlockSpec((1, RET_QK_W, RET_DV), lambda i, c: (i, 0, 0))
    return pl.pallas_call(
        _ret_kernel,
        grid=(b, l // tc),
        in_specs=[blk(RET_QK_W), blk(RET_QK_W), blk(RET_V_W), blk(RET_V_W),
                  pl.BlockSpec((1, RET_V_W), const2), st_spec,
                  pl.BlockSpec((RET_HEADS, tc, tc), lambda i, c: (0, 0, 0)),
                  pl.BlockSpec((tc, RET_QK_W), const2), pl.BlockSpec((tc, RET_QK_W), const2),
                  pl.BlockSpec((RET_QK_W, RET_DV), const2)],
        out_specs=[blk(RET_V_W), st_spec],
        out_shape=[jax.ShapeDtypeStruct((b, l, RET_V_W), BF16),
                   jax.ShapeDtypeStruct((b, RET_QK_W, RET_DV), F32)],
        scratch_shapes=[pltpu.VMEM((RET_QK_W, RET_DV), F32)],
        compiler_params=_cparams("parallel", "arbitrary"),
        name="retention",
    )(q, k, v, rg, gain.reshape(1, RET_V_W), s0, dm, qd, kd, cdec)


def _cum_matrix(n):
    tri = (np.arange(n)[:, None] >= np.arange(n)[None, :]).astype(BF16)
    return np.concatenate([tri, tri], axis=0)


LOG2E = 1.4426950408889634
SB_UNDERFLOW = -160.0


def _sb_block(qms, kbs, vbs, cum_m, carries, mask, transposed=False):
    if transposed:
        zs = [_dot(qm, kb) for qm, kb in zip(qms, kbs)]
    else:
        zs = [_dot_nt(qm, kb) for qm, kb in zip(qms, kbs)]
    nzs = [-z for z in zs]
    lgs = [jnp.minimum(nz, 0.0) - jnp.log2(1.0 + jnp.exp2(jnp.minimum(z, nz))) for z, nz in zip(zs, nzs)]
    if mask is not None:
        lgs = [jnp.where(mask, lg, 0.0) for lg in lgs]
    his = [lg.astype(BF16) for lg in lgs]
    los = [(lg - hi.astype(F32)).astype(BF16) for lg, hi in zip(lgs, his)]
    incls = [_dot(jnp.concatenate([hi, lo], axis=1), cum_m) for hi, lo in zip(his, los)]
    if carries is None:
        ws = [jnp.exp2(z + incl) for z, incl in zip(zs, incls)]
    else:
        ws = [jnp.exp2(z + incl + c) for z, incl, c in zip(zs, incls, carries)]
    if mask is not None:
        ws = [jnp.where(mask, w, 0.0) for w in ws]
    if transposed:
        outs = [_dot_nt(w.astype(BF16), vb) for w, vb in zip(ws, vbs)]
    else:
        outs = [_dot(w.astype(BF16), vb) for w, vb in zip(ws, vbs)]
    return outs, [incl[:, 0:1] for incl in incls]


SB_PAIRS = 2


def _sb_kernel(q_ref, kd_ref, vd_ref, kp_ref, vp_ref, md_ref, mp_ref, o_ref, *, tq, tkd, tk, blocks_per_q,
               n_past, past_t):
    i = pl.program_id(2)
    lane = _lane_iota((1, LANES))
    sls = [slice(p * LANES, (p + 1) * LANES) for p in range(SB_PAIRS)]
    nh = 2 * SB_PAIRS
    qms = []
    for sl in sls:
        q = q_ref[0, :, sl]
        zero = jnp.zeros_like(q)
        qms += [jnp.where(lane < SB_DH, q, zero), jnp.where(lane >= SB_DH, q, zero)]
    row = lax.broadcasted_iota(jnp.int32, (tq, tkd), 0)
    col = lax.broadcasted_iota(jnp.int32, (tq, tkd), 1)
    dmask = col < row
    nblk = i * blocks_per_q if n_past is None else n_past

    per_head = lambda xs: [x for x in xs for _ in range(2)]
    kds = per_head([kd_ref[0, :, sl].astype(BF16) for sl in sls])
    vds = per_head([vd_ref[0, :, sl].astype(BF16) for sl in sls])
    accs, carries = _sb_block(qms, kds, vds, md_ref[...], None, dmask)

    def live(cs):
        return jnp.max(functools.reduce(jnp.maximum, cs)) > SB_UNDERFLOW

    def cond(st):
        return (st[0] >= 0) & st[1]

    def body(st):
        j, accs, cs = st[0], st[2:2 + nh], st[2 + nh:]
        off = pl.multiple_of(j * tk, tk)
        if past_t:
            kbs = per_head([kp_ref[0, sl, pl.ds(off, tk)].astype(BF16) for sl in sls])
            vbs = per_head([vp_ref[0, sl, pl.ds(off, tk)].astype(BF16) for sl in sls])
        else:
            kbs = per_head([kp_ref[0, pl.ds(off, tk), sl].astype(BF16) for sl in sls])
            vbs = per_head([vp_ref[0, pl.ds(off, tk), sl].astype(BF16) for sl in sls])
        ds, ts = _sb_block(qms, kbs, vbs, mp_ref[...], cs, None, transposed=past_t)
        cs = [c + t for c, t in zip(cs, ts)]
        return (j - 1, live(cs), *[a + d for a, d in zip(accs, ds)], *cs)

    st = lax.while_loop(cond, body, (jnp.asarray(nblk - 1, jnp.int32), live(carries), *accs, *carries))
    pairs = [jnp.where(lane < SB_DH, st[2 + 2 * p], st[3 + 2 * p]) for p in range(SB_PAIRS)]
    o_ref[0] = jnp.concatenate(pairs, axis=1).astype(o_ref.dtype)


def _stick_breaking(q, k_own, v_own, own_layer, k_past, v_past, past_layer, prompt):
    b, lq, w = q.shape
    wb = SB_PAIRS * LANES
    if prompt:
        tq = tkd = tk = min(SB_TILE, lq)
        blocks_per_q, n_past = 1, None
        lp = k_past.shape[2]
        pspec = pl.BlockSpec((None, 1, lp, wb), lambda i, p, j: (past_layer, i, 0, p))
    else:
        tq = lq
        tkd = k_own.shape[2]
        tk = SB_TILE
        lp = k_past.shape[3]
        assert lp % tk == 0
        blocks_per_q, n_past = 0, lp // tk
        pspec = pl.BlockSpec((None, 1, wb, lp), lambda i, p, j: (past_layer, i, p, 0))
    assert lq % tq == 0
    qspec = pl.BlockSpec((1, tq, wb), lambda i, p, j: (i, j, p))
    dspec = pl.BlockSpec((None, 1, tkd, wb), lambda i, p, j: (own_layer, i, j, p))
    md = _cum_matrix(tkd)
    mp = _cum_matrix(tk)
    const2 = lambda i, p, j: (0, 0)
    return pl.pallas_call(
        functools.partial(_sb_kernel, tq=tq, tkd=tkd, tk=tk, blocks_per_q=blocks_per_q, n_past=n_past,
                          past_t=not prompt),
        grid=(b, w // wb, lq // tq),
        in_specs=[qspec, dspec, dspec, pspec, pspec, pl.BlockSpec(md.shape, const2), pl.BlockSpec(mp.shape, const2)],
        out_specs=qspec,
        out_shape=jax.ShapeDtypeStruct((b, lq, w), BF16),
        compiler_params=_cparams("parallel", "parallel", "arbitrary"),
        name="stick_breaking",
    )(q, k_own, v_own, k_past, v_past, md, mp)


def _band_kernel(q_ref, k_ref, v_ref, bm_ref, o_ref, *, tq, span, sub):
    lane = _lane_iota((1, LANES))
    npair = BAND_W // LANES
    sls = [slice(pr * LANES, (pr + 1) * LANES) for pr in range(npair)]
    for s in range(sub):
        rows = slice(s * tq, (s + 1) * tq)
        tile = pl.program_id(1) * sub + s
        ws = pl.multiple_of(jnp.maximum(tile * tq - (span - tq), 0), LANES)
        pairs = []
        for g0 in range(0, npair, BAND_GROUP):
            group = range(g0, g0 + BAND_GROUP)
            scores = []
            for pr in group:
                kw = k_ref[0, pl.ds(ws, span), sls[pr]]
                q = q_ref[0, rows, sls[pr]]
                zero = jnp.zeros_like(q)
                q2 = jnp.concatenate([jnp.where(lane < BAND_DH, q, zero), jnp.where(lane >= BAND_DH, q, zero)],
                                     axis=0)
                bias2 = jnp.concatenate([bm_ref[0, s, 2 * pr], bm_ref[0, s, 2 * pr + 1]], axis=0)
                scores.append(_dot_nt(q2, kw) + bias2)
            probs = [jnp.exp(sc - jnp.max(sc, axis=-1, keepdims=True)) for sc in scores]
            for n, pr in enumerate(group):
                vw = v_ref[0, pl.ds(ws, span), sls[pr]]
                p = probs[n]
                out2 = _dot(p.astype(BF16), vw) / jnp.sum(p, axis=-1, keepdims=True)
                pairs.append(jnp.where(lane < BAND_DH, out2[:tq], out2[tq:]))
        o_ref[0, rows, :] = jnp.concatenate(pairs, axis=1).astype(o_ref.dtype)


def _band_bias(rel_bias, q0, k0, tq, span, n_keys):
    d0 = q0 - k0
    n = span + tq
    padw = n + abs(d0)
    tab = rel_bias.astype(F32)
    heads = tab.shape[0]
    ext = jnp.concatenate([jnp.broadcast_to(tab[:, :1], (heads, padw)), tab,
                           jnp.broadcast_to(tab[:, -1:], (heads, padw))], axis=1)
    base = padw + REL_CLIP + d0
    g = jnp.concatenate([jnp.flip(ext[:, base - span + 1:base + 1], axis=1),
                         jnp.flip(ext[:, base + 1:base + tq + 1], axis=1)], axis=1)
    toe = jnp.tile(g, (1, tq))[:, :tq * (n - 1)].reshape(heads, tq, n - 1)[:, :, :span]
    q_pos = q0 + np.arange(tq)[:, None]
    k_pos = k0 + np.arange(span)[None, :]
    qc, kc = q_pos // CHUNK, k_pos // CHUNK
    mask = (np.arange(span)[None, :] < n_keys) & (k_pos >= 0) & (kc <= qc) & (kc >= qc - BAND_CHUNKS)
    return jnp.where(jnp.asarray(mask)[None], toe, NEG_BIG)


def _band(q, k, v, bm, tq, span):
    b, lq, w = q.shape
    nd, heads = bm.shape[:2]
    lk = k.shape[1]
    sub = BAND_SUB if (lq // tq) % BAND_SUB == 0 else 1
    nsteps = -(-(nd - 1) // sub) + 1
    bm = jnp.stack([jnp.stack([bm[min(sub * j + s, nd - 1)] for s in range(sub)]) for j in range(nsteps)])
    qspec = pl.BlockSpec((1, sub * tq, w), lambda i, j: (i, j, 0))
    kspec = pl.BlockSpec((1, lk, w), lambda i, j: (i, 0, 0))
    bspec = pl.BlockSpec((1, sub, heads, tq, span), lambda i, j: (jnp.minimum(j, nsteps - 1), 0, 0, 0, 0))
    return pl.pallas_call(
        functools.partial(_band_kernel, tq=tq, span=span, sub=sub),
        grid=(b, lq // (sub * tq)),
        in_specs=[qspec, kspec, kspec, bspec],
        out_specs=qspec,
        out_shape=jax.ShapeDtypeStruct((b, lq, w), BF16),
        compiler_params=_cparams("parallel", "arbitrary"),
        name="band",
    )(q, k, v, bm)


def _ssd_kernel(xbc_ref, z_ref, dt_ref, cw_ref, cb_ref, dtb_ref, alog_ref, dskip_ref, gain_ref, st0_ref,
                cbuf_ref, tri_ref, e_ref, y_ref, stn_ref, st_scr, xwin, *, tc, valid_len):
    c = pl.program_id(1)

    @pl.when(c == 0)
    def _():
        st_scr[...] = st0_ref[0]
        xwin[0:SUBLANES, :] = cbuf_ref[0]

    u = xbc_ref[0]
    xwin[SUBLANES:SUBLANES + tc, :] = u
    acc = cb_ref[...] + cw_ref[SSD_CONV - 1:SSD_CONV, :] * u
    for i in range(SSD_CONV - 1):
        off = SUBLANES - (SSD_CONV - 1) + i
        acc = acc + cw_ref[i:i + 1, :] * xwin[off:off + tc, :]
    xwin[0:SUBLANES, :] = u[tc - SUBLANES:, :]
    xc = _silu(acc)
    xs = xc[:, :SSD_INNER]

    rowid = c * tc + lax.broadcasted_iota(jnp.int32, (tc, LANES), 0)
    dt = jnp.where(rowid < valid_len, _softplus(dt_ref[0] + dtb_ref[...]), 0.0)
    da = dt * (-jnp.exp(alog_ref[...]))
    cum = _split_dot_left(tri_ref[...], da)
    dt_b = _split_dot(dt, e_ref[...])
    cum_b = _split_dot(cum, e_ref[...])
    last_b = cum_b[tc - 1:tc, :]
    xd = xs * dt_b
    xw = (xd * jnp.exp(last_b - cum_b)).astype(BF16)
    xd_bf = xd.astype(BF16)
    cum_t = cum.T

    lane = _lane_iota((1, LANES))
    ri = lax.broadcasted_iota(jnp.int32, (tc, tc), 0)
    ci = lax.broadcasted_iota(jnp.int32, (tc, tc), 1)
    causal = ri >= ci
    heads_per_group = SSD_HEADS // SSD_GROUPS
    gw = heads_per_group * SSD_P
    gls = [slice(g * gw, (g + 1) * gw) for g in range(SSD_GROUPS)]
    bgs = [xc[:, SSD_INNER + g * SSD_N:SSD_INNER + (g + 1) * SSD_N].astype(BF16) for g in range(SSD_GROUPS)]
    cgs = [xc[:, SSD_INNER + (SSD_GROUPS + g) * SSD_N:SSD_INNER + (SSD_GROUPS + g + 1) * SSD_N].astype(BF16)
           for g in range(SSD_GROUPS)]
    cbs = [_dot_nt(cg, bg) for cg, bg in zip(cgs, bgs)]
    y_inter = jnp.concatenate([_dot(cg, st_scr[:, gl].astype(BF16)) for cg, gl in zip(cgs, gls)], axis=1)
    s_upd = [_dot_tn(bg, xw[:, gl]) for bg, gl in zip(bgs, gls)]
    atts = []
    for pidx in range(SSD_HEADS // 2):
        cpair = cum_b[:, pidx * LANES:(pidx + 1) * LANES]
        rolled = pltpu.roll(cpair, SSD_P, axis=1)
        for hh in range(2):
            h = 2 * pidx + hh
            colv = jnp.where((lane >= SSD_P) if hh else (lane < SSD_P), cpair, rolled)
            seg = jnp.exp(jnp.where(causal, colv - cum_t[h:h + 1, :], NEG_BIG))
            atts.append((cbs[h // heads_per_group] * seg).astype(BF16))
    pairs = []
    for pidx in range(SSD_HEADS // 2):
        xdp = xd_bf[:, pidx * LANES:(pidx + 1) * LANES]
        pairs.append(jnp.where(lane < SSD_P, _dot(atts[2 * pidx], xdp), _dot(atts[2 * pidx + 1], xdp)))
    y_intra = jnp.concatenate(pairs, axis=1)
    for gl, upd in zip(gls, s_upd):
        st_scr[:, gl] = st_scr[:, gl] * jnp.exp(last_b[:, gl]) + upd
    y = y_intra + y_inter * jnp.exp(cum_b) + dskip_ref[...] * xs
    y = _rms(y * _silu(z_ref[0]), gain_ref[...])
    y_ref[0] = y.astype(y_ref.dtype)

    @pl.when(c == pl.num_programs(1) - 1)
    def _():
        stn_ref[0] = st_scr[...]


def _split_dot_left(m, x):
    hi = x.astype(BF16)
    lo = (x - hi.astype(F32)).astype(BF16)
    return _dot(m, hi) + _dot(m, lo)


def _ssd(xbc, z, dt, conv_w, conv_b, dt_bias, a_log, d_skip, gain, st0, cbuf, valid_len):
    b, l, _ = xbc.shape
    tc = SSD_TILE
    assert l % tc == 0
    pad = LANES - SSD_HEADS
    tri = (np.arange(tc)[:, None] >= np.arange(tc)[None, :]).astype(BF16)
    expand = (np.arange(LANES)[:, None] == (np.arange(SSD_INNER) // SSD_P)[None, :]).astype(BF16)
    blk = lambda w: pl.BlockSpec((1, tc, w), lambda i, c: (i, c, 0))
    const2 = lambda i, c: (0, 0)
    cs = lambda a: pl.BlockSpec(a.shape, const2)
    st_spec = pl.BlockSpec((1, SSD_N, SSD_INNER), lambda i, c: (i, 0, 0))
    params = [conv_w, conv_b.reshape(1, -1),
              jnp.pad(dt_bias.astype(F32), (0, pad)).reshape(1, LANES),
              jnp.pad(a_log.astype(F32), (0, pad)).reshape(1, LANES),
              jnp.repeat(d_skip.astype(F32), SSD_P).reshape(1, SSD_INNER),
              gain.reshape(1, SSD_INNER)]
    return pl.pallas_call(
        functools.partial(_ssd_kernel, tc=tc, valid_len=valid_len),
        grid=(b, l // tc),
        in_specs=[blk(SSD_CONV_DIM), blk(SSD_INNER), blk(LANES)] + [cs(p) for p in params]
                 + [st_spec, pl.BlockSpec((1, SUBLANES, SSD_CONV_DIM), lambda i, c: (i, 0, 0)), cs(tri), cs(expand)],
        out_specs=[blk(SSD_INNER), st_spec],
        out_shape=[jax.ShapeDtypeStruct((b, l, SSD_INNER), BF16),
                   jax.ShapeDtypeStruct((b, SSD_N, SSD_INNER), F32)],
        scratch_shapes=[pltpu.VMEM((SSD_N, SSD_INNER), F32), pltpu.VMEM((SUBLANES + tc, SSD_CONV_DIM), F32)],
        compiler_params=_cparams("parallel", "arbitrary"),
        name="ssd",
    )(xbc, z, dt, *params, st0, cbuf, tri, expand)


def _rot_tables(pos, rows):
    half = RET_DK // 2
    inv = np.power(np.float32(ROPE_BASE), -np.arange(half, dtype=np.float32) / np.float32(half))
    ang = pos.astype(np.float32)[:, None] * inv[None, :]
    cos = np.tile(np.cos(ang), (rows // pos.shape[0], 2 * RET_HEADS))
    sin = np.sin(ang)
    sin = np.tile(np.concatenate([-sin, sin], axis=1), (rows // pos.shape[0], RET_HEADS))
    return cos.astype(np.float32), sin.astype(np.float32)


def _ab_layer(x, b, l, pos0, g, w_ext, ret_gain, past, sb_prev):
    t = b * l
    rows = max(l, min(PROJ_TILE, t))
    rot = _rot_tables(pos0 + np.arange(l, dtype=np.int32), rows)
    o = [0]
    for wdt in (RET_QK_W, RET_QK_W, RET_V_W, RET_V_W, SB_W, SB_W, SB_W):
        o.append(o[-1] + wdt)
    specs = [(o[0], RET_QK_W, HEAD_SCALE, True, (BF16,), None),
             (o[1], RET_QK_W, 1.0, True, (BF16,), None),
             (o[2], RET_V_W, 1.0, None, (BF16,), None), (o[3], RET_V_W, 1.0, None, (F32,), None),
             (o[4], SB_W, HEAD_SCALE * LOG2E, None, (BF16,), None), (o[5], SB_W, 1.0, None, (F32,), sb_prev[0]),
             (o[6], SB_W, 1.0, None, (F32,), sb_prev[1])]
    rq, rk, rv, rg, sq, sk, sv = _proj(x, g, w_ext, specs, rot)
    n_sb = sk.shape[0]
    sk4, sv4 = sk.reshape(n_sb, b, l, SB_W), sv.reshape(n_sb, b, l, SB_W)
    r3 = lambda a: a.reshape(b, l, a.shape[-1])
    if past is None:
        s0 = jnp.zeros((b, RET_QK_W, RET_DV), F32)
        tc = min(RET_TILE, l)
    else:
        s0 = past[0].astype(F32).reshape(b, RET_QK_W, RET_DV)
        tc = l
    ro, s_new = _retention(r3(rq), r3(rk), r3(rv), r3(rg), ret_gain, s0, tc)
    if past is None:
        so = _stick_breaking(r3(sq), sk4, sv4, n_sb - 1, sk4, sv4, n_sb - 1, prompt=True)
    else:
        (ck_all, layer), (cv_all, _) = past[1], past[2]
        ck = jnp.transpose(ck_all, (0, 1, 3, 4, 2)).reshape(ck_all.shape[0], b, SB_W, -1)
        cv = jnp.transpose(cv_all, (0, 1, 3, 4, 2)).reshape(cv_all.shape[0], b, SB_W, -1)
        padn = (-l) % LANES
        pad_own = lambda a: jnp.pad(a[-1:], ((0, 0), (0, 0), (0, padn), (0, 0)))
        so = _stick_breaking(r3(sq), pad_own(sk4), pad_own(sv4), 0, ck, cv, layer, prompt=False)
    return ro.reshape(t, RET_V_W), so.reshape(t, SB_W), s_new.reshape(b, RET_HEADS, RET_DK, RET_DV), (sk, sv)


def _cd_layer(x, b, l, pos0, g, w_ext, rel_bias, conv_w, conv_b, dt_bias, a_log, d_skip, ssd_gain, past):
    t = b * l
    o = [0]
    for wdt in (BAND_W, BAND_W, BAND_W, SSD_INNER, SSD_CONV_DIM, LANES):
        o.append(o[-1] + wdt)
    specs = [(o[0], BAND_W, HEAD_SCALE, None, (BF16,), None), (o[1], BAND_W, 1.0, None, (F32, BF16), None),
             (o[2], BAND_W, 1.0, None, (F32, BF16), None), (o[3], SSD_INNER, 1.0, None, (F32,), None),
             (o[4], SSD_CONV_DIM, 1.0, None, (F32,), None), (o[5], LANES, 1.0, None, (F32,), None)]
    bq, bk, bk16, bv, bv16, z, xbc, dt = _proj(x, g, w_ext, specs)
    r3 = lambda a: a.reshape(b, l, a.shape[-1])
    bk3, bv3, xbc3 = r3(bk), r3(bv), r3(xbc)
    if past is None:
        tq = min(BAND_TILE, l)
        span = BAND_WINDOW + tq
        nd = BAND_WINDOW // tq + 1
        bm = jnp.stack([_band_bias(rel_bias, d * tq, max(d * tq - BAND_WINDOW, 0), tq, span, span)
                        for d in range(nd)])
        bo = _band(r3(bq), r3(bk16), r3(bv16), bm, tq, span)
        keep = min(BAND_WINDOW, l)
        new_k, new_v = bk3[:, l - keep:], bv3[:, l - keep:]
        st0 = jnp.zeros((b, SSD_N, SSD_INNER), F32)
        cbuf = jnp.zeros((b, SUBLANES, SSD_CONV_DIM), F32)
        lp = l
    else:
        ck = past[0].reshape(b, -1, BAND_W)
        cv = past[1].reshape(b, -1, BAND_W)
        wlen = ck.shape[1]
        span = -(-(wlen + l) // LANES) * LANES
        padk = span - wlen - l
        k_all = jnp.concatenate([ck.astype(BF16), r3(bk16), jnp.zeros((b, padk, BAND_W), BF16)], axis=1)
        v_all = jnp.concatenate([cv.astype(BF16), r3(bv16), jnp.zeros((b, padk, BAND_W), BF16)], axis=1)
        bm = _band_bias(rel_bias, pos0, pos0 - wlen, l, span, wlen + l)[None]
        bo = _band(r3(bq), k_all, v_all, bm, l, span)
        new_k, new_v = bk3, bv3
        st0 = jnp.transpose(past[2].astype(F32), (0, 3, 1, 2)).reshape(b, SSD_N, SSD_INNER)
        cbuf = jnp.pad(past[3].astype(F32), ((0, 0), (SUBLANES - (SSD_CONV - 1), 0), (0, 0)))
        lp = -(-l // SSD_TILE) * SSD_TILE
    assert l >= SSD_CONV - 1
    conv_new = xbc3[:, l - (SSD_CONV - 1):]
    padl = lp - l
    pad3 = lambda a: jnp.pad(a, ((0, 0), (0, padl), (0, 0))) if padl else a
    y, st_new = _ssd(pad3(xbc3), pad3(r3(z)), pad3(r3(dt)), conv_w, conv_b, dt_bias, a_log, d_skip, ssd_gain,
                     st0, cbuf, l)
    y = y[:, :l]
    ssm_new = jnp.transpose(st_new.reshape(b, SSD_N, SSD_HEADS, SSD_P), (0, 2, 3, 1))
    outs = (new_k.reshape(b, -1, BAND_HEADS, BAND_DH), new_v.reshape(b, -1, BAND_HEADS, BAND_DH), ssm_new, conv_new)
    return bo.reshape(t, BAND_W), y.reshape(t, SSD_INNER), outs


def _trunk(x3, pos0, wts, past):
    (norm_g, win, wout, final_g, w_ab, ret_gain, wo_ab, w_cd, rel_bias, conv_w, conv_b, dt_bias, a_log,
     d_skip, ssd_gain, wo_cd) = wts
    b, l, d = x3.shape
    depth = norm_g.shape[0]
    x = x3.reshape(b * l, d)
    outs = ([], None, None, [], [], [], [])
    sb_stack = ((), ())
    x = _ffn(x, norm_g[0, 0], win[0, 0], wout[0, 0])
    for layer in range(depth):
        j = layer // 2
        if layer % 2 == 0:
            pst = None if past is None else (past[0][j], (past[1], j), (past[2], j))
            ma, mb, r, sb_stack = _ab_layer(x, b, l, pos0, norm_g[layer, 1], w_ab[j], ret_gain[j], pst, sb_stack)
            outs[0].append(r)
            wo = wo_ab[j]
        else:
            pst = None if past is None else (past[3][j], past[4][j], past[5][j], past[6][j])
            ma, mb, (k, v, s, cb) = _cd_layer(x, b, l, pos0, norm_g[layer, 1], w_cd[j], rel_bias[j], conv_w[j],
                                              conv_b[j], dt_bias[j], a_log[j], d_skip[j], ssd_gain[j], pst)
            outs[3].append(k)
            outs[4].append(v)
            outs[5].append(s)
            outs[6].append(cb)
            wo = wo_cd[j]
        half = ma.shape[1]
        mix = (ma, mb, wo[:half], wo[half:])
        x = _ffn(x, norm_g[layer, 2], win[layer, 1], wout[layer, 1], mix=mix,
                 final_g=final_g if layer + 1 == depth else None)
        if layer + 1 < depth:
            x = _ffn(x, norm_g[layer + 1, 0], win[layer + 1, 0], wout[layer + 1, 0])
    sb_k, sb_v = (jnp.transpose(jnp.swapaxes(a.reshape(a.shape[0], b, l, SB_W), 2, 3)
                                .reshape(a.shape[0], b, SB_HEADS, SB_DH, l), (0, 1, 4, 2, 3)) for a in sb_stack)
    stacked = [jnp.stack(o) for o in outs if o is not None]
    return x.reshape(b, l, d), (stacked[0], sb_k, sb_v, *stacked[1:])


def kernel(x_prompt, x_sample, state_ret, cache_sb_k, cache_sb_v, cache_band_k, cache_band_v, state_ssm, state_conv, norm_g, ffn_w_in, ffn_w_out, final_g, w_in_ab, ret_gain, w_out_ab, w_in_cd, rel_bias, conv_w, conv_b, dt_bias, a_log, d_skip, ssd_gain, w_out_cd):
    win = ffn_w_in.astype(BF16)
    wout = ffn_w_out.astype(BF16)
    w_ab = w_in_ab.astype(BF16)
    n_main = 3 * BAND_W + SSD_INNER + SSD_CONV_DIM
    w_cd = jnp.concatenate([w_in_cd, jnp.zeros(w_in_cd.shape[:2] + (LANES - SSD_HEADS,), w_in_cd.dtype)],
                           axis=-1).astype(BF16)
    assert w_cd.shape[-1] == n_main + LANES
    wts = (norm_g, win, wout, final_g, w_ab, ret_gain, w_out_ab.astype(BF16), w_cd, rel_bias, conv_w, conv_b,
           dt_bias, a_log, d_skip, ssd_gain, w_out_cd.astype(BF16))
    y_p, outs_p = _trunk(x_prompt, 0, wts, None)
    past = (state_ret, cache_sb_k, cache_sb_v, cache_band_k, cache_band_v, state_ssm, state_conv)
    y_s, outs_s = _trunk(x_sample, cache_sb_k.shape[2], wts, past)
    return (y_p, y_s) + outs_p + outs_s
```

```python
import functools
import math

import jax
import jax.numpy as jnp
import numpy as np
from jax import lax
from jax.experimental import pallas as pl
from jax.experimental.pallas import tpu as pltpu

F32 = jnp.float32
BF16 = jnp.bfloat16

CHUNK = 64
EPS = 1e-6
ROPE_BASE = 10000.0
RET_HEADS, RET_DK, RET_DV = 4, 64, 128
SB_HEADS, SB_DH = 8, 64
BAND_HEADS, BAND_DH, BAND_CHUNKS, REL_CLIP = 8, 64, 8, 128
BAND_WINDOW = BAND_CHUNKS * CHUNK
SSD_HEADS, SSD_P, SSD_GROUPS, SSD_N, SSD_CONV = 8, 64, 2, 128, 4
SSD_INNER = SSD_HEADS * SSD_P
SSD_CONV_DIM = SSD_INNER + 2 * SSD_GROUPS * SSD_N
RET_QK_W = RET_HEADS * RET_DK
RET_V_W = RET_HEADS * RET_DV
SB_W = SB_HEADS * SB_DH
BAND_W = BAND_HEADS * BAND_DH
HEAD_SCALE = 0.125

LANES = 128
SUBLANES = 8
VMEM_LIMIT_BYTES = 56 * 1024 * 1024

NEG_BIG = -1e30

ROW_TILE = 1024
PROJ_TILE = 512
FUSED_TILE = 512
FFN_CHUNK = 512
RET_TILE = 256
SB_TILE = 256
BAND_TILE = 128
BAND_GROUP = 2
BAND_SUB = 2
SSD_TILE = 128


def _cparams(*sem):
    return pltpu.CompilerParams(dimension_semantics=sem, vmem_limit_bytes=VMEM_LIMIT_BYTES)


def _rms(x, g):
    return x * lax.rsqrt(jnp.mean(x * x, axis=-1, keepdims=True) + EPS) * g


def _silu(x):
    return x * jax.nn.sigmoid(x)


def _softplus(x):
    return jnp.maximum(x, 0.0) + jnp.log(1.0 + jnp.exp(-jnp.abs(x)))


def _dot(a, b):
    return jnp.dot(a, b, preferred_element_type=F32)


def _dot_nt(a, b):
    return lax.dot_general(a, b, (((1,), (1,)), ((), ())), preferred_element_type=F32)


def _dot_tn(a, b):
    return lax.dot_general(a, b, (((0,), (0,)), ((), ())), preferred_element_type=F32)


def _split_dot(x, m):
    hi = x.astype(BF16)
    lo = (x - hi.astype(F32)).astype(BF16)
    return _dot(hi, m) + _dot(lo, m)


def _lane_iota(shape):
    return lax.broadcasted_iota(jnp.int32, shape, len(shape) - 1)


def _ffn_kernel(*refs, f, has_mix, final):
    refs = list(refs)
    x_ref = refs.pop(0)
    if has_mix:
        ma_ref, mb_ref, wa_ref, wb_ref = refs[:4]
        refs = refs[4:]
    g_ref, win_ref, wout_ref = refs[:3]
    refs = refs[3:]
    if final:
        fg_ref = refs.pop(0)
    o_ref, x1_ref, a_ref = refs

    x = x_ref[...]
    if has_mix:
        x = x + _dot(ma_ref[...], wa_ref[...]) + _dot(mb_ref[...], wb_ref[...])
    o_ref[...] = _ffn_body(x, g_ref, win_ref, wout_ref, fg_ref if final else None, x1_ref, a_ref, f)


def _ffn_body(x, g_ref, win_ref, wout_ref, fg_ref, x1_ref, a_ref, f, side_jobs=()):
    x1_ref[...] = x
    h = _rms(x, g_ref[...]).astype(BF16)
    chunks = list(range(0, f, FFN_CHUNK))
    slots = {}
    for k, job in enumerate(side_jobs):
        slots.setdefault(k * (len(chunks) + 1) // len(side_jobs), []).append(job)
    for n, f0 in enumerate(chunks):
        for job in slots.get(n, ()):
            job()
        f1 = min(f0 + FFN_CHUNK, f)
        gate = _dot(h, win_ref[:, f0:f1])
        up = _dot(h, win_ref[:, f + f0:f + f1])
        a_ref[:, f0:f1] = (_silu(gate) * up).astype(BF16)
    for job in slots.get(len(chunks), ()):
        job()
    y = x1_ref[...] + 0.5 * _dot(a_ref[...], wout_ref[...])
    if fg_ref is not None:
        y = _rms(y, fg_ref[...])
    return y


def _ffn_ssd_kernel(*refs, f, final, tc, tiles_per_batch, seq_len):
    refs = list(refs)
    x_ref, ma_ref, wa_ref, wb_ref, g_ref, win_ref, wout_ref = refs[:7]
    refs = refs[7:]
    fg_ref = refs.pop(0) if final else None
    xbc_ref, z_ref, dt_ref = refs[:3]
    prm = refs[3:9]
    st0_ref, cbuf_ref, tri_ref, e_ref, o_ref, stn_ref, x1_ref, a_ref, st_scr, xwin, ybuf = refs[9:]
    i = pl.program_id(0)
    n = pl.num_programs(0) - 1
    tile = i % tiles_per_batch
    tm = x_ref.shape[0]

    @pl.when(i == 0)
    def _():
        ybuf[...] = jnp.zeros_like(ybuf)

    @pl.when((tile == 0) & (i < n))
    def _():
        st_scr[...] = st0_ref[0]
        xwin[0:SUBLANES, :] = cbuf_ref[0]

    def ssd_job(c):
        def job():
            rows = slice(c * tc, (c + 1) * tc)
            ybuf[rows, :] = _ssd_chunk(xbc_ref[rows, :], z_ref[rows, :], dt_ref[rows, :], tile * tm + c * tc,
                                       seq_len, prm, st_scr, xwin, tri_ref, e_ref)
        return job

    x = x_ref[...] + _dot(ma_ref[...], wa_ref[...]) + _dot(ybuf[...], wb_ref[...])
    o_ref[...] = _ffn_body(x, g_ref, win_ref, wout_ref, fg_ref, x1_ref, a_ref, f,
                           side_jobs=[ssd_job(c) for c in range(tm // tc)])

    @pl.when((tile == tiles_per_batch - 1) & (i < n))
    def _():
        stn_ref[0] = st_scr[...]


def _resident(shape):
    return pl.BlockSpec(shape, lambda *_: (0,) * len(shape), pipeline_mode=pl.Buffered(1))


def _ffn(x, g, win, wout, mix=None, final_g=None):
    t, d = x.shape
    f = wout.shape[0]
    tm = min(ROW_TILE, t)
    assert t % tm == 0 and f % LANES == 0
    row = lambda i: (i, 0)
    args = [x]
    specs = [pl.BlockSpec((tm, d), row)]
    if mix is not None:
        ma, mb, wa, wb = mix
        args += [ma, mb, wa, wb]
        specs += [pl.BlockSpec((tm, ma.shape[1]), row), pl.BlockSpec((tm, mb.shape[1]), row),
                  _resident(wa.shape), _resident(wb.shape)]
    args += [g.reshape(1, d), win, wout]
    specs += [_resident((1, d)), _resident(win.shape), _resident(wout.shape)]
    if final_g is not None:
        args.append(final_g.reshape(1, d))
        specs.append(_resident((1, d)))
    return pl.pallas_call(
        functools.partial(_ffn_kernel, f=f, has_mix=mix is not None, final=final_g is not None),
        grid=(t // tm,),
        in_specs=specs,
        out_specs=pl.BlockSpec((tm, d), row),
        out_shape=jax.ShapeDtypeStruct((t, d), F32),
        scratch_shapes=[pltpu.VMEM((tm, d), F32), pltpu.VMEM((tm, f), BF16)],
        compiler_params=_cparams("parallel"),
        name="ffn",
    )(*args)


def _ffn_ssd(x, g, win, wout, ma, wa, wb, final_g, xbc, z, dt, ssd_prm, st0, cbuf, seq_len):
    t, d = x.shape
    f = wout.shape[0]
    tm = FUSED_TILE
    assert t % tm == 0 and seq_len % tm == 0 and tm % SSD_TILE == 0
    n = t // tm
    tpb = seq_len // tm
    params, tri, expand = ssd_prm
    lag = lambda i: (jnp.maximum(i - 1, 0), 0)
    cur = lambda i: (jnp.minimum(i, n - 1), 0)
    bat = lambda i: (jnp.minimum(i, n - 1) // tpb, 0, 0)
    args = [x, ma, wa, wb, g.reshape(1, d), win, wout]
    specs = [pl.BlockSpec((tm, d), lag), pl.BlockSpec((tm, ma.shape[1]), lag), _resident(wa.shape),
             _resident(wb.shape), _resident((1, d)), _resident(win.shape), _resident(wout.shape)]
    if final_g is not None:
        args.append(final_g.reshape(1, d))
        specs.append(_resident((1, d)))
    args += [xbc, z, dt, *params, st0, cbuf, tri, expand]
    specs += [pl.BlockSpec((tm, xbc.shape[1]), cur), pl.BlockSpec((tm, z.shape[1]), cur),
              pl.BlockSpec((tm, dt.shape[1]), cur)] + [_resident(p.shape) for p in params]
    specs += [pl.BlockSpec((1,) + st0.shape[1:], bat), pl.BlockSpec((1,) + cbuf.shape[1:], bat),
              _resident(tri.shape), _resident(expand.shape)]
    return pl.pallas_call(
        functools.partial(_ffn_ssd_kernel, f=f, final=final_g is not None, tc=SSD_TILE, tiles_per_batch=tpb,
                          seq_len=seq_len),
        grid=(n + 1,),
        in_specs=specs,
        out_specs=[pl.BlockSpec((tm, d), lag), pl.BlockSpec((1,) + st0.shape[1:], bat)],
        out_shape=[jax.ShapeDtypeStruct((t, d), F32), jax.ShapeDtypeStruct(st0.shape, F32)],
        scratch_shapes=[pltpu.VMEM((tm, d), F32), pltpu.VMEM((tm, f), BF16), pltpu.VMEM(st0.shape[1:], F32),
                        pltpu.VMEM((SUBLANES + SSD_TILE, xbc.shape[1]), F32), pltpu.VMEM((tm, SSD_INNER), BF16)],
        compiler_params=_cparams("arbitrary"),
        name="ffn_ssd",
    )(*args)


def _proj_kernel(*refs, specs, has_rot):
    refs = list(refs)
    x_ref, g_ref, w_ref = refs[:3]
    refs = refs[3:]
    if has_rot:
        cos_ref, sin_ref = refs[:2]
        refs = refs[2:]
    prev_refs = [refs.pop(0) for s in specs if s[5]]
    h = _rms(x_ref[...], g_ref[...]).astype(BF16)
    for c0, width, scale, c_swap, dts, n_prev in specs:
        y = _dot(h, w_ref[:, c0:c0 + width])
        if c_swap:
            half = RET_DK // 2
            lane = _lane_iota((1, width))
            partner = jnp.where(lane % RET_DK < half, pltpu.roll(y, width - half, axis=1),
                                pltpu.roll(y, half, axis=1))
            y = y * cos_ref[...] + partner * sin_ref[...]
        if scale != 1.0:
            y = y * scale
        for n, _ in enumerate(dts):
            o_ref = refs.pop(0)
            if n == 0 and n_prev is not None:
                if n_prev:
                    o_ref[0:n_prev] = prev_refs.pop(0)[...]
                o_ref[n_prev] = y.astype(o_ref.dtype)
            else:
                o_ref[...] = y.astype(o_ref.dtype)


def _proj(x, g, w, specs, rot=None):
    t, d = x.shape
    tm = min(PROJ_TILE, t)
    assert t % tm == 0
    row = lambda i: (i, 0)
    row3 = lambda i: (0, i, 0)
    const2 = lambda i: (0, 0)
    args = [x, g.reshape(1, d), w]
    in_specs = [pl.BlockSpec((tm, d), row), pl.BlockSpec((1, d), const2), _resident(w.shape)]
    if rot is not None:
        cos, sin = rot
        nrot = cos.shape[0] // tm
        rot_map = lambda i: (i % nrot, 0)
        args += [cos, sin]
        in_specs += [pl.BlockSpec((tm, cos.shape[1]), rot_map), pl.BlockSpec((tm, sin.shape[1]), rot_map)]
    kspecs, out_specs, out_shape = [], [], []
    for c0, width, scale, c_swap, dts, prev in specs:
        n_prev = None
        if prev is not None:
            n_prev = 0 if prev is () else prev.shape[0]
            if n_prev:
                args.append(prev)
                in_specs.append(pl.BlockSpec((n_prev, tm, width), row3))
        kspecs.append((c0, width, scale, c_swap, dts, n_prev))
        for n, dt in enumerate(dts):
            if n == 0 and n_prev is not None:
                out_specs.append(pl.BlockSpec((n_prev + 1, tm, width), row3))
                out_shape.append(jax.ShapeDtypeStruct((n_prev + 1, t, width), dt))
            else:
                out_specs.append(pl.BlockSpec((tm, width), row))
                out_shape.append(jax.ShapeDtypeStruct((t, width), dt))
    return pl.pallas_call(
        functools.partial(_proj_kernel, specs=tuple(kspecs), has_rot=rot is not None),
        grid=(t // tm,),
        in_specs=in_specs,
        out_specs=out_specs,
        out_shape=out_shape,
        compiler_params=_cparams("parallel"),
        name="proj",
    )(*args)


def _ret_kernel(q_ref, k_ref, v_ref, rg_ref, gain_ref, s0_ref, dm_ref, qd_ref, kd_ref, cdec_ref,
                o_ref, sn_ref, s_scr):
    c = pl.program_id(1)

    @pl.when(c == 0)
    def _():
        s_scr[...] = s0_ref[0]

    q = q_ref[0]
    k = k_ref[0]
    v = v_ref[0]
    qs = (q.astype(F32) * qd_ref[...]).astype(BF16)
    ks = (k.astype(F32) * kd_ref[...]).astype(BF16)
    lane = _lane_iota((1, RET_QK_W))
    s_all = s_scr[...]
    s_bf = s_all.astype(BF16)
    upd = _dot_tn(ks, v)
    zero = jnp.zeros_like(q)
    heads = range(RET_HEADS)
    hms = [(lane >= h * RET_DK) & (lane < (h + 1) * RET_DK) for h in heads]
    sls = [slice(h * RET_DV, (h + 1) * RET_DV) for h in heads]
    atts = [_dot_nt(jnp.where(hm, q, zero), k) * dm_ref[h] for h, hm in zip(heads, hms)]
    cross = [_dot(jnp.where(hm, qs, zero), s_bf) for hm in hms]
    outs = [_dot(att.astype(BF16), v[:, sl]) + cr for att, sl, cr in zip(atts, sls, cross)]
    outs = [o * lax.rsqrt(jnp.mean(o * o, axis=-1, keepdims=True) + EPS) for o in outs]
    o = jnp.concatenate(outs, axis=1)
    o_ref[0] = (o * gain_ref[...] * _silu(rg_ref[0])).astype(o_ref.dtype)
    for h, sl in zip(heads, sls):
        rows = slice(h * RET_DK, (h + 1) * RET_DK)
        s_scr[rows, :] = s_all[rows, :] * cdec_ref[rows, :] + upd[rows, sl]

    @pl.when(c == pl.num_programs(1) - 1)
    def _():
        sn_ref[0] = s_scr[...]


def _retention(q, k, v, rg, gain, s0, tc):
    b, l, _ = q.shape
    assert l % tc == 0
    f32 = np.float32
    lg = np.log1p(-np.exp2(-5.0 - np.arange(RET_HEADS, dtype=f32))).astype(f32)
    idx = np.arange(tc, dtype=f32)
    diff = idx[:, None] - idx[None, :]
    dm = np.where(diff >= 0, np.exp(lg[:, None, None] * np.maximum(diff, 0.0)), 0.0).astype(f32)
    qd = np.repeat(np.exp(lg[None, :] * (idx[:, None] + 1.0)), RET_DK, axis=1).astype(f32)
    kd = np.repeat(np.exp(lg[None, :] * (tc - 1.0 - idx[:, None])), RET_DK, axis=1).astype(f32)
    cdec = np.broadcast_to(np.repeat(np.exp(lg * f32(tc)), RET_DK)[:, None], (RET_QK_W, RET_DV)).astype(f32)
    blk = lambda w: pl.BlockSpec((1, tc, w), lambda i, c: (i, c, 0))
    const2 = lambda i, c: (0, 0)
    st_spec = pl.BlockSpec((1, RET_QK_W, RET_DV), lambda i, c: (i, 0, 0))
    return pl.pallas_call(
        _ret_kernel,
        grid=(b, l // tc),
        in_specs=[blk(RET_QK_W), blk(RET_QK_W), blk(RET_V_W), blk(RET_V_W),
                  pl.BlockSpec((1, RET_V_W), const2), st_spec,
                  pl.BlockSpec((RET_HEADS, tc, tc), lambda i, c: (0, 0, 0)),
                  pl.BlockSpec((tc, RET_QK_W), const2), pl.BlockSpec((tc, RET_QK_W), const2),
                  pl.BlockSpec((RET_QK_W, RET_DV), const2)],
        out_specs=[blk(RET_V_W), st_spec],
        out_shape=[jax.ShapeDtypeStruct((b, l, RET_V_W), BF16),
                   jax.ShapeDtypeStruct((b, RET_QK_W, RET_DV), F32)],
        scratch_shapes=[pltpu.VMEM((RET_QK_W, RET_DV), F32)],
        compiler_params=_cparams("parallel", "arbitrary"),
        name="retention",
    )(q, k, v, rg, gain.reshape(1, RET_V_W), s0, dm, qd, kd, cdec)


def _cum_matrix(n):
    tri = (np.arange(n)[:, None] >= np.arange(n)[None, :]).astype(BF16)
    return np.concatenate([tri, tri], axis=0)


LOG2E = 1.4426950408889634
SB_UNDERFLOW = -160.0


def _sb_block(qms, kbs, vbs, cum_m, carries, mask, transposed=False):
    if transposed:
        zs = [_dot(qm, kb) for qm, kb in zip(qms, kbs)]
    else:
        zs = [_dot_nt(qm, kb) for qm, kb in zip(qms, kbs)]
    nzs = [-z for z in zs]
    lgs = [jnp.minimum(nz, 0.0) - jnp.log2(1.0 + jnp.exp2(jnp.minimum(z, nz))) for z, nz in zip(zs, nzs)]
    if mask is not None:
        lgs = [jnp.where(mask, lg, 0.0) for lg in lgs]
    his = [lg.astype(BF16) for lg in lgs]
    los = [(lg - hi.astype(F32)).astype(BF16) for lg, hi in zip(lgs, his)]
    incls = [_dot(jnp.concatenate([hi, lo], axis=1), cum_m) for hi, lo in zip(his, los)]
    if carries is None:
        ws = [jnp.exp2(z + incl) for z, incl in zip(zs, incls)]
    else:
        ws = [jnp.exp2(z + incl + c) for z, incl, c in zip(zs, incls, carries)]
    if mask is not None:
        ws = [jnp.where(mask, w, 0.0) for w in ws]
    if transposed:
        outs = [_dot_nt(w.astype(BF16), vb) for w, vb in zip(ws, vbs)]
    else:
        outs = [_dot(w.astype(BF16), vb) for w, vb in zip(ws, vbs)]
    return outs, [incl[:, 0:1] for incl in incls]


SB_PAIRS = 2


def _sb_kernel(q_ref, kd_ref, vd_ref, kp_ref, vp_ref, md_ref, mp_ref, o_ref, *, tq, tkd, tk, blocks_per_q,
               n_past, past_t):
    i = pl.program_id(2)
    lane = _lane_iota((1, LANES))
    sls = [slice(p * LANES, (p + 1) * LANES) for p in range(SB_PAIRS)]
    nh = 2 * SB_PAIRS
    qms = []
    for sl in sls:
        q = q_ref[0, :, sl]
        zero = jnp.zeros_like(q)
        qms += [jnp.where(lane < SB_DH, q, zero), jnp.where(lane >= SB_DH, q, zero)]
    row = lax.broadcasted_iota(jnp.int32, (tq, tkd), 0)
    col = lax.broadcasted_iota(jnp.int32, (tq, tkd), 1)
    dmask = col < row
    nblk = i * blocks_per_q if n_past is None else n_past

    per_head = lambda xs: [x for x in xs for _ in range(2)]
    kds = per_head([kd_ref[0, :, sl].astype(BF16) for sl in sls])
    vds = per_head([vd_ref[0, :, sl].astype(BF16) for sl in sls])
    accs, carries = _sb_block(qms, kds, vds, md_ref[...], None, dmask)

    def live(cs):
        return jnp.max(functools.reduce(jnp.maximum, cs)) > SB_UNDERFLOW

    def cond(st):
        return (st[0] >= 0) & st[1]

    def body(st):
        j, accs, cs = st[0], st[2:2 + nh], st[2 + nh:]
        off = pl.multiple_of(j * tk, tk)
        if past_t:
            kbs = per_head([kp_ref[0, sl, pl.ds(off, tk)].astype(BF16) for sl in sls])
            vbs = per_head([vp_ref[0, sl, pl.ds(off, tk)].astype(BF16) for sl in sls])
        else:
            kbs = per_head([kp_ref[0, pl.ds(off, tk), sl].astype(BF16) for sl in sls])
            vbs = per_head([vp_ref[0, pl.ds(off, tk), sl].astype(BF16) for sl in sls])
        ds, ts = _sb_block(qms, kbs, vbs, mp_ref[...], cs, None, transposed=past_t)
        cs = [c + t for c, t in zip(cs, ts)]
        return (j - 1, live(cs), *[a + d for a, d in zip(accs, ds)], *cs)

    st = lax.while_loop(cond, body, (jnp.asarray(nblk - 1, jnp.int32), live(carries), *accs, *carries))
    pairs = [jnp.where(lane < SB_DH, st[2 + 2 * p], st[3 + 2 * p]) for p in range(SB_PAIRS)]
    o_ref[0] = jnp.concatenate(pairs, axis=1).astype(o_ref.dtype)


def _stick_breaking(q, k_own, v_own, own_layer, k_past, v_past, past_layer, prompt):
    b, lq, w = q.shape
    wb = SB_PAIRS * LANES
    if prompt:
        tq = tkd = tk = min(SB_TILE, lq)
        blocks_per_q, n_past = 1, None
        lp = k_past.shape[2]
        pspec = pl.BlockSpec((None, 1, lp, wb), lambda i, p, j: (past_layer, i, 0, p))
    else:
        tq = lq
        tkd = k_own.shape[2]
        tk = SB_TILE
        lp = k_past.shape[3]
        assert lp % tk == 0
        blocks_per_q, n_past = 0, lp // tk
        pspec = pl.BlockSpec((None, 1, wb, lp), lambda i, p, j: (past_layer, i, p, 0))
    assert lq % tq == 0
    qspec = pl.BlockSpec((1, tq, wb), lambda i, p, j: (i, j, p))
    dspec = pl.BlockSpec((None, 1, tkd, wb), lambda i, p, j: (own_layer, i, j, p))
    md = _cum_matrix(tkd)
    mp = _cum_matrix(tk)
    const2 = lambda i, p, j: (0, 0)
    return pl.pallas_call(
        functools.partial(_sb_kernel, tq=tq, tkd=tkd, tk=tk, blocks_per_q=blocks_per_q, n_past=n_past,
                          past_t=not prompt),
        grid=(b, w // wb, lq // tq),
        in_specs=[qspec, dspec, dspec, pspec, pspec, pl.BlockSpec(md.shape, const2), pl.BlockSpec(mp.shape, const2)],
        out_specs=qspec,
        out_shape=jax.ShapeDtypeStruct((b, lq, w), BF16),
        compiler_params=_cparams("parallel", "parallel", "arbitrary"),
        name="stick_breaking",
    )(q, k_own, v_own, k_past, v_past, md, mp)


def _band_kernel(q_ref, k_ref, v_ref, bm_ref, o_ref, *, tq, span, sub):
    lane = _lane_iota((1, LANES))
    npair = BAND_W // LANES
    sls = [slice(pr * LANES, (pr + 1) * LANES) for pr in range(npair)]
    for s in range(sub):
        rows = slice(s * tq, (s + 1) * tq)
        tile = pl.program_id(1) * sub + s
        ws = pl.multiple_of(jnp.maximum(tile * tq - (span - tq), 0), LANES)
        pairs = []
        for g0 in range(0, npair, BAND_GROUP):
            group = range(g0, g0 + BAND_GROUP)
            scores = []
            for pr in group:
                kw = k_ref[0, pl.ds(ws, span), sls[pr]]
                q = q_ref[0, rows, sls[pr]]
                zero = jnp.zeros_like(q)
                q2 = jnp.concatenate([jnp.where(lane < BAND_DH, q, zero), jnp.where(lane >= BAND_DH, q, zero)],
                                     axis=0)
                bias2 = jnp.concatenate([bm_ref[0, s, 2 * pr], bm_ref[0, s, 2 * pr + 1]], axis=0)
                scores.append(_dot_nt(q2, kw) + bias2)
            probs = [jnp.exp(sc - jnp.max(sc, axis=-1, keepdims=True)) for sc in scores]
            for n, pr in enumerate(group):
                vw = v_ref[0, pl.ds(ws, span), sls[pr]]
                p = probs[n]
                out2 = _dot(p.astype(BF16), vw) / jnp.sum(p, axis=-1, keepdims=True)
                pairs.append(jnp.where(lane < BAND_DH, out2[:tq], out2[tq:]))
        o_ref[0, rows, :] = jnp.concatenate(pairs, axis=1).astype(o_ref.dtype)


def _band_bias(rel_bias, q0, k0, tq, span, n_keys):
    d0 = q0 - k0
    n = span + tq
    padw = n + abs(d0)
    tab = rel_bias.astype(F32)
    heads = tab.shape[0]
    ext = jnp.concatenate([jnp.broadcast_to(tab[:, :1], (heads, padw)), tab,
                           jnp.broadcast_to(tab[:, -1:], (heads, padw))], axis=1)
    base = padw + REL_CLIP + d0
    g = jnp.concatenate([jnp.flip(ext[:, base - span + 1:base + 1], axis=1),
                         jnp.flip(ext[:, base + 1:base + tq + 1], axis=1)], axis=1)
    toe = jnp.tile(g, (1, tq))[:, :tq * (n - 1)].reshape(heads, tq, n - 1)[:, :, :span]
    q_pos = q0 + np.arange(tq)[:, None]
    k_pos = k0 + np.arange(span)[None, :]
    qc, kc = q_pos // CHUNK, k_pos // CHUNK
    mask = (np.arange(span)[None, :] < n_keys) & (k_pos >= 0) & (kc <= qc) & (kc >= qc - BAND_CHUNKS)
    return jnp.where(jnp.asarray(mask)[None], toe, NEG_BIG)


def _band(q, k, v, bm, tq, span):
    b, lq, w = q.shape
    nd, heads = bm.shape[:2]
    lk = k.shape[1]
    sub = BAND_SUB if (lq // tq) % BAND_SUB == 0 else 1
    nsteps = -(-(nd - 1) // sub) + 1
    bm = jnp.stack([jnp.stack([bm[min(sub * j + s, nd - 1)] for s in range(sub)]) for j in range(nsteps)])
    qspec = pl.BlockSpec((1, sub * tq, w), lambda i, j: (i, j, 0))
    kspec = pl.BlockSpec((1, lk, w), lambda i, j: (i, 0, 0))
    bspec = pl.BlockSpec((1, sub, heads, tq, span), lambda i, j: (jnp.minimum(j, nsteps - 1), 0, 0, 0, 0))
    return pl.pallas_call(
        functools.partial(_band_kernel, tq=tq, span=span, sub=sub),
        grid=(b, lq // (sub * tq)),
        in_specs=[qspec, kspec, kspec, bspec],
        out_specs=qspec,
        out_shape=jax.ShapeDtypeStruct((b, lq, w), BF16),
        compiler_params=_cparams("parallel", "arbitrary"),
        name="band",
    )(q, k, v, bm)


def _ssd_kernel(xbc_ref, z_ref, dt_ref, *rest, tc, valid_len):
    prm = rest[:6]
    st0_ref, cbuf_ref, tri_ref, e_ref, y_ref, stn_ref, st_scr, xwin = rest[6:]
    c = pl.program_id(1)

    @pl.when(c == 0)
    def _():
        st_scr[...] = st0_ref[0]
        xwin[0:SUBLANES, :] = cbuf_ref[0]

    y_ref[0] = _ssd_chunk(xbc_ref[0], z_ref[0], dt_ref[0], c * tc, valid_len, prm, st_scr, xwin, tri_ref, e_ref)

    @pl.when(c == pl.num_programs(1) - 1)
    def _():
        stn_ref[0] = st_scr[...]


def _ssd_chunk(u, zv, dt_raw, row0, valid_len, prm, st_scr, xwin, tri_ref, e_ref):
    cw_ref, cb_ref, dtb_ref, alog_ref, dskip_ref, gain_ref = prm
    tc = u.shape[0]
    xwin[SUBLANES:SUBLANES + tc, :] = u
    acc = cb_ref[...] + cw_ref[SSD_CONV - 1:SSD_CONV, :] * u
    for i in range(SSD_CONV - 1):
        off = SUBLANES - (SSD_CONV - 1) + i
        acc = acc + cw_ref[i:i + 1, :] * xwin[off:off + tc, :]
    xwin[0:SUBLANES, :] = u[tc - SUBLANES:, :]
    xc = _silu(acc)
    xs = xc[:, :SSD_INNER]

    rowid = row0 + lax.broadcasted_iota(jnp.int32, (tc, LANES), 0)
    dt = jnp.where(rowid < valid_len, _softplus(dt_raw + dtb_ref[...]), 0.0)
    da = dt * (-jnp.exp(alog_ref[...]))
    cum = _split_dot_left(tri_ref[...], da)
    dt_b = _split_dot(dt, e_ref[...])
    cum_b = _split_dot(cum, e_ref[...])
    last_b = cum_b[tc - 1:tc, :]
    xd = xs * dt_b
    xw = (xd * jnp.exp(last_b - cum_b)).astype(BF16)
    xd_bf = xd.astype(BF16)
    cum_t = cum.T

    lane = _lane_iota((1, LANES))
    ri = lax.broadcasted_iota(jnp.int32, (tc, tc), 0)
    ci = lax.broadcasted_iota(jnp.int32, (tc, tc), 1)
    causal = ri >= ci
    heads_per_group = SSD_HEADS // SSD_GROUPS
    gw = heads_per_group * SSD_P
    gls = [slice(g * gw, (g + 1) * gw) for g in range(SSD_GROUPS)]
    bgs = [xc[:, SSD_INNER + g * SSD_N:SSD_INNER + (g + 1) * SSD_N].astype(BF16) for g in range(SSD_GROUPS)]
    cgs = [xc[:, SSD_INNER + (SSD_GROUPS + g) * SSD_N:SSD_INNER + (SSD_GROUPS + g + 1) * SSD_N].astype(BF16)
           for g in range(SSD_GROUPS)]
    cbs = [_dot_nt(cg, bg) for cg, bg in zip(cgs, bgs)]
    y_inter = jnp.concatenate([_dot(cg, st_scr[:, gl].astype(BF16)) for cg, gl in zip(cgs, gls)], axis=1)
    s_upd = [_dot_tn(bg, xw[:, gl]) for bg, gl in zip(bgs, gls)]
    atts = []
    for pidx in range(SSD_HEADS // 2):
        cpair = cum_b[:, pidx * LANES:(pidx + 1) * LANES]
        rolled = pltpu.roll(cpair, SSD_P, axis=1)
        for hh in range(2):
            h = 2 * pidx + hh
            colv = jnp.where((lane >= SSD_P) if hh else (lane < SSD_P), cpair, rolled)
            seg = jnp.exp(jnp.where(causal, colv - cum_t[h:h + 1, :], NEG_BIG))
            atts.append((cbs[h // heads_per_group] * seg).astype(BF16))
    pairs = []
    for pidx in range(SSD_HEADS // 2):
        xdp = xd_bf[:, pidx * LANES:(pidx + 1) * LANES]
        pairs.append(jnp.where(lane < SSD_P, _dot(atts[2 * pidx], xdp), _dot(atts[2 * pidx + 1], xdp)))
    y_intra = jnp.concatenate(pairs, axis=1)
    for gl, upd in zip(gls, s_upd):
        st_scr[:, gl] = st_scr[:, gl] * jnp.exp(last_b[:, gl]) + upd
    y = y_intra + y_inter * jnp.exp(cum_b) + dskip_ref[...] * xs
    return _rms(y * _silu(zv), gain_ref[...]).astype(BF16)


def _split_dot_left(m, x):
    hi = x.astype(BF16)
    lo = (x - hi.astype(F32)).astype(BF16)
    return _dot(m, hi) + _dot(m, lo)


def _ssd_params(conv_w, conv_b, dt_bias, a_log, d_skip, gain):
    pad = LANES - SSD_HEADS
    tri = (np.arange(SSD_TILE)[:, None] >= np.arange(SSD_TILE)[None, :]).astype(BF16)
    expand = (np.arange(LANES)[:, None] == (np.arange(SSD_INNER) // SSD_P)[None, :]).astype(BF16)
    params = [conv_w, conv_b.reshape(1, -1),
              jnp.pad(dt_bias.astype(F32), (0, pad)).reshape(1, LANES),
              jnp.pad(a_log.astype(F32), (0, pad)).reshape(1, LANES),
              jnp.repeat(d_skip.astype(F32), SSD_P).reshape(1, SSD_INNER),
              gain.reshape(1, SSD_INNER)]
    return params, tri, expand


def _ssd(xbc, z, dt, conv_w, conv_b, dt_bias, a_log, d_skip, gain, st0, cbuf, valid_len):
    b, l, _ = xbc.shape
    tc = SSD_TILE
    assert l % tc == 0
    params, tri, expand = _ssd_params(conv_w, conv_b, dt_bias, a_log, d_skip, gain)
    blk = lambda w: pl.BlockSpec((1, tc, w), lambda i, c: (i, c, 0))
    const2 = lambda i, c: (0, 0)
    cs = lambda a: pl.BlockSpec(a.shape, const2)
    st_spec = pl.BlockSpec((1, SSD_N, SSD_INNER), lambda i, c: (i, 0, 0))
    return pl.pallas_call(
        functools.partial(_ssd_kernel, tc=tc, valid_len=valid_len),
        grid=(b, l // tc),
        in_specs=[blk(SSD_CONV_DIM), blk(SSD_INNER), blk(LANES)] + [cs(p) for p in params]
                 + [st_spec, pl.BlockSpec((1, SUBLANES, SSD_CONV_DIM), lambda i, c: (i, 0, 0)), cs(tri), cs(expand)],
        out_specs=[blk(SSD_INNER), st_spec],
        out_shape=[jax.ShapeDtypeStruct((b, l, SSD_INNER), BF16),
                   jax.ShapeDtypeStruct((b, SSD_N, SSD_INNER), F32)],
        scratch_shapes=[pltpu.VMEM((SSD_N, SSD_INNER), F32), pltpu.VMEM((SUBLANES + tc, SSD_CONV_DIM), F32)],
        compiler_params=_cparams("parallel", "arbitrary"),
        name="ssd",
    )(xbc, z, dt, *params, st0, cbuf, tri, expand)


def _rot_tables(pos, rows):
    half = RET_DK // 2
    inv = np.power(np.float32(ROPE_BASE), -np.arange(half, dtype=np.float32) / np.float32(half))
    ang = pos.astype(np.float32)[:, None] * inv[None, :]
    cos = np.tile(np.cos(ang), (rows // pos.shape[0], 2 * RET_HEADS))
    sin = np.sin(ang)
    sin = np.tile(np.concatenate([-sin, sin], axis=1), (rows // pos.shape[0], RET_HEADS))
    return cos.astype(np.float32), sin.astype(np.float32)


def _ab_layer(x, b, l, pos0, g, w_ext, ret_gain, past, sb_prev):
    t = b * l
    rows = max(l, min(PROJ_TILE, t))
    rot = _rot_tables(pos0 + np.arange(l, dtype=np.int32), rows)
    o = [0]
    for wdt in (RET_QK_W, RET_QK_W, RET_V_W, RET_V_W, SB_W, SB_W, SB_W):
        o.append(o[-1] + wdt)
    specs = [(o[0], RET_QK_W, HEAD_SCALE, True, (BF16,), None),
             (o[1], RET_QK_W, 1.0, True, (BF16,), None),
             (o[2], RET_V_W, 1.0, None, (BF16,), None), (o[3], RET_V_W, 1.0, None, (F32,), None),
             (o[4], SB_W, HEAD_SCALE * LOG2E, None, (BF16,), None), (o[5], SB_W, 1.0, None, (F32,), sb_prev[0]),
             (o[6], SB_W, 1.0, None, (F32,), sb_prev[1])]
    rq, rk, rv, rg, sq, sk, sv = _proj(x, g, w_ext, specs, rot)
    n_sb = sk.shape[0]
    sk4, sv4 = sk.reshape(n_sb, b, l, SB_W), sv.reshape(n_sb, b, l, SB_W)
    r3 = lambda a: a.reshape(b, l, a.shape[-1])
    if past is None:
        s0 = jnp.zeros((b, RET_QK_W, RET_DV), F32)
        tc = min(RET_TILE, l)
    else:
        s0 = past[0].astype(F32).reshape(b, RET_QK_W, RET_DV)
        tc = l
    ro, s_new = _retention(r3(rq), r3(rk), r3(rv), r3(rg), ret_gain, s0, tc)
    if past is None:
        so = _stick_breaking(r3(sq), sk4, sv4, n_sb - 1, sk4, sv4, n_sb - 1, prompt=True)
    else:
        (ck_all, layer), (cv_all, _) = past[1], past[2]
        ck = jnp.transpose(ck_all, (0, 1, 3, 4, 2)).reshape(ck_all.shape[0], b, SB_W, -1)
        cv = jnp.transpose(cv_all, (0, 1, 3, 4, 2)).reshape(cv_all.shape[0], b, SB_W, -1)
        padn = (-l) % LANES
        pad_own = lambda a: jnp.pad(a[-1:], ((0, 0), (0, 0), (0, padn), (0, 0)))
        so = _stick_breaking(r3(sq), pad_own(sk4), pad_own(sv4), 0, ck, cv, layer, prompt=False)
    return ro.reshape(t, RET_V_W), so.reshape(t, SB_W), s_new.reshape(b, RET_HEADS, RET_DK, RET_DV), (sk, sv)


def _cd_layer(x, b, l, pos0, g, w_ext, rel_bias, conv_w, conv_b, dt_bias, a_log, d_skip, ssd_gain, past,
              ffn_args):
    t = b * l
    o = [0]
    for wdt in (BAND_W, BAND_W, BAND_W, SSD_INNER, SSD_CONV_DIM, LANES):
        o.append(o[-1] + wdt)
    specs = [(o[0], BAND_W, HEAD_SCALE, None, (BF16,), None), (o[1], BAND_W, 1.0, None, (F32, BF16), None),
             (o[2], BAND_W, 1.0, None, (F32, BF16), None), (o[3], SSD_INNER, 1.0, None, (F32,), None),
             (o[4], SSD_CONV_DIM, 1.0, None, (F32,), None), (o[5], LANES, 1.0, None, (F32,), None)]
    bq, bk, bk16, bv, bv16, z, xbc, dt = _proj(x, g, w_ext, specs)
    r3 = lambda a: a.reshape(b, l, a.shape[-1])
    bk3, bv3, xbc3 = r3(bk), r3(bv), r3(xbc)
    if past is None:
        tq = min(BAND_TILE, l)
        span = BAND_WINDOW + tq
        nd = BAND_WINDOW // tq + 1
        bm = jnp.stack([_band_bias(rel_bias, d * tq, max(d * tq - BAND_WINDOW, 0), tq, span, span)
                        for d in range(nd)])
        bo = _band(r3(bq), r3(bk16), r3(bv16), bm, tq, span)
        keep = min(BAND_WINDOW, l)
        new_k, new_v = bk3[:, l - keep:], bv3[:, l - keep:]
        st0 = jnp.zeros((b, SSD_N, SSD_INNER), F32)
        cbuf = jnp.zeros((b, SUBLANES, SSD_CONV_DIM), F32)
        lp = l
    else:
        ck = past[0].reshape(b, -1, BAND_W)
        cv = past[1].reshape(b, -1, BAND_W)
        wlen = ck.shape[1]
        span = -(-(wlen + l) // LANES) * LANES
        padk = span - wlen - l
        k_all = jnp.concatenate([ck.astype(BF16), r3(bk16), jnp.zeros((b, padk, BAND_W), BF16)], axis=1)
        v_all = jnp.concatenate([cv.astype(BF16), r3(bv16), jnp.zeros((b, padk, BAND_W), BF16)], axis=1)
        bm = _band_bias(rel_bias, pos0, pos0 - wlen, l, span, wlen + l)[None]
        bo = _band(r3(bq), k_all, v_all, bm, l, span)
        new_k, new_v = bk3, bv3
        st0 = jnp.transpose(past[2].astype(F32), (0, 3, 1, 2)).reshape(b, SSD_N, SSD_INNER)
        cbuf = jnp.pad(past[3].astype(F32), ((0, 0), (SUBLANES - (SSD_CONV - 1), 0), (0, 0)))
        lp = -(-l // SSD_TILE) * SSD_TILE
    assert l >= SSD_CONV - 1
    conv_new = xbc3[:, l - (SSD_CONV - 1):]
    g2, win2, wout2, wo, fin_g = ffn_args
    wa, wb = wo[:BAND_W], wo[BAND_W:]
    ma = bo.reshape(t, BAND_W)
    if past is None and l % FUSED_TILE == 0:
        prm = _ssd_params(conv_w, conv_b, dt_bias, a_log, d_skip, ssd_gain)
        x, st_new = _ffn_ssd(x, g2, win2, wout2, ma, wa, wb, fin_g, xbc, z, dt, prm, st0, cbuf, l)
    else:
        padl = lp - l
        pad3 = lambda a: jnp.pad(a, ((0, 0), (0, padl), (0, 0))) if padl else a
        y, st_new = _ssd(pad3(xbc3), pad3(r3(z)), pad3(r3(dt)), conv_w, conv_b, dt_bias, a_log, d_skip, ssd_gain,
                         st0, cbuf, l)
        x = _ffn(x, g2, win2, wout2, mix=(ma, y[:, :l].reshape(t, SSD_INNER), wa, wb), final_g=fin_g)
    ssm_new = jnp.transpose(st_new.reshape(b, SSD_N, SSD_HEADS, SSD_P), (0, 2, 3, 1))
    outs = (new_k.reshape(b, -1, BAND_HEADS, BAND_DH), new_v.reshape(b, -1, BAND_HEADS, BAND_DH), ssm_new, conv_new)
    return x, outs


def _trunk(x3, pos0, wts, past):
    (norm_g, win, wout, final_g, w_ab, ret_gain, wo_ab, w_cd, rel_bias, conv_w, conv_b, dt_bias, a_log,
     d_skip, ssd_gain, wo_cd) = wts
    b, l, d = x3.shape
    depth = norm_g.shape[0]
    x = x3.reshape(b * l, d)
    outs = ([], None, None, [], [], [], [])
    sb_stack = ((), ())
    x = _ffn(x, norm_g[0, 0], win[0, 0], wout[0, 0])
    for layer in range(depth):
        j = layer // 2
        fin_g = final_g if layer + 1 == depth else None
        if layer % 2 == 0:
            pst = None if past is None else (past[0][j], (past[1], j), (past[2], j))
            ma, mb, r, sb_stack = _ab_layer(x, b, l, pos0, norm_g[layer, 1], w_ab[j], ret_gain[j], pst, sb_stack)
            outs[0].append(r)
            half = ma.shape[1]
            mix = (ma, mb, wo_ab[j][:half], wo_ab[j][half:])
            x = _ffn(x, norm_g[layer, 2], win[layer, 1], wout[layer, 1], mix=mix, final_g=fin_g)
        else:
            pst = None if past is None else (past[3][j], past[4][j], past[5][j], past[6][j])
            ffn_args = (norm_g[layer, 2], win[layer, 1], wout[layer, 1], wo_cd[j], fin_g)
            x, (k, v, s, cb) = _cd_layer(x, b, l, pos0, norm_g[layer, 1], w_cd[j], rel_bias[j], conv_w[j],
                                         conv_b[j], dt_bias[j], a_log[j], d_skip[j], ssd_gain[j], pst, ffn_args)
            outs[3].append(k)
            outs[4].append(v)
            outs[5].append(s)
            outs[6].append(cb)
        if layer + 1 < depth:
            x = _ffn(x, norm_g[layer + 1, 0], win[layer + 1, 0], wout[layer + 1, 0])
    sb_k, sb_v = (jnp.transpose(jnp.swapaxes(a.reshape(a.shape[0], b, l, SB_W), 2, 3)
                                .reshape(a.shape[0], b, SB_HEADS, SB_DH, l), (0, 1, 4, 2, 3)) for a in sb_stack)
    stacked = [jnp.stack(o) for o in outs if o is not None]
    return x.reshape(b, l, d), (stacked[0], sb_k, sb_v, *stacked[1:])


def kernel(x_prompt, x_sample, state_ret, cache_sb_k, cache_sb_v, cache_band_k, cache_band_v, state_ssm, state_conv, norm_g, ffn_w_in, ffn_w_out, final_g, w_in_ab, ret_gain, w_out_ab, w_in_cd, rel_bias, conv_w, conv_b, dt_bias, a_log, d_skip, ssd_gain, w_out_cd):
    win = ffn_w_in.astype(BF16)
    wout = ffn_w_out.astype(BF16)
    w_ab = w_in_ab.astype(BF16)
    n_main = 3 * BAND_W + SSD_INNER + SSD_CONV_DIM
    w_cd = jnp.concatenate([w_in_cd, jnp.zeros(w_in_cd.shape[:2] + (LANES - SSD_HEADS,), w_in_cd.dtype)],
                           axis=-1).astype(BF16)
    assert w_cd.shape[-1] == n_main + LANES
    wts = (norm_g, win, wout, final_g, w_ab, ret_gain, w_out_ab.astype(BF16), w_cd, rel_bias, conv_w, conv_b,
           dt_bias, a_log, d_skip, ssd_gain, w_out_cd.astype(BF16))
    y_p, outs_p = _trunk(x_prompt, 0, wts, None)
    past = (state_ret, cache_sb_k, cache_sb_v, cache_band_k, cache_band_v, state_ssm, state_conv)
    y_s, outs_s = _trunk(x_sample, cache_sb_k.shape[2], wts, past)
    return (y_p, y_s) + outs_p + outs_s
```

```python
import functools

import jax
import jax.numpy as jnp
import numpy as np
from jax import lax
from jax.experimental import pallas as pl
from jax.experimental.pallas import tpu as pltpu

F32 = jnp.float32
BF16 = jnp.bfloat16

CHUNK = 64
EPS = 1e-6
ROPE_BASE = 10000.0
RET_HEADS, RET_DK, RET_DV = 4, 64, 128
SB_HEADS, SB_DH = 8, 64
BAND_HEADS, BAND_DH, BAND_CHUNKS, REL_CLIP = 8, 64, 8, 128
BAND_WINDOW = BAND_CHUNKS * CHUNK
SSD_HEADS, SSD_P, SSD_GROUPS, SSD_N, SSD_CONV = 8, 64, 2, 128, 4
SSD_INNER = SSD_HEADS * SSD_P
SSD_CONV_DIM = SSD_INNER + 2 * SSD_GROUPS * SSD_N
RET_QK_W = RET_HEADS * RET_DK
RET_V_W = RET_HEADS * RET_DV
SB_W = SB_HEADS * SB_DH
BAND_W = BAND_HEADS * BAND_DH
HEAD_SCALE = 0.125

LANES = 128
SUBLANES = 8
VMEM_LIMIT_BYTES = 56 * 1024 * 1024

NEG_BIG = -1e30

ROW_TILE = 1024
PROJ_TILE = 512
FUSED_TILE = 512
FFN_CHUNK = 512
RET_TILE = 256
SB_TILE = 256
BAND_TILE = 128
BAND_GROUP = 2
BAND_SUB = 4
SSD_TILE = 128


def _cparams(*sem):
    return pltpu.CompilerParams(dimension_semantics=sem, vmem_limit_bytes=VMEM_LIMIT_BYTES)


def _rms(x, g):
    return x * lax.rsqrt(jnp.mean(x * x, axis=-1, keepdims=True) + EPS) * g


def _silu(x):
    return x * jax.nn.sigmoid(x)


def _softplus(x):
    return jnp.maximum(x, 0.0) + jnp.log(1.0 + jnp.exp(-jnp.abs(x)))


def _dot(a, b):
    return jnp.dot(a, b, preferred_element_type=F32)


def _dot_nt(a, b):
    return lax.dot_general(a, b, (((1,), (1,)), ((), ())), preferred_element_type=F32)


def _dot_tn(a, b):
    return lax.dot_general(a, b, (((0,), (0,)), ((), ())), preferred_element_type=F32)


def _split_dot(x, m):
    hi = x.astype(BF16)
    lo = (x - hi.astype(F32)).astype(BF16)
    return _dot(hi, m) + _dot(lo, m)


def _lane_iota(shape):
    return lax.broadcasted_iota(jnp.int32, shape, len(shape) - 1)


def _ffn_kernel(*refs, f, has_mix, final):
    refs = list(refs)
    x_ref = refs.pop(0)
    if has_mix:
        ma_ref, mb_ref, wa_ref, wb_ref = refs[:4]
        refs = refs[4:]
    g_ref, win_ref, wout_ref = refs[:3]
    refs = refs[3:]
    if final:
        fg_ref = refs.pop(0)
    o_ref, x1_ref, a_ref = refs

    x = x_ref[...]
    if has_mix:
        x = x + _dot(ma_ref[...], wa_ref[...]) + _dot(mb_ref[...], wb_ref[...])
    o_ref[...] = _ffn_body(x, g_ref, win_ref, wout_ref, fg_ref if final else None, x1_ref, a_ref, f)


def _ffn_body(x, g_ref, win_ref, wout_ref, fg_ref, x1_ref, a_ref, f, side_jobs=()):
    x1_ref[...] = x
    h = _rms(x, g_ref[...]).astype(BF16)
    chunks = list(range(0, f, FFN_CHUNK))
    slots = {}
    for k, job in enumerate(side_jobs):
        slots.setdefault(k * (len(chunks) + 1) // len(side_jobs), []).append(job)
    for n, f0 in enumerate(chunks):
        for job in slots.get(n, ()):
            job()
        f1 = min(f0 + FFN_CHUNK, f)
        gate = _dot(h, win_ref[:, f0:f1])
        up = _dot(h, win_ref[:, f + f0:f + f1])
        a_ref[:, f0:f1] = (_silu(gate) * up).astype(BF16)
    for job in slots.get(len(chunks), ()):
        job()
    y = x1_ref[...] + 0.5 * _dot(a_ref[...], wout_ref[...])
    if fg_ref is not None:
        y = _rms(y, fg_ref[...])
    return y


def _ffn_ssd_kernel(*refs, f, final, tc, tiles_per_batch, seq_len):
    refs = list(refs)
    x_ref, ma_ref, wa_ref, wb_ref, g_ref, win_ref, wout_ref = refs[:7]
    refs = refs[7:]
    fg_ref = refs.pop(0) if final else None
    xbc_ref, z_ref, dt_ref = refs[:3]
    prm = refs[3:9]
    st0_ref, cbuf_ref, tri_ref, e_ref, o_ref, stn_ref, x1_ref, a_ref, st_scr, xwin, ybuf = refs[9:]
    i = pl.program_id(0)
    n = pl.num_programs(0) - 1
    tile = i % tiles_per_batch
    tm = x_ref.shape[0]

    @pl.when(i == 0)
    def _():
        ybuf[...] = jnp.zeros_like(ybuf)

    @pl.when((tile == 0) & (i < n))
    def _():
        st_scr[...] = st0_ref[0]
        xwin[0:SUBLANES, :] = cbuf_ref[0]

    def ssd_job(c):
        def job():
            rows = slice(c * tc, (c + 1) * tc)
            ybuf[rows, :] = _ssd_chunk(xbc_ref[rows, :], z_ref[rows, :], dt_ref[rows, :], tile * tm + c * tc,
                                       seq_len, prm, st_scr, xwin, tri_ref, e_ref)
        return job

    x = x_ref[...] + _dot(ma_ref[...], wa_ref[...]) + _dot(ybuf[...], wb_ref[...])
    o_ref[...] = _ffn_body(x, g_ref, win_ref, wout_ref, fg_ref, x1_ref, a_ref, f,
                           side_jobs=[ssd_job(c) for c in range(tm // tc)])

    @pl.when((tile == tiles_per_batch - 1) & (i < n))
    def _():
        stn_ref[0] = st_scr[...]


def _resident(shape):
    return pl.BlockSpec(shape, lambda *_: (0,) * len(shape), pipeline_mode=pl.Buffered(1))


def _ffn(x, g, win, wout, mix=None, final_g=None):
    t, d = x.shape
    f = wout.shape[0]
    tm = min(ROW_TILE, t)
    assert t % tm == 0 and f % LANES == 0
    row = lambda i: (i, 0)
    args = [x]
    specs = [pl.BlockSpec((tm, d), row)]
    if mix is not None:
        ma, mb, wa, wb = mix
        args += [ma, mb, wa, wb]
        specs += [pl.BlockSpec((tm, ma.shape[1]), row), pl.BlockSpec((tm, mb.shape[1]), row),
                  _resident(wa.shape), _resident(wb.shape)]
    args += [g.reshape(1, d), win, wout]
    specs += [_resident((1, d)), _resident(win.shape), _resident(wout.shape)]
    if final_g is not None:
        args.append(final_g.reshape(1, d))
        specs.append(_resident((1, d)))
    return pl.pallas_call(
        functools.partial(_ffn_kernel, f=f, has_mix=mix is not None, final=final_g is not None),
        grid=(t // tm,),
        in_specs=specs,
        out_specs=pl.BlockSpec((tm, d), row),
        out_shape=jax.ShapeDtypeStruct((t, d), F32),
        scratch_shapes=[pltpu.VMEM((tm, d), F32), pltpu.VMEM((tm, f), BF16)],
        compiler_params=_cparams("parallel"),
        name="ffn",
    )(*args)


def _ffn_ssd(x, g, win, wout, ma, wa, wb, final_g, xbc, z, dt, ssd_prm, st0, cbuf, seq_len):
    t, d = x.shape
    f = wout.shape[0]
    tm = FUSED_TILE
    assert t % tm == 0 and seq_len % tm == 0 and tm % SSD_TILE == 0
    n = t // tm
    tpb = seq_len // tm
    params, tri, expand = ssd_prm
    lag = lambda i: (jnp.maximum(i - 1, 0), 0)
    cur = lambda i: (jnp.minimum(i, n - 1), 0)
    bat = lambda i: (jnp.minimum(i, n - 1) // tpb, 0, 0)
    args = [x, ma, wa, wb, g.reshape(1, d), win, wout]
    specs = [pl.BlockSpec((tm, d), lag), pl.BlockSpec((tm, ma.shape[1]), lag), _resident(wa.shape),
             _resident(wb.shape), _resident((1, d)), _resident(win.shape), _resident(wout.shape)]
    if final_g is not None:
        args.append(final_g.reshape(1, d))
        specs.append(_resident((1, d)))
    args += [xbc, z, dt, *params, st0, cbuf, tri, expand]
    specs += [pl.BlockSpec((tm, xbc.shape[1]), cur), pl.BlockSpec((tm, z.shape[1]), cur),
              pl.BlockSpec((tm, dt.shape[1]), cur)] + [_resident(p.shape) for p in params]
    specs += [pl.BlockSpec((1,) + st0.shape[1:], bat), pl.BlockSpec((1,) + cbuf.shape[1:], bat),
              _resident(tri.shape), _resident(expand.shape)]
    return pl.pallas_call(
        functools.partial(_ffn_ssd_kernel, f=f, final=final_g is not None, tc=SSD_TILE, tiles_per_batch=tpb,
                          seq_len=seq_len),
        grid=(n + 1,),
        in_specs=specs,
        out_specs=[pl.BlockSpec((tm, d), lag), pl.BlockSpec((1,) + st0.shape[1:], bat)],
        out_shape=[jax.ShapeDtypeStruct((t, d), F32), jax.ShapeDtypeStruct(st0.shape, F32)],
        scratch_shapes=[pltpu.VMEM((tm, d), F32), pltpu.VMEM((tm, f), BF16), pltpu.VMEM(st0.shape[1:], F32),
                        pltpu.VMEM((SUBLANES + SSD_TILE, xbc.shape[1]), F32), pltpu.VMEM((tm, SSD_INNER), BF16)],
        compiler_params=_cparams("arbitrary"),
        name="ffn_ssd",
    )(*args)


def _proj_kernel(*refs, specs, has_rot):
    refs = list(refs)
    x_ref, g_ref, w_ref = refs[:3]
    refs = refs[3:]
    if has_rot:
        cos_ref, sin_ref = refs[:2]
        refs = refs[2:]
    prev_refs = [refs.pop(0) for s in specs if s[5]]
    h = _rms(x_ref[...], g_ref[...]).astype(BF16)
    for c0, width, scale, c_swap, dts, n_prev in specs:
        y = _dot(h, w_ref[:, c0:c0 + width])
        if c_swap:
            half = RET_DK // 2
            lane = _lane_iota((1, width))
            partner = jnp.where(lane % RET_DK < half, pltpu.roll(y, width - half, axis=1),
                                pltpu.roll(y, half, axis=1))
            y = y * cos_ref[...] + partner * sin_ref[...]
        if scale != 1.0:
            y = y * scale
        for n, _ in enumerate(dts):
            o_ref = refs.pop(0)
            if n == 0 and n_prev is not None:
                if n_prev:
                    o_ref[0:n_prev] = prev_refs.pop(0)[...]
                o_ref[n_prev] = y.astype(o_ref.dtype)
            else:
                o_ref[...] = y.astype(o_ref.dtype)


def _proj(x, g, w, specs, rot=None):
    t, d = x.shape
    tm = min(PROJ_TILE, t)
    assert t % tm == 0
    row = lambda i: (i, 0)
    row3 = lambda i: (0, i, 0)
    const2 = lambda i: (0, 0)
    args = [x, g.reshape(1, d), w]
    in_specs = [pl.BlockSpec((tm, d), row), pl.BlockSpec((1, d), const2), _resident(w.shape)]
    if rot is not None:
        cos, sin = rot
        nrot = cos.shape[0] // tm
        rot_map = lambda i: (i % nrot, 0)
        args += [cos, sin]
        in_specs += [pl.BlockSpec((tm, cos.shape[1]), rot_map), pl.BlockSpec((tm, sin.shape[1]), rot_map)]
    kspecs, out_specs, out_shape = [], [], []
    for c0, width, scale, c_swap, dts, prev in specs:
        n_prev = None
        if prev is not None:
            n_prev = 0 if prev is () else prev.shape[0]
            if n_prev:
                args.append(prev)
                in_specs.append(pl.BlockSpec((n_prev, tm, width), row3))
        kspecs.append((c0, width, scale, c_swap, dts, n_prev))
        for n, dt in enumerate(dts):
            if n == 0 and n_prev is not None:
                out_specs.append(pl.BlockSpec((n_prev + 1, tm, width), row3))
                out_shape.append(jax.ShapeDtypeStruct((n_prev + 1, t, width), dt))
            else:
                out_specs.append(pl.BlockSpec((tm, width), row))
                out_shape.append(jax.ShapeDtypeStruct((t, width), dt))
    return pl.pallas_call(
        functools.partial(_proj_kernel, specs=tuple(kspecs), has_rot=rot is not None),
        grid=(t // tm,),
        in_specs=in_specs,
        out_specs=out_specs,
        out_shape=out_shape,
        compiler_params=_cparams("parallel"),
        name="proj",
    )(*args)


def _ret_kernel(q_ref, k_ref, v_ref, rg_ref, gain_ref, s0_ref, dm_ref, qd_ref, kd_ref, cdec_ref,
                o_ref, sn_ref, s_scr):
    c = pl.program_id(1)

    @pl.when(c == 0)
    def _():
        s_scr[...] = s0_ref[0]

    q = q_ref[0]
    k = k_ref[0]
    v = v_ref[0]
    qs = (q.astype(F32) * qd_ref[...]).astype(BF16)
    ks = (k.astype(F32) * kd_ref[...]).astype(BF16)
    lane = _lane_iota((1, RET_QK_W))
    s_all = s_scr[...]
    s_bf = s_all.astype(BF16)
    upd = _dot_tn(ks, v)
    zero = jnp.zeros_like(q)
    heads = range(RET_HEADS)
    hms = [(lane >= h * RET_DK) & (lane < (h + 1) * RET_DK) for h in heads]
    sls = [slice(h * RET_DV, (h + 1) * RET_DV) for h in heads]
    atts = [_dot_nt(jnp.where(hm, q, zero), k) * dm_ref[h] for h, hm in zip(heads, hms)]
    cross = [_dot(jnp.where(hm, qs, zero), s_bf) for hm in hms]
    outs = [_dot(att.astype(BF16), v[:, sl]) + cr for att, sl, cr in zip(atts, sls, cross)]
    outs = [o * lax.rsqrt(jnp.mean(o * o, axis=-1, keepdims=True) + EPS) for o in outs]
    o = jnp.concatenate(outs, axis=1)
    o_ref[0] = (o * gain_ref[...] * _silu(rg_ref[0])).astype(o_ref.dtype)
    for h, sl in zip(heads, sls):
        rows = slice(h * RET_DK, (h + 1) * RET_DK)
        s_scr[rows, :] = s_all[rows, :] * cdec_ref[rows, :] + upd[rows, sl]

    @pl.when(c == pl.num_programs(1) - 1)
    def _():
        sn_ref[0] = s_scr[...]


def _retention(q, k, v, rg, gain, s0, tc):
    b, l, _ = q.shape
    assert l % tc == 0
    f32 = np.float32
    lg = np.log1p(-np.exp2(-5.0 - np.arange(RET_HEADS, dtype=f32))).astype(f32)
    idx = np.arange(tc, dtype=f32)
    diff = idx[:, None] - idx[None, :]
    dm = np.where(diff >= 0, np.exp(lg[:, None, None] * np.maximum(diff, 0.0)), 0.0).astype(f32)
    qd = np.repeat(np.exp(lg[None, :] * (idx[:, None] + 1.0)), RET_DK, axis=1).astype(f32)
    kd = np.repeat(np.exp(lg[None, :] * (tc - 1.0 - idx[:, None])), RET_DK, axis=1).astype(f32)
    cdec = np.broadcast_to(np.repeat(np.exp(lg * f32(tc)), RET_DK)[:, None], (RET_QK_W, RET_DV)).astype(f32)
    blk = lambda w: pl.BlockSpec((1, tc, w), lambda i, c: (i, c, 0))
    const2 = lambda i, c: (0, 0)
    st_spec = pl.BlockSpec((1, RET_QK_W, RET_DV), lambda i, c: (i, 0, 0))
    return pl.pallas_call(
        _ret_kernel,
        grid=(b, l // tc),
        in_specs=[blk(RET_QK_W), blk(RET_QK_W), blk(RET_V_W), blk(RET_V_W),
                  pl.BlockSpec((1, RET_V_W), const2), st_spec,
                  pl.BlockSpec((RET_HEADS, tc, tc), lambda i, c: (0, 0, 0)),
                  pl.BlockSpec((tc, RET_QK_W), const2), pl.BlockSpec((tc, RET_QK_W), const2),
                  pl.BlockSpec((RET_QK_W, RET_DV), const2)],
        out_specs=[blk(RET_V_W), st_spec],
        out_shape=[jax.ShapeDtypeStruct((b, l, RET_V_W), BF16),
                   jax.ShapeDtypeStruct((b, RET_QK_W, RET_DV), F32)],
        scratch_shapes=[pltpu.VMEM((RET_QK_W, RET_DV), F32)],
        compiler_params=_cparams("parallel", "arbitrary"),
        name="retention",
    )(q, k, v, rg, gain.reshape(1, RET_V_W), s0, dm, qd, kd, cdec)


def _cum_matrix(n):
    tri = (np.arange(n)[:, None] >= np.arange(n)[None, :]).astype(BF16)
    return np.concatenate([tri, tri], axis=0)


LOG2E = 1.4426950408889634
SB_UNDERFLOW = -160.0


def _sb_block(qms, kbs, vbs, cum_m, carries, mask, transposed=False):
    if transposed:
        zs = [_dot(qm, kb) for qm, kb in zip(qms, kbs)]
    else:
        zs = [_dot_nt(qm, kb) for qm, kb in zip(qms, kbs)]
    nzs = [-z for z in zs]
    lgs = [jnp.minimum(nz, 0.0) - jnp.log2(1.0 + jnp.exp2(jnp.minimum(z, nz))) for z, nz in zip(zs, nzs)]
    if mask is not None:
        lgs = [jnp.where(mask, lg, 0.0) for lg in lgs]
    his = [lg.astype(BF16) for lg in lgs]
    los = [(lg - hi.astype(F32)).astype(BF16) for lg, hi in zip(lgs, his)]
    incls = [_dot(jnp.concatenate([hi, lo], axis=1), cum_m) for hi, lo in zip(his, los)]
    if carries is None:
        ws = [jnp.exp2(z + incl) for z, incl in zip(zs, incls)]
    else:
        ws = [jnp.exp2(z + incl + c) for z, incl, c in zip(zs, incls, carries)]
    if mask is not None:
        ws = [jnp.where(mask, w, 0.0) for w in ws]
    if transposed:
        outs = [_dot_nt(w.astype(BF16), vb) for w, vb in zip(ws, vbs)]
    else:
        outs = [_dot(w.astype(BF16), vb) for w, vb in zip(ws, vbs)]
    return outs, [incl[:, 0:1] for incl in incls]


SB_PAIRS = 2


def _sb_kernel(q_ref, kd_ref, vd_ref, kp_ref, vp_ref, md_ref, mp_ref, o_ref, *, tq, tkd, tk, blocks_per_q,
               n_past, past_t):
    i = pl.program_id(2)
    lane = _lane_iota((1, LANES))
    sls = [slice(p * LANES, (p + 1) * LANES) for p in range(SB_PAIRS)]
    nh = 2 * SB_PAIRS
    qms = []
    for sl in sls:
        q = q_ref[0, :, sl]
        zero = jnp.zeros_like(q)
        qms += [jnp.where(lane < SB_DH, q, zero), jnp.where(lane >= SB_DH, q, zero)]
    row = lax.broadcasted_iota(jnp.int32, (tq, tkd), 0)
    col = lax.broadcasted_iota(jnp.int32, (tq, tkd), 1)
    dmask = col < row
    nblk = i * blocks_per_q if n_past is None else n_past

    per_head = lambda xs: [x for x in xs for _ in range(2)]
    kds = per_head([kd_ref[0, :, sl].astype(BF16) for sl in sls])
    vds = per_head([vd_ref[0, :, sl].astype(BF16) for sl in sls])
    accs, carries = _sb_block(qms, kds, vds, md_ref[...], None, dmask)

    def live(cs):
        return jnp.max(functools.reduce(jnp.maximum, cs)) > SB_UNDERFLOW

    def cond(st):
        return (st[0] >= 0) & st[1]

    def body(st):
        j, accs, cs = st[0], st[2:2 + nh], st[2 + nh:]
        off = pl.multiple_of(j * tk, tk)
        if past_t:
            kbs = per_head([kp_ref[0, sl, pl.ds(off, tk)].astype(BF16) for sl in sls])
            vbs = per_head([vp_ref[0, sl, pl.ds(off, tk)].astype(BF16) for sl in sls])
        else:
            kbs = per_head([kp_ref[0, pl.ds(off, tk), sl].astype(BF16) for sl in sls])
            vbs = per_head([vp_ref[0, pl.ds(off, tk), sl].astype(BF16) for sl in sls])
        ds, ts = _sb_block(qms, kbs, vbs, mp_ref[...], cs, None, transposed=past_t)
        cs = [c + t for c, t in zip(cs, ts)]
        return (j - 1, live(cs), *[a + d for a, d in zip(accs, ds)], *cs)

    st = lax.while_loop(cond, body, (jnp.asarray(nblk - 1, jnp.int32), live(carries), *accs, *carries))
    pairs = [jnp.where(lane < SB_DH, st[2 + 2 * p], st[3 + 2 * p]) for p in range(SB_PAIRS)]
    o_ref[0] = jnp.concatenate(pairs, axis=1).astype(o_ref.dtype)


def _stick_breaking(q, k_own, v_own, own_layer, k_past, v_past, past_layer, prompt):
    b, lq, w = q.shape
    wb = SB_PAIRS * LANES
    if prompt:
        tq = tkd = tk = min(SB_TILE, lq)
        blocks_per_q, n_past = 1, None
        lp = k_past.shape[2]
        pspec = pl.BlockSpec((None, 1, lp, wb), lambda i, p, j: (past_layer, i, 0, p))
    else:
        tq = lq
        tkd = k_own.shape[2]
        tk = SB_TILE
        lp = k_past.shape[3]
        assert lp % tk == 0
        blocks_per_q, n_past = 0, lp // tk
        pspec = pl.BlockSpec((None, 1, wb, lp), lambda i, p, j: (past_layer, i, p, 0))
    assert lq % tq == 0
    qspec = pl.BlockSpec((1, tq, wb), lambda i, p, j: (i, j, p))
    dspec = pl.BlockSpec((None, 1, tkd, wb), lambda i, p, j: (own_layer, i, j, p))
    md = _cum_matrix(tkd)
    mp = _cum_matrix(tk)
    const2 = lambda i, p, j: (0, 0)
    return pl.pallas_call(
        functools.partial(_sb_kernel, tq=tq, tkd=tkd, tk=tk, blocks_per_q=blocks_per_q, n_past=n_past,
                          past_t=not prompt),
        grid=(b, w // wb, lq // tq),
        in_specs=[qspec, dspec, dspec, pspec, pspec, pl.BlockSpec(md.shape, const2), pl.BlockSpec(mp.shape, const2)],
        out_specs=qspec,
        out_shape=jax.ShapeDtypeStruct((b, lq, w), BF16),
        compiler_params=_cparams("parallel", "parallel", "arbitrary"),
        name="stick_breaking",
    )(q, k_own, v_own, k_past, v_past, md, mp)


def _band_kernel(q_ref, k_ref, v_ref, bm_ref, o_ref, *, tq, span, sub):
    lane = _lane_iota((1, LANES))
    npair = BAND_W // LANES
    sls = [slice(pr * LANES, (pr + 1) * LANES) for pr in range(npair)]
    for s in range(sub):
        rows = slice(s * tq, (s + 1) * tq)
        tile = pl.program_id(1) * sub + s
        ws = pl.multiple_of(jnp.maximum(tile * tq - (span - tq), 0), LANES)
        pairs = []
        for g0 in range(0, npair, BAND_GROUP):
            group = range(g0, g0 + BAND_GROUP)
            scores = []
            for pr in group:
                kw = k_ref[0, pl.ds(ws, span), sls[pr]]
                q = q_ref[0, rows, sls[pr]]
                zero = jnp.zeros_like(q)
                q2 = jnp.concatenate([jnp.where(lane < BAND_DH, q, zero), jnp.where(lane >= BAND_DH, q, zero)],
                                     axis=0)
                bias2 = jnp.concatenate([bm_ref[0, s, 2 * pr], bm_ref[0, s, 2 * pr + 1]], axis=0)
                scores.append(_dot_nt(q2, kw) + bias2)
            probs = [jnp.exp(sc - jnp.max(sc, axis=-1, keepdims=True)) for sc in scores]
            for n, pr in enumerate(group):
                vw = v_ref[0, pl.ds(ws, span), sls[pr]]
                p = probs[n]
                out2 = _dot(p.astype(BF16), vw) / jnp.sum(p, axis=-1, keepdims=True)
                pairs.append(jnp.where(lane < BAND_DH, out2[:tq], out2[tq:]))
        o_ref[0, rows, :] = jnp.concatenate(pairs, axis=1).astype(o_ref.dtype)


def _band_bias(rel_bias, q0, k0, tq, span, n_keys):
    d0 = q0 - k0
    n = span + tq
    padw = n + abs(d0)
    tab = rel_bias.astype(F32)
    heads = tab.shape[0]
    ext = jnp.concatenate([jnp.broadcast_to(tab[:, :1], (heads, padw)), tab,
                           jnp.broadcast_to(tab[:, -1:], (heads, padw))], axis=1)
    base = padw + REL_CLIP + d0
    g = jnp.concatenate([jnp.flip(ext[:, base - span + 1:base + 1], axis=1),
                         jnp.flip(ext[:, base + 1:base + tq + 1], axis=1)], axis=1)
    toe = jnp.tile(g, (1, tq))[:, :tq * (n - 1)].reshape(heads, tq, n - 1)[:, :, :span]
    q_pos = q0 + np.arange(tq)[:, None]
    k_pos = k0 + np.arange(span)[None, :]
    qc, kc = q_pos // CHUNK, k_pos // CHUNK
    mask = (np.arange(span)[None, :] < n_keys) & (k_pos >= 0) & (kc <= qc) & (kc >= qc - BAND_CHUNKS)
    return jnp.where(jnp.asarray(mask)[None], toe, NEG_BIG)


def _band(q, k, v, bm, tq, span):
    b, lq, w = q.shape
    nd, heads = bm.shape[:2]
    lk = k.shape[1]
    sub = BAND_SUB if (lq // tq) % BAND_SUB == 0 else 1
    nsteps = -(-(nd - 1) // sub) + 1
    bm = jnp.stack([jnp.stack([bm[min(sub * j + s, nd - 1)] for s in range(sub)]) for j in range(nsteps)])
    qspec = pl.BlockSpec((1, sub * tq, w), lambda i, j: (i, j, 0))
    kspec = pl.BlockSpec((1, lk, w), lambda i, j: (i, 0, 0))
    bspec = pl.BlockSpec((1, sub, heads, tq, span), lambda i, j: (jnp.minimum(j, nsteps - 1), 0, 0, 0, 0))
    return pl.pallas_call(
        functools.partial(_band_kernel, tq=tq, span=span, sub=sub),
        grid=(b, lq // (sub * tq)),
        in_specs=[qspec, kspec, kspec, bspec],
        out_specs=qspec,
        out_shape=jax.ShapeDtypeStruct((b, lq, w), BF16),
        compiler_params=_cparams("parallel", "arbitrary"),
        name="band",
    )(q, k, v, bm)


def _ssd_kernel(xbc_ref, z_ref, dt_ref, *rest, tc, valid_len):
    prm = rest[:6]
    st0_ref, cbuf_ref, tri_ref, e_ref, y_ref, stn_ref, st_scr, xwin = rest[6:]
    c = pl.program_id(1)

    @pl.when(c == 0)
    def _():
        st_scr[...] = st0_ref[0]
        xwin[0:SUBLANES, :] = cbuf_ref[0]

    y_ref[0] = _ssd_chunk(xbc_ref[0], z_ref[0], dt_ref[0], c * tc, valid_len, prm, st_scr, xwin, tri_ref, e_ref)

    @pl.when(c == pl.num_programs(1) - 1)
    def _():
        stn_ref[0] = st_scr[...]


def _ssd_chunk(u, zv, dt_raw, row0, valid_len, prm, st_scr, xwin, tri_ref, e_ref):
    cw_ref, cb_ref, dtb_ref, alog_ref, dskip_ref, gain_ref = prm
    tc = u.shape[0]
    xwin[SUBLANES:SUBLANES + tc, :] = u
    acc = cb_ref[...] + cw_ref[SSD_CONV - 1:SSD_CONV, :] * u
    for i in range(SSD_CONV - 1):
        off = SUBLANES - (SSD_CONV - 1) + i
        acc = acc + cw_ref[i:i + 1, :] * xwin[off:off + tc, :]
    xwin[0:SUBLANES, :] = u[tc - SUBLANES:, :]
    xc = _silu(acc)
    xs = xc[:, :SSD_INNER]

    rowid = row0 + lax.broadcasted_iota(jnp.int32, (tc, LANES), 0)
    dt = jnp.where(rowid < valid_len, _softplus(dt_raw + dtb_ref[...]), 0.0)
    da = dt * (-jnp.exp(alog_ref[...]))
    cum = _split_dot_left(tri_ref[...], da)
    dt_b = _split_dot(dt, e_ref[...])
    cum_b = _split_dot(cum, e_ref[...])
    last_b = cum_b[tc - 1:tc, :]
    xd = xs * dt_b
    xw = (xd * jnp.exp(last_b - cum_b)).astype(BF16)
    xd_bf = xd.astype(BF16)
    cum_t = cum.T

    lane = _lane_iota((1, LANES))
    ri = lax.broadcasted_iota(jnp.int32, (tc, tc), 0)
    ci = lax.broadcasted_iota(jnp.int32, (tc, tc), 1)
    causal = ri >= ci
    heads_per_group = SSD_HEADS // SSD_GROUPS
    gw = heads_per_group * SSD_P
    gls = [slice(g * gw, (g + 1) * gw) for g in range(SSD_GROUPS)]
    bgs = [xc[:, SSD_INNER + g * SSD_N:SSD_INNER + (g + 1) * SSD_N].astype(BF16) for g in range(SSD_GROUPS)]
    cgs = [xc[:, SSD_INNER + (SSD_GROUPS + g) * SSD_N:SSD_INNER + (SSD_GROUPS + g + 1) * SSD_N].astype(BF16)
           for g in range(SSD_GROUPS)]
    cbs = [_dot_nt(cg, bg) for cg, bg in zip(cgs, bgs)]
    y_inter = jnp.concatenate([_dot(cg, st_scr[:, gl].astype(BF16)) for cg, gl in zip(cgs, gls)], axis=1)
    s_upd = [_dot_tn(bg, xw[:, gl]) for bg, gl in zip(bgs, gls)]
    atts = []
    for pidx in range(SSD_HEADS // 2):
        cpair = cum_b[:, pidx * LANES:(pidx + 1) * LANES]
        rolled = pltpu.roll(cpair, SSD_P, axis=1)
        for hh in range(2):
            h = 2 * pidx + hh
            colv = jnp.where((lane >= SSD_P) if hh else (lane < SSD_P), cpair, rolled)
            seg = jnp.exp(jnp.where(causal, colv - cum_t[h:h + 1, :], NEG_BIG))
            atts.append((cbs[h // heads_per_group] * seg).astype(BF16))
    pairs = []
    for pidx in range(SSD_HEADS // 2):
        xdp = xd_bf[:, pidx * LANES:(pidx + 1) * LANES]
        pairs.append(jnp.where(lane < SSD_P, _dot(atts[2 * pidx], xdp), _dot(atts[2 * pidx + 1], xdp)))
    y_intra = jnp.concatenate(pairs, axis=1)
    for gl, upd in zip(gls, s_upd):
        st_scr[:, gl] = st_scr[:, gl] * jnp.exp(last_b[:, gl]) + upd
    y = y_intra + y_inter * jnp.exp(cum_b) + dskip_ref[...] * xs
    return _rms(y * _silu(zv), gain_ref[...]).astype(BF16)


def _split_dot_left(m, x):
    hi = x.astype(BF16)
    lo = (x - hi.astype(F32)).astype(BF16)
    return _dot(m, hi) + _dot(m, lo)


def _ssd_params(conv_w, conv_b, dt_bias, a_log, d_skip, gain):
    pad = LANES - SSD_HEADS
    tri = (np.arange(SSD_TILE)[:, None] >= np.arange(SSD_TILE)[None, :]).astype(BF16)
    expand = (np.arange(LANES)[:, None] == (np.arange(SSD_INNER) // SSD_P)[None, :]).astype(BF16)
    params = [conv_w, conv_b.reshape(1, -1),
              jnp.pad(dt_bias.astype(F32), (0, pad)).reshape(1, LANES),
              jnp.pad(a_log.astype(F32), (0, pad)).reshape(1, LANES),
              jnp.repeat(d_skip.astype(F32), SSD_P).reshape(1, SSD_INNER),
              gain.reshape(1, SSD_INNER)]
    return params, tri, expand


def _ssd(xbc, z, dt, conv_w, conv_b, dt_bias, a_log, d_skip, gain, st0, cbuf, valid_len):
    b, l, _ = xbc.shape
    tc = SSD_TILE
    assert l % tc == 0
    params, tri, expand = _ssd_params(conv_w, conv_b, dt_bias, a_log, d_skip, gain)
    blk = lambda w: pl.BlockSpec((1, tc, w), lambda i, c: (i, c, 0))
    const2 = lambda i, c: (0, 0)
    cs = lambda a: pl.BlockSpec(a.shape, const2)
    st_spec = pl.BlockSpec((1, SSD_N, SSD_INNER), lambda i, c: (i, 0, 0))
    return pl.pallas_call(
        functools.partial(_ssd_kernel, tc=tc, valid_len=valid_len),
        grid=(b, l // tc),
        in_specs=[blk(SSD_CONV_DIM), blk(SSD_INNER), blk(LANES)] + [cs(p) for p in params]
                 + [st_spec, pl.BlockSpec((1, SUBLANES, SSD_CONV_DIM), lambda i, c: (i, 0, 0)), cs(tri), cs(expand)],
        out_specs=[blk(SSD_INNER), st_spec],
        out_shape=[jax.ShapeDtypeStruct((b, l, SSD_INNER), BF16),
                   jax.ShapeDtypeStruct((b, SSD_N, SSD_INNER), F32)],
        scratch_shapes=[pltpu.VMEM((SSD_N, SSD_INNER), F32), pltpu.VMEM((SUBLANES + tc, SSD_CONV_DIM), F32)],
        compiler_params=_cparams("parallel", "arbitrary"),
        name="ssd",
    )(xbc, z, dt, *params, st0, cbuf, tri, expand)


def _rot_tables(pos, rows):
    half = RET_DK // 2
    inv = np.power(np.float32(ROPE_BASE), -np.arange(half, dtype=np.float32) / np.float32(half))
    ang = pos.astype(np.float32)[:, None] * inv[None, :]
    cos = np.tile(np.cos(ang), (rows // pos.shape[0], 2 * RET_HEADS))
    sin = np.sin(ang)
    sin = np.tile(np.concatenate([-sin, sin], axis=1), (rows // pos.shape[0], RET_HEADS))
    return cos.astype(np.float32), sin.astype(np.float32)


def _ab_layer(x, b, l, pos0, g, w_ext, ret_gain, past, sb_prev):
    t = b * l
    rows = max(l, min(PROJ_TILE, t))
    rot = _rot_tables(pos0 + np.arange(l, dtype=np.int32), rows)
    o = [0]
    for wdt in (RET_QK_W, RET_QK_W, RET_V_W, RET_V_W, SB_W, SB_W, SB_W):
        o.append(o[-1] + wdt)
    specs = [(o[0], RET_QK_W, HEAD_SCALE, True, (BF16,), None),
             (o[1], RET_QK_W, 1.0, True, (BF16,), None),
             (o[2], RET_V_W, 1.0, None, (BF16,), None), (o[3], RET_V_W, 1.0, None, (F32,), None),
             (o[4], SB_W, HEAD_SCALE * LOG2E, None, (BF16,), None), (o[5], SB_W, 1.0, None, (F32,), sb_prev[0]),
             (o[6], SB_W, 1.0, None, (F32,), sb_prev[1])]
    rq, rk, rv, rg, sq, sk, sv = _proj(x, g, w_ext, specs, rot)
    n_sb = sk.shape[0]
    sk4, sv4 = sk.reshape(n_sb, b, l, SB_W), sv.reshape(n_sb, b, l, SB_W)
    r3 = lambda a: a.reshape(b, l, a.shape[-1])
    if past is None:
        s0 = jnp.zeros((b, RET_QK_W, RET_DV), F32)
        tc = min(RET_TILE, l)
    else:
        s0 = past[0].astype(F32).reshape(b, RET_QK_W, RET_DV)
        tc = l
    ro, s_new = _retention(r3(rq), r3(rk), r3(rv), r3(rg), ret_gain, s0, tc)
    if past is None:
        so = _stick_breaking(r3(sq), sk4, sv4, n_sb - 1, sk4, sv4, n_sb - 1, prompt=True)
    else:
        (ck_all, layer), (cv_all, _) = past[1], past[2]
        ck = jnp.transpose(ck_all, (0, 1, 3, 4, 2)).reshape(ck_all.shape[0], b, SB_W, -1)
        cv = jnp.transpose(cv_all, (0, 1, 3, 4, 2)).reshape(cv_all.shape[0], b, SB_W, -1)
        padn = (-l) % LANES
        pad_own = lambda a: jnp.pad(a[-1:], ((0, 0), (0, 0), (0, padn), (0, 0)))
        so = _stick_breaking(r3(sq), pad_own(sk4), pad_own(sv4), 0, ck, cv, layer, prompt=False)
    return ro.reshape(t, RET_V_W), so.reshape(t, SB_W), s_new.reshape(b, RET_HEADS, RET_DK, RET_DV), (sk, sv)


def _cd_layer(x, b, l, pos0, g, w_ext, rel_bias, conv_w, conv_b, dt_bias, a_log, d_skip, ssd_gain, past,
              ffn_args):
    t = b * l
    o = [0]
    for wdt in (BAND_W, BAND_W, BAND_W, SSD_INNER, SSD_CONV_DIM, LANES):
        o.append(o[-1] + wdt)
    specs = [(o[0], BAND_W, HEAD_SCALE, None, (BF16,), None), (o[1], BAND_W, 1.0, None, (F32, BF16), None),
             (o[2], BAND_W, 1.0, None, (F32, BF16), None), (o[3], SSD_INNER, 1.0, None, (F32,), None),
             (o[4], SSD_CONV_DIM, 1.0, None, (F32,), None), (o[5], LANES, 1.0, None, (F32,), None)]
    bq, bk, bk16, bv, bv16, z, xbc, dt = _proj(x, g, w_ext, specs)
    r3 = lambda a: a.reshape(b, l, a.shape[-1])
    bk3, bv3, xbc3 = r3(bk), r3(bv), r3(xbc)
    if past is None:
        tq = min(BAND_TILE, l)
        span = BAND_WINDOW + tq
        nd = BAND_WINDOW // tq + 1
        bm = jnp.stack([_band_bias(rel_bias, d * tq, max(d * tq - BAND_WINDOW, 0), tq, span, span)
                        for d in range(nd)])
        bo = _band(r3(bq), r3(bk16), r3(bv16), bm, tq, span)
        keep = min(BAND_WINDOW, l)
        new_k, new_v = bk3[:, l - keep:], bv3[:, l - keep:]
        st0 = jnp.zeros((b, SSD_N, SSD_INNER), F32)
        cbuf = jnp.zeros((b, SUBLANES, SSD_CONV_DIM), F32)
        lp = l
    else:
        ck = past[0].reshape(b, -1, BAND_W)
        cv = past[1].reshape(b, -1, BAND_W)
        wlen = ck.shape[1]
        span = -(-(wlen + l) // LANES) * LANES
        padk = span - wlen - l
        k_all = jnp.concatenate([ck.astype(BF16), r3(bk16), jnp.zeros((b, padk, BAND_W), BF16)], axis=1)
        v_all = jnp.concatenate([cv.astype(BF16), r3(bv16), jnp.zeros((b, padk, BAND_W), BF16)], axis=1)
        bm = _band_bias(rel_bias, pos0, pos0 - wlen, l, span, wlen + l)[None]
        bo = _band(r3(bq), k_all, v_all, bm, l, span)
        new_k, new_v = bk3, bv3
        st0 = jnp.transpose(past[2].astype(F32), (0, 3, 1, 2)).reshape(b, SSD_N, SSD_INNER)
        cbuf = jnp.pad(past[3].astype(F32), ((0, 0), (SUBLANES - (SSD_CONV - 1), 0), (0, 0)))
        lp = -(-l // SSD_TILE) * SSD_TILE
    assert l >= SSD_CONV - 1
    conv_new = xbc3[:, l - (SSD_CONV - 1):]
    g2, win2, wout2, wo, fin_g = ffn_args
    wa, wb = wo[:BAND_W], wo[BAND_W:]
    ma = bo.reshape(t, BAND_W)
    if past is None and l % FUSED_TILE == 0:
        prm = _ssd_params(conv_w, conv_b, dt_bias, a_log, d_skip, ssd_gain)
        x, st_new = _ffn_ssd(x, g2, win2, wout2, ma, wa, wb, fin_g, xbc, z, dt, prm, st0, cbuf, l)
    else:
        padl = lp - l
        pad3 = lambda a: jnp.pad(a, ((0, 0), (0, padl), (0, 0))) if padl else a
        y, st_new = _ssd(pad3(xbc3), pad3(r3(z)), pad3(r3(dt)), conv_w, conv_b, dt_bias, a_log, d_skip, ssd_gain,
                         st0, cbuf, l)
        x = _ffn(x, g2, win2, wout2, mix=(ma, y[:, :l].reshape(t, SSD_INNER), wa, wb), final_g=fin_g)
    ssm_new = jnp.transpose(st_new.reshape(b, SSD_N, SSD_HEADS, SSD_P), (0, 2, 3, 1))
    outs = (new_k.reshape(b, -1, BAND_HEADS, BAND_DH), new_v.reshape(b, -1, BAND_HEADS, BAND_DH), ssm_new, conv_new)
    return x, outs


def _trunk(x3, pos0, wts, past):
    (norm_g, win, wout, final_g, w_ab, ret_gain, wo_ab, w_cd, rel_bias, conv_w, conv_b, dt_bias, a_log,
     d_skip, ssd_gain, wo_cd) = wts
    b, l, d = x3.shape
    depth = norm_g.shape[0]
    x = x3.reshape(b * l, d)
    outs = ([], None, None, [], [], [], [])
    sb_stack = ((), ())
    x = _ffn(x, norm_g[0, 0], win[0, 0], wout[0, 0])
    for layer in range(depth):
        j = layer // 2
        fin_g = final_g if layer + 1 == depth else None
        if layer % 2 == 0:
            pst = None if past is None else (past[0][j], (past[1], j), (past[2], j))
            ma, mb, r, sb_stack = _ab_layer(x, b, l, pos0, norm_g[layer, 1], w_ab[j], ret_gain[j], pst, sb_stack)
            outs[0].append(r)
            half = ma.shape[1]
            mix = (ma, mb, wo_ab[j][:half], wo_ab[j][half:])
            x = _ffn(x, norm_g[layer, 2], win[layer, 1], wout[layer, 1], mix=mix, final_g=fin_g)
        else:
            pst = None if past is None else (past[3][j], past[4][j], past[5][j], past[6][j])
            ffn_args = (norm_g[layer, 2], win[layer, 1], wout[layer, 1], wo_cd[j], fin_g)
            x, (k, v, s, cb) = _cd_layer(x, b, l, pos0, norm_g[layer, 1], w_cd[j], rel_bias[j], conv_w[j],
                                         conv_b[j], dt_bias[j], a_log[j], d_skip[j], ssd_gain[j], pst, ffn_args)
            outs[3].append(k)
            outs[4].append(v)
            outs[5].append(s)
            outs[6].append(cb)
        if layer + 1 < depth:
            x = _ffn(x, norm_g[layer + 1, 0], win[layer + 1, 0], wout[layer + 1, 0])
    sb_k, sb_v = (jnp.transpose(jnp.swapaxes(a.reshape(a.shape[0], b, l, SB_W), 2, 3)
                                .reshape(a.shape[0], b, SB_HEADS, SB_DH, l), (0, 1, 4, 2, 3)) for a in sb_stack)
    stacked = [jnp.stack(o) for o in outs if o is not None]
    return x.reshape(b, l, d), (stacked[0], sb_k, sb_v, *stacked[1:])


def kernel(x_prompt, x_sample, state_ret, cache_sb_k, cache_sb_v, cache_band_k, cache_band_v, state_ssm, state_conv, norm_g, ffn_w_in, ffn_w_out, final_g, w_in_ab, ret_gain, w_out_ab, w_in_cd, rel_bias, conv_w, conv_b, dt_bias, a_log, d_skip, ssd_gain, w_out_cd):
    win = ffn_w_in.astype(BF16)
    wout = ffn_w_out.astype(BF16)
    w_ab = w_in_ab.astype(BF16)
    n_main = 3 * BAND_W + SSD_INNER + SSD_CONV_DIM
    w_cd = jnp.concatenate([w_in_cd, jnp.zeros(w_in_cd.shape[:2] + (LANES - SSD_HEADS,), w_in_cd.dtype)],
                           axis=-1).astype(BF16)
    assert w_cd.shape[-1] == n_main + LANES
    wts = (norm_g, win, wout, final_g, w_ab, ret_gain, w_out_ab.astype(BF16), w_cd, rel_bias, conv_w, conv_b,
           dt_bias, a_log, d_skip, ssd_gain, w_out_cd.astype(BF16))
    y_p, outs_p = _trunk(x_prompt, 0, wts, None)
    past = (state_ret, cache_sb_k, cache_sb_v, cache_band_k, cache_band_v, state_ssm, state_conv)
    y_s, outs_s = _trunk(x_sample, cache_sb_k.shape[2], wts, past)
    return (y_p, y_s) + outs_p + outs_s
```

```python
import functools

import jax
import jax.numpy as jnp
import numpy as np
from jax import lax
from jax.experimental import pallas as pl
from jax.experimental.pallas import tpu as pltpu

F32 = jnp.float32
BF16 = jnp.bfloat16

CHUNK = 64
EPS = 1e-6
ROPE_BASE = 10000.0
RET_HEADS, RET_DK, RET_DV = 4, 64, 128
SB_HEADS, SB_DH = 8, 64
BAND_HEADS, BAND_DH, BAND_CHUNKS, REL_CLIP = 8, 64, 8, 128
BAND_WINDOW = BAND_CHUNKS * CHUNK
SSD_HEADS, SSD_P, SSD_GROUPS, SSD_N, SSD_CONV = 8, 64, 2, 128, 4
SSD_INNER = SSD_HEADS * SSD_P
SSD_CONV_DIM = SSD_INNER + 2 * SSD_GROUPS * SSD_N
RET_QK_W = RET_HEADS * RET_DK
RET_V_W = RET_HEADS * RET_DV
SB_W = SB_HEADS * SB_DH
BAND_W = BAND_HEADS * BAND_DH
HEAD_SCALE = 0.125

LANES = 128
SUBLANES = 8
VMEM_LIMIT_BYTES = 56 * 1024 * 1024

NEG_BIG = -1e30

ROW_TILE = 1024
PROJ_TILE = 512
FUSED_TILE = 512
FFN_CHUNK = 512
RET_TILE = 256
SB_TILE = 256
BAND_TILE = 128
BAND_GROUP = 2
BAND_SUB = 4
SSD_TILE = 128


def _cparams(*sem):
    return pltpu.CompilerParams(dimension_semantics=sem, vmem_limit_bytes=VMEM_LIMIT_BYTES)


def _rms(x, g):
    return x * lax.rsqrt(jnp.mean(x * x, axis=-1, keepdims=True) + EPS) * g


def _silu(x):
    return x * jax.nn.sigmoid(x)


def _softplus(x):
    return jnp.maximum(x, 0.0) + jnp.log(1.0 + jnp.exp(-jnp.abs(x)))


def _dot(a, b):
    return jnp.dot(a, b, preferred_element_type=F32)


def _dot_nt(a, b):
    return lax.dot_general(a, b, (((1,), (1,)), ((), ())), preferred_element_type=F32)


def _dot_tn(a, b):
    return lax.dot_general(a, b, (((0,), (0,)), ((), ())), preferred_element_type=F32)


def _split_dot(x, m):
    hi = x.astype(BF16)
    lo = (x - hi.astype(F32)).astype(BF16)
    return _dot(hi, m) + _dot(lo, m)


def _lane_iota(shape):
    return lax.broadcasted_iota(jnp.int32, shape, len(shape) - 1)


def _ffn_kernel(*refs, f, has_mix, final):
    refs = list(refs)
    x_ref = refs.pop(0)
    if has_mix:
        ma_ref, mb_ref, wa_ref, wb_ref = refs[:4]
        refs = refs[4:]
    g_ref, win_ref, wout_ref = refs[:3]
    refs = refs[3:]
    if final:
        fg_ref = refs.pop(0)
    o_ref, x1_ref, a_ref = refs

    x = x_ref[...]
    if has_mix:
        x = x + _dot(ma_ref[...], wa_ref[...]) + _dot(mb_ref[...], wb_ref[...])
    o_ref[...] = _ffn_body(x, g_ref, win_ref, wout_ref, fg_ref if final else None, x1_ref, a_ref, f)


def _ffn_body(x, g_ref, win_ref, wout_ref, fg_ref, x1_ref, a_ref, f, side_jobs=()):
    x1_ref[...] = x
    h = _rms(x, g_ref[...]).astype(BF16)
    chunks = list(range(0, f, FFN_CHUNK))
    slots = {}
    for k, job in enumerate(side_jobs):
        slots.setdefault(k * (len(chunks) + 1) // len(side_jobs), []).append(job)
    for n, f0 in enumerate(chunks):
        for job in slots.get(n, ()):
            job()
        f1 = min(f0 + FFN_CHUNK, f)
        gate = _dot(h, win_ref[:, f0:f1])
        up = _dot(h, win_ref[:, f + f0:f + f1])
        a_ref[:, f0:f1] = (_silu(gate) * up).astype(BF16)
    for job in slots.get(len(chunks), ()):
        job()
    y = x1_ref[...] + 0.5 * _dot(a_ref[...], wout_ref[...])
    if fg_ref is not None:
        y = _rms(y, fg_ref[...])
    return y


def _ffn_ssd_kernel(*refs, f, final, tc, tiles_per_batch, seq_len):
    refs = list(refs)
    x_ref, ma_ref, wa_ref, wb_ref, g_ref, win_ref, wout_ref = refs[:7]
    refs = refs[7:]
    fg_ref = refs.pop(0) if final else None
    xbc_ref, z_ref, dt_ref = refs[:3]
    prm = refs[3:9]
    st0_ref, cbuf_ref, tri_ref, e_ref, o_ref, stn_ref, x1_ref, a_ref, st_scr, xwin, ybuf = refs[9:]
    i = pl.program_id(0)
    n = pl.num_programs(0) - 1
    tile = i % tiles_per_batch
    tm = x_ref.shape[0]

    @pl.when(i == 0)
    def _():
        ybuf[...] = jnp.zeros_like(ybuf)

    @pl.when((tile == 0) & (i < n))
    def _():
        st_scr[...] = st0_ref[0]
        xwin[0:SUBLANES, :] = cbuf_ref[0]

    def ssd_job(c):
        def job():
            rows = slice(c * tc, (c + 1) * tc)
            ybuf[rows, :] = _ssd_chunk(xbc_ref[rows, :], z_ref[rows, :], dt_ref[rows, :], tile * tm + c * tc,
                                       seq_len, prm, st_scr, xwin, tri_ref, e_ref)
        return job

    x = x_ref[...] + _dot(ma_ref[...], wa_ref[...]) + _dot(ybuf[...], wb_ref[...])
    o_ref[...] = _ffn_body(x, g_ref, win_ref, wout_ref, fg_ref, x1_ref, a_ref, f,
                           side_jobs=[ssd_job(c) for c in range(tm // tc)])

    @pl.when((tile == tiles_per_batch - 1) & (i < n))
    def _():
        stn_ref[0] = st_scr[...]


def _resident(shape):
    return pl.BlockSpec(shape, lambda *_: (0,) * len(shape), pipeline_mode=pl.Buffered(1))


def _ffn(x, g, win, wout, mix=None, final_g=None):
    t, d = x.shape
    f = wout.shape[0]
    tm = min(ROW_TILE, t)
    assert t % tm == 0 and f % LANES == 0
    row = lambda i: (i, 0)
    args = [x]
    specs = [pl.BlockSpec((tm, d), row)]
    if mix is not None:
        ma, mb, wa, wb = mix
        args += [ma, mb, wa, wb]
        specs += [pl.BlockSpec((tm, ma.shape[1]), row), pl.BlockSpec((tm, mb.shape[1]), row),
                  _resident(wa.shape), _resident(wb.shape)]
    args += [g.reshape(1, d), win, wout]
    specs += [_resident((1, d)), _resident(win.shape), _resident(wout.shape)]
    if final_g is not None:
        args.append(final_g.reshape(1, d))
        specs.append(_resident((1, d)))
    return pl.pallas_call(
        functools.partial(_ffn_kernel, f=f, has_mix=mix is not None, final=final_g is not None),
        grid=(t // tm,),
        in_specs=specs,
        out_specs=pl.BlockSpec((tm, d), row),
        out_shape=jax.ShapeDtypeStruct((t, d), F32),
        scratch_shapes=[pltpu.VMEM((tm, d), F32), pltpu.VMEM((tm, f), BF16)],
        compiler_params=_cparams("parallel"),
        name="ffn",
    )(*args)


def _ffn_ssd(x, g, win, wout, ma, wa, wb, final_g, xbc, z, dt, ssd_prm, st0, cbuf, seq_len):
    t, d = x.shape
    f = wout.shape[0]
    tm = FUSED_TILE
    assert t % tm == 0 and seq_len % tm == 0 and tm % SSD_TILE == 0
    n = t // tm
    tpb = seq_len // tm
    params, tri, expand = ssd_prm
    lag = lambda i: (jnp.maximum(i - 1, 0), 0)
    cur = lambda i: (jnp.minimum(i, n - 1), 0)
    bat = lambda i: (jnp.minimum(i, n - 1) // tpb, 0, 0)
    args = [x, ma, wa, wb, g.reshape(1, d), win, wout]
    specs = [pl.BlockSpec((tm, d), lag), pl.BlockSpec((tm, ma.shape[1]), lag), _resident(wa.shape),
             _resident(wb.shape), _resident((1, d)), _resident(win.shape), _resident(wout.shape)]
    if final_g is not None:
        args.append(final_g.reshape(1, d))
        specs.append(_resident((1, d)))
    args += [xbc, z, dt, *params, st0, cbuf, tri, expand]
    specs += [pl.BlockSpec((tm, xbc.shape[1]), cur), pl.BlockSpec((tm, z.shape[1]), cur),
              pl.BlockSpec((tm, dt.shape[1]), cur)] + [_resident(p.shape) for p in params]
    specs += [pl.BlockSpec((1,) + st0.shape[1:], bat), pl.BlockSpec((1,) + cbuf.shape[1:], bat),
              _resident(tri.shape), _resident(expand.shape)]
    return pl.pallas_call(
        functools.partial(_ffn_ssd_kernel, f=f, final=final_g is not None, tc=SSD_TILE, tiles_per_batch=tpb,
                          seq_len=seq_len),
        grid=(n + 1,),
        in_specs=specs,
        out_specs=[pl.BlockSpec((tm, d), lag), pl.BlockSpec((1,) + st0.shape[1:], bat)],
        out_shape=[jax.ShapeDtypeStruct((t, d), F32), jax.ShapeDtypeStruct(st0.shape, F32)],
        scratch_shapes=[pltpu.VMEM((tm, d), F32), pltpu.VMEM((tm, f), BF16), pltpu.VMEM(st0.shape[1:], F32),
                        pltpu.VMEM((SUBLANES + SSD_TILE, xbc.shape[1]), F32), pltpu.VMEM((tm, SSD_INNER), BF16)],
        compiler_params=_cparams("arbitrary"),
        name="ffn_ssd",
    )(*args)


def _proj_kernel(*refs, specs, has_rot):
    refs = list(refs)
    x_ref, g_ref, w_ref = refs[:3]
    refs = refs[3:]
    if has_rot:
        cos_ref, sin_ref = refs[:2]
        refs = refs[2:]
    prev_refs = [refs.pop(0) for s in specs if s[5]]
    h = _rms(x_ref[...], g_ref[...]).astype(BF16)
    for c0, width, scale, c_swap, dts, n_prev in specs:
        y = _dot(h, w_ref[:, c0:c0 + width])
        if c_swap:
            half = RET_DK // 2
            lane = _lane_iota((1, width))
            partner = jnp.where(lane % RET_DK < half, pltpu.roll(y, width - half, axis=1),
                                pltpu.roll(y, half, axis=1))
            y = y * cos_ref[...] + partner * sin_ref[...]
        if scale != 1.0:
            y = y * scale
        for n, _ in enumerate(dts):
            o_ref = refs.pop(0)
            if n == 0 and n_prev is not None:
                if n_prev:
                    o_ref[0:n_prev] = prev_refs.pop(0)[...]
                o_ref[n_prev] = y.astype(o_ref.dtype)
            else:
                o_ref[...] = y.astype(o_ref.dtype)


def _proj(x, g, w, specs, rot=None):
    t, d = x.shape
    tm = min(PROJ_TILE, t)
    assert t % tm == 0
    row = lambda i: (i, 0)
    row3 = lambda i: (0, i, 0)
    const2 = lambda i: (0, 0)
    args = [x, g.reshape(1, d), w]
    in_specs = [pl.BlockSpec((tm, d), row), pl.BlockSpec((1, d), const2), _resident(w.shape)]
    if rot is not None:
        cos, sin = rot
        nrot = cos.shape[0] // tm
        rot_map = lambda i: (i % nrot, 0)
        args += [cos, sin]
        in_specs += [pl.BlockSpec((tm, cos.shape[1]), rot_map), pl.BlockSpec((tm, sin.shape[1]), rot_map)]
    kspecs, out_specs, out_shape = [], [], []
    for c0, width, scale, c_swap, dts, prev in specs:
        n_prev = None
        if prev is not None:
            n_prev = 0 if prev is () else prev.shape[0]
            if n_prev:
                args.append(prev)
                in_specs.append(pl.BlockSpec((n_prev, tm, width), row3))
        kspecs.append((c0, width, scale, c_swap, dts, n_prev))
        for n, dt in enumerate(dts):
            if n == 0 and n_prev is not None:
                out_specs.append(pl.BlockSpec((n_prev + 1, tm, width), row3))
                out_shape.append(jax.ShapeDtypeStruct((n_prev + 1, t, width), dt))
            else:
                out_specs.append(pl.BlockSpec((tm, width), row))
                out_shape.append(jax.ShapeDtypeStruct((t, width), dt))
    return pl.pallas_call(
        functools.partial(_proj_kernel, specs=tuple(kspecs), has_rot=rot is not None),
        grid=(t // tm,),
        in_specs=in_specs,
        out_specs=out_specs,
        out_shape=out_shape,
        compiler_params=_cparams("parallel"),
        name="proj",
    )(*args)


def _ret_kernel(q_ref, k_ref, v_ref, rg_ref, gain_ref, s0_ref, dm_ref, qd_ref, kd_ref, cdec_ref,
                o_ref, sn_ref, s_scr):
    c = pl.program_id(1)

    @pl.when(c == 0)
    def _():
        s_scr[...] = s0_ref[0]

    q = q_ref[0]
    k = k_ref[0]
    v = v_ref[0]
    qs = (q.astype(F32) * qd_ref[...]).astype(BF16)
    ks = (k.astype(F32) * kd_ref[...]).astype(BF16)
    lane = _lane_iota((1, RET_QK_W))
    s_all = s_scr[...]
    s_bf = s_all.astype(BF16)
    upd = _dot_tn(ks, v)
    zero = jnp.zeros_like(q)
    heads = range(RET_HEADS)
    hms = [(lane >= h * RET_DK) & (lane < (h + 1) * RET_DK) for h in heads]
    sls = [slice(h * RET_DV, (h + 1) * RET_DV) for h in heads]
    atts = [_dot_nt(jnp.where(hm, q, zero), k) * dm_ref[h] for h, hm in zip(heads, hms)]
    cross = [_dot(jnp.where(hm, qs, zero), s_bf) for hm in hms]
    outs = [_dot(att.astype(BF16), v[:, sl]) + cr for att, sl, cr in zip(atts, sls, cross)]
    outs = [o * lax.rsqrt(jnp.mean(o * o, axis=-1, keepdims=True) + EPS) for o in outs]
    o = jnp.concatenate(outs, axis=1)
    o_ref[0] = (o * gain_ref[...] * _silu(rg_ref[0])).astype(o_ref.dtype)
    for h, sl in zip(heads, sls):
        rows = slice(h * RET_DK, (h + 1) * RET_DK)
        s_scr[rows, :] = s_all[rows, :] * cdec_ref[rows, :] + upd[rows, sl]

    @pl.when(c == pl.num_programs(1) - 1)
    def _():
        sn_ref[0] = s_scr[...]


def _retention(q, k, v, rg, gain, s0, tc):
    b, l, _ = q.shape
    assert l % tc == 0
    f32 = np.float32
    lg = np.log1p(-np.exp2(-5.0 - np.arange(RET_HEADS, dtype=f32))).astype(f32)
    idx = np.arange(tc, dtype=f32)
    diff = idx[:, None] - idx[None, :]
    dm = np.where(diff >= 0, np.exp(lg[:, None, None] * np.maximum(diff, 0.0)), 0.0).astype(f32)
    qd = np.repeat(np.exp(lg[None, :] * (idx[:, None] + 1.0)), RET_DK, axis=1).astype(f32)
    kd = np.repeat(np.exp(lg[None, :] * (tc - 1.0 - idx[:, None])), RET_DK, axis=1).astype(f32)
    cdec = np.broadcast_to(np.repeat(np.exp(lg * f32(tc)), RET_DK)[:, None], (RET_QK_W, RET_DV)).astype(f32)
    blk = lambda w: pl.BlockSpec((1, tc, w), lambda i, c: (i, c, 0))
    const2 = lambda i, c: (0, 0)
    st_spec = pl.BlockSpec((1, RET_QK_W, RET_DV), lambda i, c: (i, 0, 0))
    return pl.pallas_call(
        _ret_kernel,
        grid=(b, l // tc),
        in_specs=[blk(RET_QK_W), blk(RET_QK_W), blk(RET_V_W), blk(RET_V_W),
                  pl.BlockSpec((1, RET_V_W), const2), st_spec,
                  pl.BlockSpec((RET_HEADS, tc, tc), lambda i, c: (0, 0, 0)),
                  pl.BlockSpec((tc, RET_QK_W), const2), pl.BlockSpec((tc, RET_QK_W), const2),
                  pl.BlockSpec((RET_QK_W, RET_DV), const2)],
        out_specs=[blk(RET_V_W), st_spec],
        out_shape=[jax.ShapeDtypeStruct((b, l, RET_V_W), BF16),
                   jax.ShapeDtypeStruct((b, RET_QK_W, RET_DV), F32)],
        scratch_shapes=[pltpu.VMEM((RET_QK_W, RET_DV), F32)],
        compiler_params=_cparams("parallel", "arbitrary"),
        name="retention",
    )(q, k, v, rg, gain.reshape(1, RET_V_W), s0, dm, qd, kd, cdec)


def _cum_matrix(n):
    tri = (np.arange(n)[:, None] >= np.arange(n)[None, :]).astype(BF16)
    return np.concatenate([tri, tri], axis=0)


LOG2E = 1.4426950408889634
SB_UNDERFLOW = -160.0


def _sb_scores(qms, kbs, transposed):
    return [_dot(qm, kb) if transposed else _dot_nt(qm, kb) for qm, kb in zip(qms, kbs)]


def _sb_logs(zs, mask):
    nzs = [-z for z in zs]
    lgs = [jnp.minimum(nz, 0.0) - jnp.log2(1.0 + jnp.exp2(jnp.minimum(z, nz))) for z, nz in zip(zs, nzs)]
    if mask is not None:
        lgs = [jnp.where(mask, lg, 0.0) for lg in lgs]
    his = [lg.astype(BF16) for lg in lgs]
    return [jnp.concatenate([hi, (lg - hi.astype(F32)).astype(BF16)], axis=1) for lg, hi in zip(lgs, his)]


def _sb_weights(zs, incls, carries, mask):
    if carries is None:
        ws = [jnp.exp2(z + incl) for z, incl in zip(zs, incls)]
    else:
        ws = [jnp.exp2(z + incl + c) for z, incl, c in zip(zs, incls, carries)]
    if mask is not None:
        ws = [jnp.where(mask, w, 0.0) for w in ws]
    return [w.astype(BF16) for w in ws]


def _sb_values(ws, vbs, transposed):
    return [_dot_nt(w, vb) if transposed else _dot(w, vb) for w, vb in zip(ws, vbs)]


def _sb_block(qms, kbs, vbs, cum_m, carries, mask, transposed=False):
    zs = _sb_scores(qms, kbs, transposed)
    incls = [_dot(hl, cum_m) for hl in _sb_logs(zs, mask)]
    outs = _sb_values(_sb_weights(zs, incls, carries, mask), vbs, transposed)
    return outs, [incl[:, 0:1] for incl in incls]


SB_PAIRS = 2


def _sb_kernel(q_ref, kd_ref, vd_ref, kp_ref, vp_ref, md_ref, mp_ref, o_ref, *, tq, tkd, tk, blocks_per_q,
               n_past, past_t):
    i = pl.program_id(2)
    lane = _lane_iota((1, LANES))
    sls = [slice(p * LANES, (p + 1) * LANES) for p in range(SB_PAIRS)]
    nh = 2 * SB_PAIRS
    qms = []
    for sl in sls:
        q = q_ref[0, :, sl]
        zero = jnp.zeros_like(q)
        qms += [jnp.where(lane < SB_DH, q, zero), jnp.where(lane >= SB_DH, q, zero)]
    row = lax.broadcasted_iota(jnp.int32, (tq, tkd), 0)
    col = lax.broadcasted_iota(jnp.int32, (tq, tkd), 1)
    dmask = col < row
    nblk = i * blocks_per_q if n_past is None else n_past

    per_head = lambda xs: [x for x in xs for _ in range(2)]

    def past_kv(j):
        off = pl.multiple_of(j * tk, tk)
        if past_t:
            return (per_head([kp_ref[0, sl, pl.ds(off, tk)].astype(BF16) for sl in sls]),
                    per_head([vp_ref[0, sl, pl.ds(off, tk)].astype(BF16) for sl in sls]))
        return (per_head([kp_ref[0, pl.ds(off, tk), sl].astype(BF16) for sl in sls]),
                per_head([vp_ref[0, pl.ds(off, tk), sl].astype(BF16) for sl in sls]))

    j0 = jnp.asarray(nblk - 1, jnp.int32)
    kds = per_head([kd_ref[0, :, sl].astype(BF16) for sl in sls])
    vds = per_head([vd_ref[0, :, sl].astype(BF16) for sl in sls])
    kbs, vbs = past_kv(jnp.maximum(j0, 0))
    z_own, z_far = _sb_scores(qms, kds, False), _sb_scores(qms, kbs, past_t)
    l_own, l_far = _sb_logs(z_own, dmask), _sb_logs(z_far, None)
    in_own = [_dot(hl, md_ref[...]) for hl in l_own]
    in_far = [_dot(hl, mp_ref[...]) for hl in l_far]
    c_far = [jnp.where(j0 >= 0, incl[:, 0:1], NEG_BIG) for incl in in_own]
    w_own, w_far = _sb_weights(z_own, in_own, None, dmask), _sb_weights(z_far, in_far, c_far, None)
    accs = [a + d for a, d in zip(_sb_values(w_own, vds, False), _sb_values(w_far, vbs, past_t))]
    carries = [c + incl[:, 0:1] for c, incl in zip(c_far, in_far)]

    def live(cs):
        return jnp.max(functools.reduce(jnp.maximum, cs)) > SB_UNDERFLOW

    def cond(st):
        return (st[0] >= 0) & st[1]

    def body(st):
        cs = st[2 + nh:]
        kbs, vbs = past_kv(st[0])
        ds, ts = _sb_block(qms, kbs, vbs, mp_ref[...], cs, None, transposed=past_t)
        cs = [c + t for c, t in zip(cs, ts)]
        return (st[0] - 1, live(cs), *[a + d for a, d in zip(st[2:2 + nh], ds)], *cs)

    st = lax.while_loop(cond, body, (j0 - 1, live(carries), *accs, *carries))
    pairs = [jnp.where(lane < SB_DH, st[2 + 2 * p], st[3 + 2 * p]) for p in range(SB_PAIRS)]
    o_ref[0] = jnp.concatenate(pairs, axis=1).astype(o_ref.dtype)


def _stick_breaking(q, k_own, v_own, own_layer, k_past, v_past, past_layer, prompt):
    b, lq, w = q.shape
    wb = SB_PAIRS * LANES
    if prompt:
        tq = tkd = tk = min(SB_TILE, lq)
        blocks_per_q, n_past = 1, None
        lp = k_past.shape[2]
        pspec = pl.BlockSpec((None, 1, lp, wb), lambda i, p, j: (past_layer, i, 0, p))
    else:
        tq = lq
        tkd = k_own.shape[2]
        tk = SB_TILE
        lp = k_past.shape[3]
        assert lp % tk == 0
        blocks_per_q, n_past = 0, lp // tk
        pspec = pl.BlockSpec((None, 1, wb, lp), lambda i, p, j: (past_layer, i, p, 0))
    assert lq % tq == 0
    qspec = pl.BlockSpec((1, tq, wb), lambda i, p, j: (i, j, p))
    dspec = pl.BlockSpec((None, 1, tkd, wb), lambda i, p, j: (own_layer, i, j, p))
    md = _cum_matrix(tkd)
    mp = _cum_matrix(tk)
    const2 = lambda i, p, j: (0, 0)
    return pl.pallas_call(
        functools.partial(_sb_kernel, tq=tq, tkd=tkd, tk=tk, blocks_per_q=blocks_per_q, n_past=n_past,
                          past_t=not prompt),
        grid=(b, w // wb, lq // tq),
        in_specs=[qspec, dspec, dspec, pspec, pspec, pl.BlockSpec(md.shape, const2), pl.BlockSpec(mp.shape, const2)],
        out_specs=qspec,
        out_shape=jax.ShapeDtypeStruct((b, lq, w), BF16),
        compiler_params=_cparams("parallel", "parallel", "arbitrary"),
        name="stick_breaking",
    )(q, k_own, v_own, k_past, v_past, md, mp)


def _band_kernel(q_ref, k_ref, v_ref, bm_ref, o_ref, *, tq, span, sub):
    lane = _lane_iota((1, LANES))
    npair = BAND_W // LANES
    sls = [slice(pr * LANES, (pr + 1) * LANES) for pr in range(npair)]
    for s in range(sub):
        rows = slice(s * tq, (s + 1) * tq)
        tile = pl.program_id(1) * sub + s
        ws = pl.multiple_of(jnp.maximum(tile * tq - (span - tq), 0), LANES)
        pairs = []
        for g0 in range(0, npair, BAND_GROUP):
            group = range(g0, g0 + BAND_GROUP)
            scores = []
            for pr in group:
                kw = k_ref[0, pl.ds(ws, span), sls[pr]]
                q = q_ref[0, rows, sls[pr]]
                zero = jnp.zeros_like(q)
                q2 = jnp.concatenate([jnp.where(lane < BAND_DH, q, zero), jnp.where(lane >= BAND_DH, q, zero)],
                                     axis=0)
                bias2 = jnp.concatenate([bm_ref[0, s, 2 * pr], bm_ref[0, s, 2 * pr + 1]], axis=0)
                scores.append(_dot_nt(q2, kw) + bias2)
            probs = [jnp.exp(sc - jnp.max(sc, axis=-1, keepdims=True)) for sc in scores]
            for n, pr in enumerate(group):
                vw = v_ref[0, pl.ds(ws, span), sls[pr]]
                p = probs[n]
                out2 = _dot(p.astype(BF16), vw) / jnp.sum(p, axis=-1, keepdims=True)
                pairs.append(jnp.where(lane < BAND_DH, out2[:tq], out2[tq:]))
        o_ref[0, rows, :] = jnp.concatenate(pairs, axis=1).astype(o_ref.dtype)


def _band_bias(rel_bias, q0, k0, tq, span, n_keys):
    d0 = q0 - k0
    n = span + tq
    padw = n + abs(d0)
    tab = rel_bias.astype(F32)
    heads = tab.shape[0]
    ext = jnp.concatenate([jnp.broadcast_to(tab[:, :1], (heads, padw)), tab,
                           jnp.broadcast_to(tab[:, -1:], (heads, padw))], axis=1)
    base = padw + REL_CLIP + d0
    g = jnp.concatenate([jnp.flip(ext[:, base - span + 1:base + 1], axis=1),
                         jnp.flip(ext[:, base + 1:base + tq + 1], axis=1)], axis=1)
    toe = jnp.tile(g, (1, tq))[:, :tq * (n - 1)].reshape(heads, tq, n - 1)[:, :, :span]
    q_pos = q0 + np.arange(tq)[:, None]
    k_pos = k0 + np.arange(span)[None, :]
    qc, kc = q_pos // CHUNK, k_pos // CHUNK
    mask = (np.arange(span)[None, :] < n_keys) & (k_pos >= 0) & (kc <= qc) & (kc >= qc - BAND_CHUNKS)
    return jnp.where(jnp.asarray(mask)[None], toe, NEG_BIG)


def _band(q, k, v, bm, tq, span):
    b, lq, w = q.shape
    nd, heads = bm.shape[:2]
    lk = k.shape[1]
    sub = BAND_SUB if (lq // tq) % BAND_SUB == 0 else 1
    nsteps = -(-(nd - 1) // sub) + 1
    bm = jnp.stack([jnp.stack([bm[min(sub * j + s, nd - 1)] for s in range(sub)]) for j in range(nsteps)])
    qspec = pl.BlockSpec((1, sub * tq, w), lambda i, j: (i, j, 0))
    kspec = pl.BlockSpec((1, lk, w), lambda i, j: (i, 0, 0))
    bspec = pl.BlockSpec((1, sub, heads, tq, span), lambda i, j: (jnp.minimum(j, nsteps - 1), 0, 0, 0, 0))
    return pl.pallas_call(
        functools.partial(_band_kernel, tq=tq, span=span, sub=sub),
        grid=(b, lq // (sub * tq)),
        in_specs=[qspec, kspec, kspec, bspec],
        out_specs=qspec,
        out_shape=jax.ShapeDtypeStruct((b, lq, w), BF16),
        compiler_params=_cparams("parallel", "arbitrary"),
        name="band",
    )(q, k, v, bm)


def _ssd_kernel(xbc_ref, z_ref, dt_ref, *rest, tc, valid_len):
    prm = rest[:6]
    st0_ref, cbuf_ref, tri_ref, e_ref, y_ref, stn_ref, st_scr, xwin = rest[6:]
    c = pl.program_id(1)

    @pl.when(c == 0)
    def _():
        st_scr[...] = st0_ref[0]
        xwin[0:SUBLANES, :] = cbuf_ref[0]

    y_ref[0] = _ssd_chunk(xbc_ref[0], z_ref[0], dt_ref[0], c * tc, valid_len, prm, st_scr, xwin, tri_ref, e_ref)

    @pl.when(c == pl.num_programs(1) - 1)
    def _():
        stn_ref[0] = st_scr[...]


def _ssd_chunk(u, zv, dt_raw, row0, valid_len, prm, st_scr, xwin, tri_ref, e_ref):
    cw_ref, cb_ref, dtb_ref, alog_ref, dskip_ref, gain_ref = prm
    tc = u.shape[0]
    xwin[SUBLANES:SUBLANES + tc, :] = u
    acc = cb_ref[...] + cw_ref[SSD_CONV - 1:SSD_CONV, :] * u
    for i in range(SSD_CONV - 1):
        off = SUBLANES - (SSD_CONV - 1) + i
        acc = acc + cw_ref[i:i + 1, :] * xwin[off:off + tc, :]
    xwin[0:SUBLANES, :] = u[tc - SUBLANES:, :]
    xc = _silu(acc)
    xs = xc[:, :SSD_INNER]

    rowid = row0 + lax.broadcasted_iota(jnp.int32, (tc, LANES), 0)
    dt = jnp.where(rowid < valid_len, _softplus(dt_raw + dtb_ref[...]), 0.0)
    da = dt * (-jnp.exp(alog_ref[...]))
    cum = _split_dot_left(tri_ref[...], da)
    dt_b = _split_dot(dt, e_ref[...])
    cum_b = _split_dot(cum, e_ref[...])
    last_b = cum_b[tc - 1:tc, :]
    xd = xs * dt_b
    xw = (xd * jnp.exp(last_b - cum_b)).astype(BF16)
    xd_bf = xd.astype(BF16)
    cum_t = cum.T

    lane = _lane_iota((1, LANES))
    ri = lax.broadcasted_iota(jnp.int32, (tc, tc), 0)
    ci = lax.broadcasted_iota(jnp.int32, (tc, tc), 1)
    causal = ri >= ci
    heads_per_group = SSD_HEADS // SSD_GROUPS
    gw = heads_per_group * SSD_P
    gls = [slice(g * gw, (g + 1) * gw) for g in range(SSD_GROUPS)]
    bgs = [xc[:, SSD_INNER + g * SSD_N:SSD_INNER + (g + 1) * SSD_N].astype(BF16) for g in range(SSD_GROUPS)]
    cgs = [xc[:, SSD_INNER + (SSD_GROUPS + g) * SSD_N:SSD_INNER + (SSD_GROUPS + g + 1) * SSD_N].astype(BF16)
           for g in range(SSD_GROUPS)]
    cbs = [_dot_nt(cg, bg) for cg, bg in zip(cgs, bgs)]
    y_inter = jnp.concatenate([_dot(cg, st_scr[:, gl].astype(BF16)) for cg, gl in zip(cgs, gls)], axis=1)
    s_upd = [_dot_tn(bg, xw[:, gl]) for bg, gl in zip(bgs, gls)]
    atts = []
    for pidx in range(SSD_HEADS // 2):
        cpair = cum_b[:, pidx * LANES:(pidx + 1) * LANES]
        rolled = pltpu.roll(cpair, SSD_P, axis=1)
        for hh in range(2):
            h = 2 * pidx + hh
            colv = jnp.where((lane >= SSD_P) if hh else (lane < SSD_P), cpair, rolled)
            seg = jnp.exp(jnp.where(causal, colv - cum_t[h:h + 1, :], NEG_BIG))
            atts.append((cbs[h // heads_per_group] * seg).astype(BF16))
    pairs = []
    for pidx in range(SSD_HEADS // 2):
        xdp = xd_bf[:, pidx * LANES:(pidx + 1) * LANES]
        pairs.append(jnp.where(lane < SSD_P, _dot(atts[2 * pidx], xdp), _dot(atts[2 * pidx + 1], xdp)))
    y_intra = jnp.concatenate(pairs, axis=1)
    for gl, upd in zip(gls, s_upd):
        st_scr[:, gl] = st_scr[:, gl] * jnp.exp(last_b[:, gl]) + upd
    y = y_intra + y_inter * jnp.exp(cum_b) + dskip_ref[...] * xs
    return _rms(y * _silu(zv), gain_ref[...]).astype(BF16)


def _split_dot_left(m, x):
    hi = x.astype(BF16)
    lo = (x - hi.astype(F32)).astype(BF16)
    return _dot(m, hi) + _dot(m, lo)


def _ssd_params(conv_w, conv_b, dt_bias, a_log, d_skip, gain):
    pad = LANES - SSD_HEADS
    tri = (np.arange(SSD_TILE)[:, None] >= np.arange(SSD_TILE)[None, :]).astype(BF16)
    expand = (np.arange(LANES)[:, None] == (np.arange(SSD_INNER) // SSD_P)[None, :]).astype(BF16)
    params = [conv_w, conv_b.reshape(1, -1),
              jnp.pad(dt_bias.astype(F32), (0, pad)).reshape(1, LANES),
              jnp.pad(a_log.astype(F32), (0, pad)).reshape(1, LANES),
              jnp.repeat(d_skip.astype(F32), SSD_P).reshape(1, SSD_INNER),
              gain.reshape(1, SSD_INNER)]
    return params, tri, expand


def _ssd(xbc, z, dt, conv_w, conv_b, dt_bias, a_log, d_skip, gain, st0, cbuf, valid_len):
    b, l, _ = xbc.shape
    tc = SSD_TILE
    assert l % tc == 0
    params, tri, expand = _ssd_params(conv_w, conv_b, dt_bias, a_log, d_skip, gain)
    blk = lambda w: pl.BlockSpec((1, tc, w), lambda i, c: (i, c, 0))
    const2 = lambda i, c: (0, 0)
    cs = lambda a: pl.BlockSpec(a.shape, const2)
    st_spec = pl.BlockSpec((1, SSD_N, SSD_INNER), lambda i, c: (i, 0, 0))
    return pl.pallas_call(
        functools.partial(_ssd_kernel, tc=tc, valid_len=valid_len),
        grid=(b, l // tc),
        in_specs=[blk(SSD_CONV_DIM), blk(SSD_INNER), blk(LANES)] + [cs(p) for p in params]
                 + [st_spec, pl.BlockSpec((1, SUBLANES, SSD_CONV_DIM), lambda i, c: (i, 0, 0)), cs(tri), cs(expand)],
        out_specs=[blk(SSD_INNER), st_spec],
        out_shape=[jax.ShapeDtypeStruct((b, l, SSD_INNER), BF16),
                   jax.ShapeDtypeStruct((b, SSD_N, SSD_INNER), F32)],
        scratch_shapes=[pltpu.VMEM((SSD_N, SSD_INNER), F32), pltpu.VMEM((SUBLANES + tc, SSD_CONV_DIM), F32)],
        compiler_params=_cparams("parallel", "arbitrary"),
        name="ssd",
    )(xbc, z, dt, *params, st0, cbuf, tri, expand)


def _rot_tables(pos, rows):
    half = RET_DK // 2
    inv = np.power(np.float32(ROPE_BASE), -np.arange(half, dtype=np.float32) / np.float32(half))
    ang = pos.astype(np.float32)[:, None] * inv[None, :]
    cos = np.tile(np.cos(ang), (rows // pos.shape[0], 2 * RET_HEADS))
    sin = np.sin(ang)
    sin = np.tile(np.concatenate([-sin, sin], axis=1), (rows // pos.shape[0], RET_HEADS))
    return cos.astype(np.float32), sin.astype(np.float32)


def _ab_layer(x, b, l, pos0, g, w_ext, ret_gain, past, sb_prev):
    t = b * l
    rows = max(l, min(PROJ_TILE, t))
    rot = _rot_tables(pos0 + np.arange(l, dtype=np.int32), rows)
    o = [0]
    for wdt in (RET_QK_W, RET_QK_W, RET_V_W, RET_V_W, SB_W, SB_W, SB_W):
        o.append(o[-1] + wdt)
    specs = [(o[0], RET_QK_W, HEAD_SCALE, True, (BF16,), None),
             (o[1], RET_QK_W, 1.0, True, (BF16,), None),
             (o[2], RET_V_W, 1.0, None, (BF16,), None), (o[3], RET_V_W, 1.0, None, (F32,), None),
             (o[4], SB_W, HEAD_SCALE * LOG2E, None, (BF16,), None), (o[5], SB_W, 1.0, None, (F32,), sb_prev[0]),
             (o[6], SB_W, 1.0, None, (F32,), sb_prev[1])]
    rq, rk, rv, rg, sq, sk, sv = _proj(x, g, w_ext, specs, rot)
    n_sb = sk.shape[0]
    sk4, sv4 = sk.reshape(n_sb, b, l, SB_W), sv.reshape(n_sb, b, l, SB_W)
    r3 = lambda a: a.reshape(b, l, a.shape[-1])
    if past is None:
        s0 = jnp.zeros((b, RET_QK_W, RET_DV), F32)
        tc = min(RET_TILE, l)
    else:
        s0 = past[0].astype(F32).reshape(b, RET_QK_W, RET_DV)
        tc = l
    ro, s_new = _retention(r3(rq), r3(rk), r3(rv), r3(rg), ret_gain, s0, tc)
    if past is None:
        so = _stick_breaking(r3(sq), sk4, sv4, n_sb - 1, sk4, sv4, n_sb - 1, prompt=True)
    else:
        (ck_all, layer), (cv_all, _) = past[1], past[2]
        ck = jnp.transpose(ck_all, (0, 1, 3, 4, 2)).reshape(ck_all.shape[0], b, SB_W, -1)
        cv = jnp.transpose(cv_all, (0, 1, 3, 4, 2)).reshape(cv_all.shape[0], b, SB_W, -1)
        padn = (-l) % LANES
        pad_own = lambda a: jnp.pad(a[-1:], ((0, 0), (0, 0), (0, padn), (0, 0)))
        so = _stick_breaking(r3(sq), pad_own(sk4), pad_own(sv4), 0, ck, cv, layer, prompt=False)
    return ro.reshape(t, RET_V_W), so.reshape(t, SB_W), s_new.reshape(b, RET_HEADS, RET_DK, RET_DV), (sk, sv)


def _cd_layer(x, b, l, pos0, g, w_ext, rel_bias, conv_w, conv_b, dt_bias, a_log, d_skip, ssd_gain, past,
              ffn_args):
    t = b * l
    o = [0]
    for wdt in (BAND_W, BAND_W, BAND_W, SSD_INNER, SSD_CONV_DIM, LANES):
        o.append(o[-1] + wdt)
    specs = [(o[0], BAND_W, HEAD_SCALE, None, (BF16,), None), (o[1], BAND_W, 1.0, None, (F32, BF16), None),
             (o[2], BAND_W, 1.0, None, (F32, BF16), None), (o[3], SSD_INNER, 1.0, None, (F32,), None),
             (o[4], SSD_CONV_DIM, 1.0, None, (F32,), None), (o[5], LANES, 1.0, None, (F32,), None)]
    bq, bk, bk16, bv, bv16, z, xbc, dt = _proj(x, g, w_ext, specs)
    r3 = lambda a: a.reshape(b, l, a.shape[-1])
    bk3, bv3, xbc3 = r3(bk), r3(bv), r3(xbc)
    if past is None:
        tq = min(BAND_TILE, l)
        span = BAND_WINDOW + tq
        nd = BAND_WINDOW // tq + 1
        bm = jnp.stack([_band_bias(rel_bias, d * tq, max(d * tq - BAND_WINDOW, 0), tq, span, span)
                        for d in range(nd)])
        bo = _band(r3(bq), r3(bk16), r3(bv16), bm, tq, span)
        keep = min(BAND_WINDOW, l)
        new_k, new_v = bk3[:, l - keep:], bv3[:, l - keep:]
        st0 = jnp.zeros((b, SSD_N, SSD_INNER), F32)
        cbuf = jnp.zeros((b, SUBLANES, SSD_CONV_DIM), F32)
        lp = l
    else:
        ck = past[0].reshape(b, -1, BAND_W)
        cv = past[1].reshape(b, -1, BAND_W)
        wlen = ck.shape[1]
        span = -(-(wlen + l) // LANES) * LANES
        padk = span - wlen - l
        k_all = jnp.concatenate([ck.astype(BF16), r3(bk16), jnp.zeros((b, padk, BAND_W), BF16)], axis=1)
        v_all = jnp.concatenate([cv.astype(BF16), r3(bv16), jnp.zeros((b, padk, BAND_W), BF16)], axis=1)
        bm = _band_bias(rel_bias, pos0, pos0 - wlen, l, span, wlen + l)[None]
        bo = _band(r3(bq), k_all, v_all, bm, l, span)
        new_k, new_v = bk3, bv3
        st0 = jnp.transpose(past[2].astype(F32), (0, 3, 1, 2)).reshape(b, SSD_N, SSD_INNER)
        cbuf = jnp.pad(past[3].astype(F32), ((0, 0), (SUBLANES - (SSD_CONV - 1), 0), (0, 0)))
        lp = -(-l // SSD_TILE) * SSD_TILE
    assert l >= SSD_CONV - 1
    conv_new = xbc3[:, l - (SSD_CONV - 1):]
    g2, win2, wout2, wo, fin_g = ffn_args
    wa, wb = wo[:BAND_W], wo[BAND_W:]
    ma = bo.reshape(t, BAND_W)
    if past is None and l % FUSED_TILE == 0:
        prm = _ssd_params(conv_w, conv_b, dt_bias, a_log, d_skip, ssd_gain)
        x, st_new = _ffn_ssd(x, g2, win2, wout2, ma, wa, wb, fin_g, xbc, z, dt, prm, st0, cbuf, l)
    else:
        padl = lp - l
        pad3 = lambda a: jnp.pad(a, ((0, 0), (0, padl), (0, 0))) if padl else a
        y, st_new = _ssd(pad3(xbc3), pad3(r3(z)), pad3(r3(dt)), conv_w, conv_b, dt_bias, a_log, d_skip, ssd_gain,
                         st0, cbuf, l)
        x = _ffn(x, g2, win2, wout2, mix=(ma, y[:, :l].reshape(t, SSD_INNER), wa, wb), final_g=fin_g)
    ssm_new = jnp.transpose(st_new.reshape(b, SSD_N, SSD_HEADS, SSD_P), (0, 2, 3, 1))
    outs = (new_k.reshape(b, -1, BAND_HEADS, BAND_DH), new_v.reshape(b, -1, BAND_HEADS, BAND_DH), ssm_new, conv_new)
    return x, outs


def _trunk(x3, pos0, wts, past):
    (norm_g, win, wout, final_g, w_ab, ret_gain, wo_ab, w_cd, rel_bias, conv_w, conv_b, dt_bias, a_log,
     d_skip, ssd_gain, wo_cd) = wts
    b, l, d = x3.shape
    depth = norm_g.shape[0]
    x = x3.reshape(b * l, d)
    outs = ([], None, None, [], [], [], [])
    sb_stack = ((), ())
    x = _ffn(x, norm_g[0, 0], win[0, 0], wout[0, 0])
    for layer in range(depth):
        j = layer // 2
        fin_g = final_g if layer + 1 == depth else None
        if layer % 2 == 0:
            pst = None if past is None else (past[0][j], (past[1], j), (past[2], j))
            ma, mb, r, sb_stack = _ab_layer(x, b, l, pos0, norm_g[layer, 1], w_ab[j], ret_gain[j], pst, sb_stack)
            outs[0].append(r)
            half = ma.shape[1]
            mix = (ma, mb, wo_ab[j][:half], wo_ab[j][half:])
            x = _ffn(x, norm_g[layer, 2], win[layer, 1], wout[layer, 1], mix=mix, final_g=fin_g)
        else:
            pst = None if past is None else (past[3][j], past[4][j], past[5][j], past[6][j])
            ffn_args = (norm_g[layer, 2], win[layer, 1], wout[layer, 1], wo_cd[j], fin_g)
            x, (k, v, s, cb) = _cd_layer(x, b, l, pos0, norm_g[layer, 1], w_cd[j], rel_bias[j], conv_w[j],
                                         conv_b[j], dt_bias[j], a_log[j], d_skip[j], ssd_gain[j], pst, ffn_args)
            outs[3].append(k)
            outs[4].append(v)
            outs[5].append(s)
            outs[6].append(cb)
        if layer + 1 < depth:
            x = _ffn(x, norm_g[layer + 1, 0], win[layer + 1, 0], wout[layer + 1, 0])
    sb_k, sb_v = (jnp.transpose(jnp.swapaxes(a.reshape(a.shape[0], b, l, SB_W), 2, 3)
                                .reshape(a.shape[0], b, SB_HEADS, SB_DH, l), (0, 1, 4, 2, 3)) for a in sb_stack)
    stacked = [jnp.stack(o) for o in outs if o is not None]
    return x.reshape(b, l, d), (stacked[0], sb_k, sb_v, *stacked[1:])


def kernel(x_prompt, x_sample, state_ret, cache_sb_k, cache_sb_v, cache_band_k, cache_band_v, state_ssm, state_conv, norm_g, ffn_w_in, ffn_w_out, final_g, w_in_ab, ret_gain, w_out_ab, w_in_cd, rel_bias, conv_w, conv_b, dt_bias, a_log, d_skip, ssd_gain, w_out_cd):
    win = ffn_w_in.astype(BF16)
    wout = ffn_w_out.astype(BF16)
    w_ab = w_in_ab.astype(BF16)
    n_main = 3 * BAND_W + SSD_INNER + SSD_CONV_DIM
    w_cd = jnp.concatenate([w_in_cd, jnp.zeros(w_in_cd.shape[:2] + (LANES - SSD_HEADS,), w_in_cd.dtype)],
                           axis=-1).astype(BF16)
    assert w_cd.shape[-1] == n_main + LANES
    wts = (norm_g, win, wout, final_g, w_ab, ret_gain, w_out_ab.astype(BF16), w_cd, rel_bias, conv_w, conv_b,
           dt_bias, a_log, d_skip, ssd_gain, w_out_cd.astype(BF16))
    y_p, outs_p = _trunk(x_prompt, 0, wts, None)
    past = (state_ret, cache_sb_k, cache_sb_v, cache_band_k, cache_band_v, state_ssm, state_conv)
    y_s, outs_s = _trunk(x_sample, cache_sb_k.shape[2], wts, past)
    return (y_p, y_s) + outs_p + outs_s
```

```python
import functools

import jax
import jax.numpy as jnp
import numpy as np
from jax import lax
from jax.experimental import pallas as pl
from jax.experimental.pallas import tpu as pltpu

F32 = jnp.float32
BF16 = jnp.bfloat16

CHUNK = 64
EPS = 1e-6
ROPE_BASE = 10000.0
RET_HEADS, RET_DK, RET_DV = 4, 64, 128
SB_HEADS, SB_DH = 8, 64
BAND_HEADS, BAND_DH, BAND_CHUNKS, REL_CLIP = 8, 64, 8, 128
BAND_WINDOW = BAND_CHUNKS * CHUNK
SSD_HEADS, SSD_P, SSD_GROUPS, SSD_N, SSD_CONV = 8, 64, 2, 128, 4
SSD_INNER = SSD_HEADS * SSD_P
SSD_CONV_DIM = SSD_INNER + 2 * SSD_GROUPS * SSD_N
RET_QK_W = RET_HEADS * RET_DK
RET_V_W = RET_HEADS * RET_DV
SB_W = SB_HEADS * SB_DH
BAND_W = BAND_HEADS * BAND_DH
HEAD_SCALE = 0.125

LANES = 128
SUBLANES = 8
VMEM_LIMIT_BYTES = 56 * 1024 * 1024

NEG_BIG = -1e30

ROW_TILE = 1024
PROJ_TILE = 512
FUSED_TILE = 512
FFN_CHUNK = 512
RET_TILE = 256
RET_SUB = 4
SB_TILE = 256
BAND_TILE = 128
BAND_GROUP = 2
BAND_SUB = 4
SSD_TILE = 128


def _cparams(*sem):
    return pltpu.CompilerParams(dimension_semantics=sem, vmem_limit_bytes=VMEM_LIMIT_BYTES)


def _rms(x, g):
    return x * lax.rsqrt(jnp.mean(x * x, axis=-1, keepdims=True) + EPS) * g


def _silu(x):
    return x * jax.nn.sigmoid(x)


def _softplus(x):
    return jnp.maximum(x, 0.0) + jnp.log(1.0 + jnp.exp(-jnp.abs(x)))


def _dot(a, b):
    return jnp.dot(a, b, preferred_element_type=F32)


def _dot_nt(a, b):
    return lax.dot_general(a, b, (((1,), (1,)), ((), ())), preferred_element_type=F32)


def _dot_tn(a, b):
    return lax.dot_general(a, b, (((0,), (0,)), ((), ())), preferred_element_type=F32)


def _split_dot(x, m):
    hi = x.astype(BF16)
    lo = (x - hi.astype(F32)).astype(BF16)
    return _dot(hi, m) + _dot(lo, m)


def _lane_iota(shape):
    return lax.broadcasted_iota(jnp.int32, shape, len(shape) - 1)


def _ffn_kernel(*refs, f, has_mix, final):
    refs = list(refs)
    x_ref = refs.pop(0)
    if has_mix:
        ma_ref, mb_ref, wa_ref, wb_ref = refs[:4]
        refs = refs[4:]
    g_ref, win_ref, wout_ref = refs[:3]
    refs = refs[3:]
    if final:
        fg_ref = refs.pop(0)
    o_ref, x1_ref, a_ref = refs

    x = x_ref[...]
    if has_mix:
        x = x + _dot(ma_ref[...], wa_ref[...]) + _dot(mb_ref[...], wb_ref[...])
    o_ref[...] = _ffn_body(x, g_ref, win_ref, wout_ref, fg_ref if final else None, x1_ref, a_ref, f)


def _ffn_body(x, g_ref, win_ref, wout_ref, fg_ref, x1_ref, a_ref, f, side_jobs=()):
    x1_ref[...] = x
    h = _rms(x, g_ref[...]).astype(BF16)
    chunks = list(range(0, f, FFN_CHUNK))
    slots = {}
    for k, job in enumerate(side_jobs):
        slots.setdefault(k * (len(chunks) + 1) // len(side_jobs), []).append(job)
    for n, f0 in enumerate(chunks):
        for job in slots.get(n, ()):
            job()
        f1 = min(f0 + FFN_CHUNK, f)
        gate = _dot(h, win_ref[:, f0:f1])
        up = _dot(h, win_ref[:, f + f0:f + f1])
        a_ref[:, f0:f1] = (_silu(gate) * up).astype(BF16)
    for job in slots.get(len(chunks), ()):
        job()
    y = x1_ref[...] + 0.5 * _dot(a_ref[...], wout_ref[...])
    if fg_ref is not None:
        y = _rms(y, fg_ref[...])
    return y


def _ffn_ssd_kernel(*refs, f, final, tc, tiles_per_batch, seq_len):
    refs = list(refs)
    x_ref, ma_ref, wa_ref, wb_ref, g_ref, win_ref, wout_ref = refs[:7]
    refs = refs[7:]
    fg_ref = refs.pop(0) if final else None
    xbc_ref, z_ref, dt_ref = refs[:3]
    prm = refs[3:9]
    st0_ref, cbuf_ref, tri_ref, e_ref, o_ref, stn_ref, x1_ref, a_ref, st_scr, xwin, ybuf = refs[9:]
    i = pl.program_id(0)
    n = pl.num_programs(0) - 1
    tile = i % tiles_per_batch
    tm = x_ref.shape[0]

    @pl.when(i == 0)
    def _():
        ybuf[...] = jnp.zeros_like(ybuf)

    @pl.when((tile == 0) & (i < n))
    def _():
        st_scr[...] = st0_ref[0]
        xwin[0:SUBLANES, :] = cbuf_ref[0]

    def ssd_job(c):
        def job():
            rows = slice(c * tc, (c + 1) * tc)
            ybuf[rows, :] = _ssd_chunk(xbc_ref[rows, :], z_ref[rows, :], dt_ref[rows, :], tile * tm + c * tc,
                                       seq_len, prm, st_scr, xwin, tri_ref, e_ref)
        return job

    x = x_ref[...] + _dot(ma_ref[...], wa_ref[...]) + _dot(ybuf[...], wb_ref[...])
    o_ref[...] = _ffn_body(x, g_ref, win_ref, wout_ref, fg_ref, x1_ref, a_ref, f,
                           side_jobs=[ssd_job(c) for c in range(tm // tc)])

    @pl.when((tile == tiles_per_batch - 1) & (i < n))
    def _():
        stn_ref[0] = st_scr[...]


def _resident(shape):
    return pl.BlockSpec(shape, lambda *_: (0,) * len(shape), pipeline_mode=pl.Buffered(1))


def _ffn(x, g, win, wout, mix=None, final_g=None):
    t, d = x.shape
    f = wout.shape[0]
    tm = min(ROW_TILE, t)
    assert t % tm == 0 and f % LANES == 0
    row = lambda i: (i, 0)
    args = [x]
    specs = [pl.BlockSpec((tm, d), row)]
    if mix is not None:
        ma, mb, wa, wb = mix
        args += [ma, mb, wa, wb]
        specs += [pl.BlockSpec((tm, ma.shape[1]), row), pl.BlockSpec((tm, mb.shape[1]), row),
                  _resident(wa.shape), _resident(wb.shape)]
    args += [g.reshape(1, d), win, wout]
    specs += [_resident((1, d)), _resident(win.shape), _resident(wout.shape)]
    if final_g is not None:
        args.append(final_g.reshape(1, d))
        specs.append(_resident((1, d)))
    return pl.pallas_call(
        functools.partial(_ffn_kernel, f=f, has_mix=mix is not None, final=final_g is not None),
        grid=(t // tm,),
        in_specs=specs,
        out_specs=pl.BlockSpec((tm, d), row),
        out_shape=jax.ShapeDtypeStruct((t, d), F32),
        scratch_shapes=[pltpu.VMEM((tm, d), F32), pltpu.VMEM((tm, f), BF16)],
        compiler_params=_cparams("parallel"),
        name="ffn",
    )(*args)


def _ffn_ssd(x, g, win, wout, ma, wa, wb, final_g, xbc, z, dt, ssd_prm, st0, cbuf, seq_len):
    t, d = x.shape
    f = wout.shape[0]
    tm = FUSED_TILE
    assert t % tm == 0 and seq_len % tm == 0 and tm % SSD_TILE == 0
    n = t // tm
    tpb = seq_len // tm
    params, tri, expand = ssd_prm
    lag = lambda i: (jnp.maximum(i - 1, 0), 0)
    cur = lambda i: (jnp.minimum(i, n - 1), 0)
    bat = lambda i: (jnp.minimum(i, n - 1) // tpb, 0, 0)
    args = [x, ma, wa, wb, g.reshape(1, d), win, wout]
    specs = [pl.BlockSpec((tm, d), lag), pl.BlockSpec((tm, ma.shape[1]), lag), _resident(wa.shape),
             _resident(wb.shape), _resident((1, d)), _resident(win.shape), _resident(wout.shape)]
    if final_g is not None:
        args.append(final_g.reshape(1, d))
        specs.append(_resident((1, d)))
    args += [xbc, z, dt, *params, st0, cbuf, tri, expand]
    specs += [pl.BlockSpec((tm, xbc.shape[1]), cur), pl.BlockSpec((tm, z.shape[1]), cur),
              pl.BlockSpec((tm, dt.shape[1]), cur)] + [_resident(p.shape) for p in params]
    specs += [pl.BlockSpec((1,) + st0.shape[1:], bat), pl.BlockSpec((1,) + cbuf.shape[1:], bat),
              _resident(tri.shape), _resident(expand.shape)]
    return pl.pallas_call(
        functools.partial(_ffn_ssd_kernel, f=f, final=final_g is not None, tc=SSD_TILE, tiles_per_batch=tpb,
                          seq_len=seq_len),
        grid=(n + 1,),
        in_specs=specs,
        out_specs=[pl.BlockSpec((tm, d), lag), pl.BlockSpec((1,) + st0.shape[1:], bat)],
        out_shape=[jax.ShapeDtypeStruct((t, d), F32), jax.ShapeDtypeStruct(st0.shape, F32)],
        scratch_shapes=[pltpu.VMEM((tm, d), F32), pltpu.VMEM((tm, f), BF16), pltpu.VMEM(st0.shape[1:], F32),
                        pltpu.VMEM((SUBLANES + SSD_TILE, xbc.shape[1]), F32), pltpu.VMEM((tm, SSD_INNER), BF16)],
        compiler_params=_cparams("arbitrary"),
        name="ffn_ssd",
    )(*args)


def _proj_kernel(*refs, specs, has_rot):
    refs = list(refs)
    x_ref, g_ref, w_ref = refs[:3]
    refs = refs[3:]
    if has_rot:
        cos_ref, sin_ref = refs[:2]
        refs = refs[2:]
    prev_refs = [refs.pop(0) for s in specs if s[5]]
    h = _rms(x_ref[...], g_ref[...]).astype(BF16)
    for c0, width, scale, c_swap, dts, n_prev in specs:
        y = _dot(h, w_ref[:, c0:c0 + width])
        if c_swap:
            half = RET_DK // 2
            lane = _lane_iota((1, width))
            partner = jnp.where(lane % RET_DK < half, pltpu.roll(y, width - half, axis=1),
                                pltpu.roll(y, half, axis=1))
            y = y * cos_ref[...] + partner * sin_ref[...]
        if scale != 1.0:
            y = y * scale
        for n, _ in enumerate(dts):
            o_ref = refs.pop(0)
            if n == 0 and n_prev is not None:
                if n_prev:
                    o_ref[0:n_prev] = prev_refs.pop(0)[...]
                o_ref[n_prev] = y.astype(o_ref.dtype)
            else:
                o_ref[...] = y.astype(o_ref.dtype)


def _proj(x, g, w, specs, rot=None):
    t, d = x.shape
    tm = min(PROJ_TILE, t)
    assert t % tm == 0
    row = lambda i: (i, 0)
    row3 = lambda i: (0, i, 0)
    const2 = lambda i: (0, 0)
    args = [x, g.reshape(1, d), w]
    in_specs = [pl.BlockSpec((tm, d), row), pl.BlockSpec((1, d), const2), _resident(w.shape)]
    if rot is not None:
        cos, sin = rot
        nrot = cos.shape[0] // tm
        rot_map = lambda i: (i % nrot, 0)
        args += [cos, sin]
        in_specs += [pl.BlockSpec((tm, cos.shape[1]), rot_map), pl.BlockSpec((tm, sin.shape[1]), rot_map)]
    kspecs, out_specs, out_shape = [], [], []
    for c0, width, scale, c_swap, dts, prev in specs:
        n_prev = None
        if prev is not None:
            n_prev = 0 if prev is () else prev.shape[0]
            if n_prev:
                args.append(prev)
                in_specs.append(pl.BlockSpec((n_prev, tm, width), row3))
        kspecs.append((c0, width, scale, c_swap, dts, n_prev))
        for n, dt in enumerate(dts):
            if n == 0 and n_prev is not None:
                out_specs.append(pl.BlockSpec((n_prev + 1, tm, width), row3))
                out_shape.append(jax.ShapeDtypeStruct((n_prev + 1, t, width), dt))
            else:
                out_specs.append(pl.BlockSpec((tm, width), row))
                out_shape.append(jax.ShapeDtypeStruct((t, width), dt))
    return pl.pallas_call(
        functools.partial(_proj_kernel, specs=tuple(kspecs), has_rot=rot is not None),
        grid=(t // tm,),
        in_specs=in_specs,
        out_specs=out_specs,
        out_shape=out_shape,
        compiler_params=_cparams("parallel"),
        name="proj",
    )(*args)


def _ret_kernel(q_ref, k_ref, v_ref, rg_ref, gain_ref, s0_ref, dm_ref, qd_ref, kd_ref, cdec_ref,
                o_ref, sn_ref, s_scr, *, tc, sub):
    c = pl.program_id(1)

    @pl.when(c == 0)
    def _():
        s_scr[...] = s0_ref[0]

    lane = _lane_iota((1, RET_QK_W))
    heads = range(RET_HEADS)
    hms = [(lane >= h * RET_DK) & (lane < (h + 1) * RET_DK) for h in heads]
    sls = [slice(h * RET_DV, (h + 1) * RET_DV) for h in heads]
    for s in range(sub):
        rws = slice(s * tc, (s + 1) * tc)
        q = q_ref[0, rws, :]
        k = k_ref[0, rws, :]
        v = v_ref[0, rws, :]
        qs = (q.astype(F32) * qd_ref[...]).astype(BF16)
        ks = (k.astype(F32) * kd_ref[...]).astype(BF16)
        s_all = s_scr[...]
        s_bf = s_all.astype(BF16)
        upd = _dot_tn(ks, v)
        zero = jnp.zeros_like(q)
        atts = [_dot_nt(jnp.where(hm, q, zero), k) * dm_ref[h] for h, hm in zip(heads, hms)]
        cross = [_dot(jnp.where(hm, qs, zero), s_bf) for hm in hms]
        outs = [_dot(att.astype(BF16), v[:, sl]) + cr for att, sl, cr in zip(atts, sls, cross)]
        outs = [o * lax.rsqrt(jnp.mean(o * o, axis=-1, keepdims=True) + EPS) for o in outs]
        o = jnp.concatenate(outs, axis=1)
        o_ref[0, rws, :] = (o * gain_ref[...] * _silu(rg_ref[0, rws, :])).astype(o_ref.dtype)
        for h, sl in zip(heads, sls):
            rows = slice(h * RET_DK, (h + 1) * RET_DK)
            s_scr[rows, :] = s_all[rows, :] * cdec_ref[rows, :] + upd[rows, sl]

    @pl.when(c == pl.num_programs(1) - 1)
    def _():
        sn_ref[0] = s_scr[...]


def _retention(q, k, v, rg, gain, s0, tc):
    b, l, _ = q.shape
    assert l % tc == 0
    f32 = np.float32
    lg = np.log1p(-np.exp2(-5.0 - np.arange(RET_HEADS, dtype=f32))).astype(f32)
    idx = np.arange(tc, dtype=f32)
    diff = idx[:, None] - idx[None, :]
    dm = np.where(diff >= 0, np.exp(lg[:, None, None] * np.maximum(diff, 0.0)), 0.0).astype(f32)
    qd = np.repeat(np.exp(lg[None, :] * (idx[:, None] + 1.0)), RET_DK, axis=1).astype(f32)
    kd = np.repeat(np.exp(lg[None, :] * (tc - 1.0 - idx[:, None])), RET_DK, axis=1).astype(f32)
    cdec = np.broadcast_to(np.repeat(np.exp(lg * f32(tc)), RET_DK)[:, None], (RET_QK_W, RET_DV)).astype(f32)
    sub = RET_SUB if (l // tc) % RET_SUB == 0 else 1
    blk = lambda w: pl.BlockSpec((1, sub * tc, w), lambda i, c: (i, c, 0))
    const2 = lambda i, c: (0, 0)
    st_spec = pl.BlockSpec((1, RET_QK_W, RET_DV), lambda i, c: (i, 0, 0))
    return pl.pallas_call(
        functools.partial(_ret_kernel, tc=tc, sub=sub),
        grid=(b, l // (sub * tc)),
        in_specs=[blk(RET_QK_W), blk(RET_QK_W), blk(RET_V_W), blk(RET_V_W),
                  pl.BlockSpec((1, RET_V_W), const2), st_spec,
                  pl.BlockSpec((RET_HEADS, tc, tc), lambda i, c: (0, 0, 0)),
                  pl.BlockSpec((tc, RET_QK_W), const2), pl.BlockSpec((tc, RET_QK_W), const2),
                  pl.BlockSpec((RET_QK_W, RET_DV), const2)],
        out_specs=[blk(RET_V_W), st_spec],
        out_shape=[jax.ShapeDtypeStruct((b, l, RET_V_W), BF16),
                   jax.ShapeDtypeStruct((b, RET_QK_W, RET_DV), F32)],
        scratch_shapes=[pltpu.VMEM((RET_QK_W, RET_DV), F32)],
        compiler_params=_cparams("parallel", "arbitrary"),
        name="retention",
    )(q, k, v, rg, gain.reshape(1, RET_V_W), s0, dm, qd, kd, cdec)


def _cum_matrix(n):
    tri = (np.arange(n)[:, None] >= np.arange(n)[None, :]).astype(BF16)
    return np.concatenate([tri, tri], axis=0)


LOG2E = 1.4426950408889634
SB_UNDERFLOW = -160.0


def _sb_scores(qms, kbs, transposed):
    return [_dot(qm, kb) if transposed else _dot_nt(qm, kb) for qm, kb in zip(qms, kbs)]


def _sb_logs(zs, mask):
    nzs = [-z for z in zs]
    lgs = [jnp.minimum(nz, 0.0) - jnp.log2(1.0 + jnp.exp2(jnp.minimum(z, nz))) for z, nz in zip(zs, nzs)]
    if mask is not None:
        lgs = [jnp.where(mask, lg, 0.0) for lg in lgs]
    his = [lg.astype(BF16) for lg in lgs]
    return [jnp.concatenate([hi, (lg - hi.astype(F32)).astype(BF16)], axis=1) for lg, hi in zip(lgs, his)]


def _sb_weights(zs, incls, carries, mask):
    if carries is None:
        ws = [jnp.exp2(z + incl) for z, incl in zip(zs, incls)]
    else:
        ws = [jnp.exp2(z + incl + c) for z, incl, c in zip(zs, incls, carries)]
    if mask is not None:
        ws = [jnp.where(mask, w, 0.0) for w in ws]
    return [w.astype(BF16) for w in ws]


def _sb_values(ws, vbs, transposed):
    return [_dot_nt(w, vb) if transposed else _dot(w, vb) for w, vb in zip(ws, vbs)]


def _sb_block(qms, kbs, vbs, cum_m, carries, mask, transposed=False):
    zs = _sb_scores(qms, kbs, transposed)
    incls = [_dot(hl, cum_m) for hl in _sb_logs(zs, mask)]
    outs = _sb_values(_sb_weights(zs, incls, carries, mask), vbs, transposed)
    return outs, [incl[:, 0:1] for incl in incls]


SB_PAIRS = 2


def _sb_kernel(q_ref, kd_ref, vd_ref, kp_ref, vp_ref, md_ref, mp_ref, o_ref, *, tq, tkd, tk, blocks_per_q,
               n_past, past_t):
    i = pl.program_id(2)
    lane = _lane_iota((1, LANES))
    sls = [slice(p * LANES, (p + 1) * LANES) for p in range(SB_PAIRS)]
    nh = 2 * SB_PAIRS
    qms = []
    for sl in sls:
        q = q_ref[0, :, sl]
        zero = jnp.zeros_like(q)
        qms += [jnp.where(lane < SB_DH, q, zero), jnp.where(lane >= SB_DH, q, zero)]
    row = lax.broadcasted_iota(jnp.int32, (tq, tkd), 0)
    col = lax.broadcasted_iota(jnp.int32, (tq, tkd), 1)
    dmask = col < row
    nblk = i * blocks_per_q if n_past is None else n_past

    per_head = lambda xs: [x for x in xs for _ in range(2)]

    def past_kv(j):
        off = pl.multiple_of(j * tk, tk)
        if past_t:
            return (per_head([kp_ref[0, sl, pl.ds(off, tk)].astype(BF16) for sl in sls]),
                    per_head([vp_ref[0, sl, pl.ds(off, tk)].astype(BF16) for sl in sls]))
        return (per_head([kp_ref[0, pl.ds(off, tk), sl].astype(BF16) for sl in sls]),
                per_head([vp_ref[0, pl.ds(off, tk), sl].astype(BF16) for sl in sls]))

    j0 = jnp.asarray(nblk - 1, jnp.int32)
    kds = per_head([kd_ref[0, :, sl].astype(BF16) for sl in sls])
    vds = per_head([vd_ref[0, :, sl].astype(BF16) for sl in sls])
    kbs, vbs = past_kv(jnp.maximum(j0, 0))
    z_own, z_far = _sb_scores(qms, kds, False), _sb_scores(qms, kbs, past_t)
    l_own, l_far = _sb_logs(z_own, dmask), _sb_logs(z_far, None)
    in_own = [_dot(hl, md_ref[...]) for hl in l_own]
    in_far = [_dot(hl, mp_ref[...]) for hl in l_far]
    c_far = [jnp.where(j0 >= 0, incl[:, 0:1], NEG_BIG) for incl in in_own]
    w_own, w_far = _sb_weights(z_own, in_own, None, dmask), _sb_weights(z_far, in_far, c_far, None)
    accs = [a + d for a, d in zip(_sb_values(w_own, vds, False), _sb_values(w_far, vbs, past_t))]
    carries = [c + incl[:, 0:1] for c, incl in zip(c_far, in_far)]

    def live(cs):
        return jnp.max(functools.reduce(jnp.maximum, cs)) > SB_UNDERFLOW

    def cond(st):
        return (st[0] >= 0) & st[1]

    def body(st):
        cs = st[2 + nh:]
        kbs, vbs = past_kv(st[0])
        ds, ts = _sb_block(qms, kbs, vbs, mp_ref[...], cs, None, transposed=past_t)
        cs = [c + t for c, t in zip(cs, ts)]
        return (st[0] - 1, live(cs), *[a + d for a, d in zip(st[2:2 + nh], ds)], *cs)

    st = lax.while_loop(cond, body, (j0 - 1, live(carries), *accs, *carries))
    pairs = [jnp.where(lane < SB_DH, st[2 + 2 * p], st[3 + 2 * p]) for p in range(SB_PAIRS)]
    o_ref[0] = jnp.concatenate(pairs, axis=1).astype(o_ref.dtype)


def _stick_breaking(q, k_own, v_own, own_layer, k_past, v_past, past_layer, prompt):
    b, lq, w = q.shape
    wb = SB_PAIRS * LANES
    if prompt:
        tq = tkd = tk = min(SB_TILE, lq)
        blocks_per_q, n_past = 1, None
        lp = k_past.shape[2]
        pspec = pl.BlockSpec((None, 1, lp, wb), lambda i, p, j: (past_layer, i, 0, p))
    else:
        tq = lq
        tkd = k_own.shape[2]
        tk = SB_TILE
        lp = k_past.shape[3]
        assert lp % tk == 0
        blocks_per_q, n_past = 0, lp // tk
        pspec = pl.BlockSpec((None, 1, wb, lp), lambda i, p, j: (past_layer, i, p, 0))
    assert lq % tq == 0
    qspec = pl.BlockSpec((1, tq, wb), lambda i, p, j: (i, j, p))
    dspec = pl.BlockSpec((None, 1, tkd, wb), lambda i, p, j: (own_layer, i, j, p))
    md = _cum_matrix(tkd)
    mp = _cum_matrix(tk)
    const2 = lambda i, p, j: (0, 0)
    return pl.pallas_call(
        functools.partial(_sb_kernel, tq=tq, tkd=tkd, tk=tk, blocks_per_q=blocks_per_q, n_past=n_past,
                          past_t=not prompt),
        grid=(b, w // wb, lq // tq),
        in_specs=[qspec, dspec, dspec, pspec, pspec, pl.BlockSpec(md.shape, const2), pl.BlockSpec(mp.shape, const2)],
        out_specs=qspec,
        out_shape=jax.ShapeDtypeStruct((b, lq, w), BF16),
        compiler_params=_cparams("parallel", "parallel", "arbitrary"),
        name="stick_breaking",
    )(q, k_own, v_own, k_past, v_past, md, mp)


def _band_kernel(q_ref, k_ref, v_ref, bm_ref, o_ref, *, tq, span, sub):
    lane = _lane_iota((1, LANES))
    npair = BAND_W // LANES
    sls = [slice(pr * LANES, (pr + 1) * LANES) for pr in range(npair)]
    for s in range(sub):
        rows = slice(s * tq, (s + 1) * tq)
        tile = pl.program_id(1) * sub + s
        ws = pl.multiple_of(jnp.maximum(tile * tq - (span - tq), 0), LANES)
        pairs = []
        for g0 in range(0, npair, BAND_GROUP):
            group = range(g0, g0 + BAND_GROUP)
            scores = []
            for pr in group:
                kw = k_ref[0, pl.ds(ws, span), sls[pr]]
                q = q_ref[0, rows, sls[pr]]
                zero = jnp.zeros_like(q)
                q2 = jnp.concatenate([jnp.where(lane < BAND_DH, q, zero), jnp.where(lane >= BAND_DH, q, zero)],
                                     axis=0)
                bias2 = jnp.concatenate([bm_ref[0, s, 2 * pr], bm_ref[0, s, 2 * pr + 1]], axis=0)
                scores.append(_dot_nt(q2, kw) + bias2)
            probs = [jnp.exp(sc - jnp.max(sc, axis=-1, keepdims=True)) for sc in scores]
            for n, pr in enumerate(group):
                vw = v_ref[0, pl.ds(ws, span), sls[pr]]
                p = probs[n]
                out2 = _dot(p.astype(BF16), vw) / jnp.sum(p, axis=-1, keepdims=True)
                pairs.append(jnp.where(lane < BAND_DH, out2[:tq], out2[tq:]))
        o_ref[0, rows, :] = jnp.concatenate(pairs, axis=1).astype(o_ref.dtype)


def _band_bias(rel_bias, q0, k0, tq, span, n_keys):
    d0 = q0 - k0
    n = span + tq
    padw = n + abs(d0)
    tab = rel_bias.astype(F32)
    heads = tab.shape[0]
    ext = jnp.concatenate([jnp.broadcast_to(tab[:, :1], (heads, padw)), tab,
                           jnp.broadcast_to(tab[:, -1:], (heads, padw))], axis=1)
    base = padw + REL_CLIP + d0
    g = jnp.concatenate([jnp.flip(ext[:, base - span + 1:base + 1], axis=1),
                         jnp.flip(ext[:, base + 1:base + tq + 1], axis=1)], axis=1)
    toe = jnp.tile(g, (1, tq))[:, :tq * (n - 1)].reshape(heads, tq, n - 1)[:, :, :span]
    q_pos = q0 + np.arange(tq)[:, None]
    k_pos = k0 + np.arange(span)[None, :]
    qc, kc = q_pos // CHUNK, k_pos // CHUNK
    mask = (np.arange(span)[None, :] < n_keys) & (k_pos >= 0) & (kc <= qc) & (kc >= qc - BAND_CHUNKS)
    return jnp.where(jnp.asarray(mask)[None], toe, NEG_BIG)


def _band(q, k, v, bm, tq, span):
    b, lq, w = q.shape
    nd, heads = bm.shape[:2]
    lk = k.shape[1]
    sub = BAND_SUB if (lq // tq) % BAND_SUB == 0 else 1
    nsteps = -(-(nd - 1) // sub) + 1
    bm = jnp.stack([jnp.stack([bm[min(sub * j + s, nd - 1)] for s in range(sub)]) for j in range(nsteps)])
    qspec = pl.BlockSpec((1, sub * tq, w), lambda i, j: (i, j, 0))
    kspec = pl.BlockSpec((1, lk, w), lambda i, j: (i, 0, 0))
    bspec = pl.BlockSpec((1, sub, heads, tq, span), lambda i, j: (jnp.minimum(j, nsteps - 1), 0, 0, 0, 0))
    return pl.pallas_call(
        functools.partial(_band_kernel, tq=tq, span=span, sub=sub),
        grid=(b, lq // (sub * tq)),
        in_specs=[qspec, kspec, kspec, bspec],
        out_specs=qspec,
        out_shape=jax.ShapeDtypeStruct((b, lq, w), BF16),
        compiler_params=_cparams("parallel", "arbitrary"),
        name="band",
    )(q, k, v, bm)


def _ssd_kernel(xbc_ref, z_ref, dt_ref, *rest, tc, valid_len):
    prm = rest[:6]
    st0_ref, cbuf_ref, tri_ref, e_ref, y_ref, stn_ref, st_scr, xwin = rest[6:]
    c = pl.program_id(1)

    @pl.when(c == 0)
    def _():
        st_scr[...] = st0_ref[0]
        xwin[0:SUBLANES, :] = cbuf_ref[0]

    y_ref[0] = _ssd_chunk(xbc_ref[0], z_ref[0], dt_ref[0], c * tc, valid_len, prm, st_scr, xwin, tri_ref, e_ref)

    @pl.when(c == pl.num_programs(1) - 1)
    def _():
        stn_ref[0] = st_scr[...]


def _ssd_chunk(u, zv, dt_raw, row0, valid_len, prm, st_scr, xwin, tri_ref, e_ref):
    cw_ref, cb_ref, dtb_ref, alog_ref, dskip_ref, gain_ref = prm
    tc = u.shape[0]
    xwin[SUBLANES:SUBLANES + tc, :] = u
    acc = cb_ref[...] + cw_ref[SSD_CONV - 1:SSD_CONV, :] * u
    for i in range(SSD_CONV - 1):
        off = SUBLANES - (SSD_CONV - 1) + i
        acc = acc + cw_ref[i:i + 1, :] * xwin[off:off + tc, :]
    xwin[0:SUBLANES, :] = u[tc - SUBLANES:, :]
    xc = _silu(acc)
    xs = xc[:, :SSD_INNER]

    rowid = row0 + lax.broadcasted_iota(jnp.int32, (tc, LANES), 0)
    dt = jnp.where(rowid < valid_len, _softplus(dt_raw + dtb_ref[...]), 0.0)
    da = dt * (-jnp.exp(alog_ref[...]))
    cum = _split_dot_left(tri_ref[...], da)
    dt_b = _split_dot(dt, e_ref[...])
    cum_b = _split_dot(cum, e_ref[...])
    last_b = cum_b[tc - 1:tc, :]
    xd = xs * dt_b
    xw = (xd * jnp.exp(last_b - cum_b)).astype(BF16)
    xd_bf = xd.astype(BF16)
    cum_t = cum.T

    lane = _lane_iota((1, LANES))
    ri = lax.broadcasted_iota(jnp.int32, (tc, tc), 0)
    ci = lax.broadcasted_iota(jnp.int32, (tc, tc), 1)
    causal = ri >= ci
    heads_per_group = SSD_HEADS // SSD_GROUPS
    gw = heads_per_group * SSD_P
    gls = [slice(g * gw, (g + 1) * gw) for g in range(SSD_GROUPS)]
    bgs = [xc[:, SSD_INNER + g * SSD_N:SSD_INNER + (g + 1) * SSD_N].astype(BF16) for g in range(SSD_GROUPS)]
    cgs = [xc[:, SSD_INNER + (SSD_GROUPS + g) * SSD_N:SSD_INNER + (SSD_GROUPS + g + 1) * SSD_N].astype(BF16)
           for g in range(SSD_GROUPS)]
    cbs = [_dot_nt(cg, bg) for cg, bg in zip(cgs, bgs)]
    y_inter = jnp.concatenate([_dot(cg, st_scr[:, gl].astype(BF16)) for cg, gl in zip(cgs, gls)], axis=1)
    s_upd = [_dot_tn(bg, xw[:, gl]) for bg, gl in zip(bgs, gls)]
    atts = []
    for pidx in range(SSD_HEADS // 2):
        cpair = cum_b[:, pidx * LANES:(pidx + 1) * LANES]
        rolled = pltpu.roll(cpair, SSD_P, axis=1)
        for hh in range(2):
            h = 2 * pidx + hh
            colv = jnp.where((lane >= SSD_P) if hh else (lane < SSD_P), cpair, rolled)
            seg = jnp.exp(jnp.where(causal, colv - cum_t[h:h + 1, :], NEG_BIG))
            atts.append((cbs[h // heads_per_group] * seg).astype(BF16))
    pairs = []
    for pidx in range(SSD_HEADS // 2):
        xdp = xd_bf[:, pidx * LANES:(pidx + 1) * LANES]
        pairs.append(jnp.where(lane < SSD_P, _dot(atts[2 * pidx], xdp), _dot(atts[2 * pidx + 1], xdp)))
    y_intra = jnp.concatenate(pairs, axis=1)
    for gl, upd in zip(gls, s_upd):
        st_scr[:, gl] = st_scr[:, gl] * jnp.exp(last_b[:, gl]) + upd
    y = y_intra + y_inter * jnp.exp(cum_b) + dskip_ref[...] * xs
    return _rms(y * _silu(zv), gain_ref[...]).astype(BF16)


def _split_dot_left(m, x):
    hi = x.astype(BF16)
    lo = (x - hi.astype(F32)).astype(BF16)
    return _dot(m, hi) + _dot(m, lo)


def _ssd_params(conv_w, conv_b, dt_bias, a_log, d_skip, gain):
    pad = LANES - SSD_HEADS
    tri = (np.arange(SSD_TILE)[:, None] >= np.arange(SSD_TILE)[None, :]).astype(BF16)
    expand = (np.arange(LANES)[:, None] == (np.arange(SSD_INNER) // SSD_P)[None, :]).astype(BF16)
    params = [conv_w, conv_b.reshape(1, -1),
              jnp.pad(dt_bias.astype(F32), (0, pad)).reshape(1, LANES),
              jnp.pad(a_log.astype(F32), (0, pad)).reshape(1, LANES),
              jnp.repeat(d_skip.astype(F32), SSD_P).reshape(1, SSD_INNER),
              gain.reshape(1, SSD_INNER)]
    return params, tri, expand


def _ssd(xbc, z, dt, conv_w, conv_b, dt_bias, a_log, d_skip, gain, st0, cbuf, valid_len):
    b, l, _ = xbc.shape
    tc = SSD_TILE
    assert l % tc == 0
    params, tri, expand = _ssd_params(conv_w, conv_b, dt_bias, a_log, d_skip, gain)
    blk = lambda w: pl.BlockSpec((1, tc, w), lambda i, c: (i, c, 0))
    const2 = lambda i, c: (0, 0)
    cs = lambda a: pl.BlockSpec(a.shape, const2)
    st_spec = pl.BlockSpec((1, SSD_N, SSD_INNER), lambda i, c: (i, 0, 0))
    return pl.pallas_call(
        functools.partial(_ssd_kernel, tc=tc, valid_len=valid_len),
        grid=(b, l // tc),
        in_specs=[blk(SSD_CONV_DIM), blk(SSD_INNER), blk(LANES)] + [cs(p) for p in params]
                 + [st_spec, pl.BlockSpec((1, SUBLANES, SSD_CONV_DIM), lambda i, c: (i, 0, 0)), cs(tri), cs(expand)],
        out_specs=[blk(SSD_INNER), st_spec],
        out_shape=[jax.ShapeDtypeStruct((b, l, SSD_INNER), BF16),
                   jax.ShapeDtypeStruct((b, SSD_N, SSD_INNER), F32)],
        scratch_shapes=[pltpu.VMEM((SSD_N, SSD_INNER), F32), pltpu.VMEM((SUBLANES + tc, SSD_CONV_DIM), F32)],
        compiler_params=_cparams("parallel", "arbitrary"),
        name="ssd",
    )(xbc, z, dt, *params, st0, cbuf, tri, expand)


def _rot_tables(pos, rows):
    half = RET_DK // 2
    inv = np.power(np.float32(ROPE_BASE), -np.arange(half, dtype=np.float32) / np.float32(half))
    ang = pos.astype(np.float32)[:, None] * inv[None, :]
    cos = np.tile(np.cos(ang), (rows // pos.shape[0], 2 * RET_HEADS))
    sin = np.sin(ang)
    sin = np.tile(np.concatenate([-sin, sin], axis=1), (rows // pos.shape[0], RET_HEADS))
    return cos.astype(np.float32), sin.astype(np.float32)


def _ab_layer(x, b, l, pos0, g, w_ext, ret_gain, past, sb_prev):
    t = b * l
    rows = max(l, min(PROJ_TILE, t))
    rot = _rot_tables(pos0 + np.arange(l, dtype=np.int32), rows)
    o = [0]
    for wdt in (RET_QK_W, RET_QK_W, RET_V_W, RET_V_W, SB_W, SB_W, SB_W):
        o.append(o[-1] + wdt)
    specs = [(o[0], RET_QK_W, HEAD_SCALE, True, (BF16,), None),
             (o[1], RET_QK_W, 1.0, True, (BF16,), None),
             (o[2], RET_V_W, 1.0, None, (BF16,), None), (o[3], RET_V_W, 1.0, None, (F32,), None),
             (o[4], SB_W, HEAD_SCALE * LOG2E, None, (BF16,), None), (o[5], SB_W, 1.0, None, (F32,), sb_prev[0]),
             (o[6], SB_W, 1.0, None, (F32,), sb_prev[1])]
    rq, rk, rv, rg, sq, sk, sv = _proj(x, g, w_ext, specs, rot)
    n_sb = sk.shape[0]
    sk4, sv4 = sk.reshape(n_sb, b, l, SB_W), sv.reshape(n_sb, b, l, SB_W)
    r3 = lambda a: a.reshape(b, l, a.shape[-1])
    if past is None:
        s0 = jnp.zeros((b, RET_QK_W, RET_DV), F32)
        tc = min(RET_TILE, l)
    else:
        s0 = past[0].astype(F32).reshape(b, RET_QK_W, RET_DV)
        tc = l
    ro, s_new = _retention(r3(rq), r3(rk), r3(rv), r3(rg), ret_gain, s0, tc)
    if past is None:
        so = _stick_breaking(r3(sq), sk4, sv4, n_sb - 1, sk4, sv4, n_sb - 1, prompt=True)
    else:
        (ck_all, layer), (cv_all, _) = past[1], past[2]
        ck = jnp.transpose(ck_all, (0, 1, 3, 4, 2)).reshape(ck_all.shape[0], b, SB_W, -1)
        cv = jnp.transpose(cv_all, (0, 1, 3, 4, 2)).reshape(cv_all.shape[0], b, SB_W, -1)
        padn = (-l) % LANES
        pad_own = lambda a: jnp.pad(a[-1:], ((0, 0), (0, 0), (0, padn), (0, 0)))
        so = _stick_breaking(r3(sq), pad_own(sk4), pad_own(sv4), 0, ck, cv, layer, prompt=False)
    return ro.reshape(t, RET_V_W), so.reshape(t, SB_W), s_new.reshape(b, RET_HEADS, RET_DK, RET_DV), (sk, sv)


def _cd_layer(x, b, l, pos0, g, w_ext, rel_bias, conv_w, conv_b, dt_bias, a_log, d_skip, ssd_gain, past,
              ffn_args):
    t = b * l
    o = [0]
    for wdt in (BAND_W, BAND_W, BAND_W, SSD_INNER, SSD_CONV_DIM, LANES):
        o.append(o[-1] + wdt)
    specs = [(o[0], BAND_W, HEAD_SCALE, None, (BF16,), None), (o[1], BAND_W, 1.0, None, (F32, BF16), None),
             (o[2], BAND_W, 1.0, None, (F32, BF16), None), (o[3], SSD_INNER, 1.0, None, (F32,), None),
             (o[4], SSD_CONV_DIM, 1.0, None, (F32,), None), (o[5], LANES, 1.0, None, (F32,), None)]
    bq, bk, bk16, bv, bv16, z, xbc, dt = _proj(x, g, w_ext, specs)
    r3 = lambda a: a.reshape(b, l, a.shape[-1])
    bk3, bv3, xbc3 = r3(bk), r3(bv), r3(xbc)
    if past is None:
        tq = min(BAND_TILE, l)
        span = BAND_WINDOW + tq
        nd = BAND_WINDOW // tq + 1
        bm = jnp.stack([_band_bias(rel_bias, d * tq, max(d * tq - BAND_WINDOW, 0), tq, span, span)
                        for d in range(nd)])
        bo = _band(r3(bq), r3(bk16), r3(bv16), bm, tq, span)
        keep = min(BAND_WINDOW, l)
        new_k, new_v = bk3[:, l - keep:], bv3[:, l - keep:]
        st0 = jnp.zeros((b, SSD_N, SSD_INNER), F32)
        cbuf = jnp.zeros((b, SUBLANES, SSD_CONV_DIM), F32)
        lp = l
    else:
        ck = past[0].reshape(b, -1, BAND_W)
        cv = past[1].reshape(b, -1, BAND_W)
        wlen = ck.shape[1]
        span = -(-(wlen + l) // LANES) * LANES
        padk = span - wlen - l
        k_all = jnp.concatenate([ck.astype(BF16), r3(bk16), jnp.zeros((b, padk, BAND_W), BF16)], axis=1)
        v_all = jnp.concatenate([cv.astype(BF16), r3(bv16), jnp.zeros((b, padk, BAND_W), BF16)], axis=1)
        bm = _band_bias(rel_bias, pos0, pos0 - wlen, l, span, wlen + l)[None]
        bo = _band(r3(bq), k_all, v_all, bm, l, span)
        new_k, new_v = bk3, bv3
        st0 = jnp.transpose(past[2].astype(F32), (0, 3, 1, 2)).reshape(b, SSD_N, SSD_INNER)
        cbuf = jnp.pad(past[3].astype(F32), ((0, 0), (SUBLANES - (SSD_CONV - 1), 0), (0, 0)))
        lp = -(-l // SSD_TILE) * SSD_TILE
    assert l >= SSD_CONV - 1
    conv_new = xbc3[:, l - (SSD_CONV - 1):]
    g2, win2, wout2, wo, fin_g = ffn_args
    wa, wb = wo[:BAND_W], wo[BAND_W:]
    ma = bo.reshape(t, BAND_W)
    if past is None and l % FUSED_TILE == 0:
        prm = _ssd_params(conv_w, conv_b, dt_bias, a_log, d_skip, ssd_gain)
        x, st_new = _ffn_ssd(x, g2, win2, wout2, ma, wa, wb, fin_g, xbc, z, dt, prm, st0, cbuf, l)
    else:
        padl = lp - l
        pad3 = lambda a: jnp.pad(a, ((0, 0), (0, padl), (0, 0))) if padl else a
        y, st_new = _ssd(pad3(xbc3), pad3(r3(z)), pad3(r3(dt)), conv_w, conv_b, dt_bias, a_log, d_skip, ssd_gain,
                         st0, cbuf, l)
        x = _ffn(x, g2, win2, wout2, mix=(ma, y[:, :l].reshape(t, SSD_INNER), wa, wb), final_g=fin_g)
    ssm_new = jnp.transpose(st_new.reshape(b, SSD_N, SSD_HEADS, SSD_P), (0, 2, 3, 1))
    outs = (new_k.reshape(b, -1, BAND_HEADS, BAND_DH), new_v.reshape(b, -1, BAND_HEADS, BAND_DH), ssm_new, conv_new)
    return x, outs


def _trunk(x3, pos0, wts, past):
    (norm_g, win, wout, final_g, w_ab, ret_gain, wo_ab, w_cd, rel_bias, conv_w, conv_b, dt_bias, a_log,
     d_skip, ssd_gain, wo_cd) = wts
    b, l, d = x3.shape
    depth = norm_g.shape[0]
    x = x3.reshape(b * l, d)
    outs = ([], None, None, [], [], [], [])
    sb_stack = ((), ())
    x = _ffn(x, norm_g[0, 0], win[0, 0], wout[0, 0])
    for layer in range(depth):
        j = layer // 2
        fin_g = final_g if layer + 1 == depth else None
        if layer % 2 == 0:
            pst = None if past is None else (past[0][j], (past[1], j), (past[2], j))
            ma, mb, r, sb_stack = _ab_layer(x, b, l, pos0, norm_g[layer, 1], w_ab[j], ret_gain[j], pst, sb_stack)
            outs[0].append(r)
            half = ma.shape[1]
            mix = (ma, mb, wo_ab[j][:half], wo_ab[j][half:])
            x = _ffn(x, norm_g[layer, 2], win[layer, 1], wout[layer, 1], mix=mix, final_g=fin_g)
        else:
            pst = None if past is None else (past[3][j], past[4][j], past[5][j], past[6][j])
            ffn_args = (norm_g[layer, 2], win[layer, 1], wout[layer, 1], wo_cd[j], fin_g)
            x, (k, v, s, cb) = _cd_layer(x, b, l, pos0, norm_g[layer, 1], w_cd[j], rel_bias[j], conv_w[j],
                                         conv_b[j], dt_bias[j], a_log[j], d_skip[j], ssd_gain[j], pst, ffn_args)
            outs[3].append(k)
            outs[4].append(v)
            outs[5].append(s)
            outs[6].append(cb)
        if layer + 1 < depth:
            x = _ffn(x, norm_g[layer + 1, 0], win[layer + 1, 0], wout[layer + 1, 0])
    sb_k, sb_v = (jnp.transpose(jnp.swapaxes(a.reshape(a.shape[0], b, l, SB_W), 2, 3)
                                .reshape(a.shape[0], b, SB_HEADS, SB_DH, l), (0, 1, 4, 2, 3)) for a in sb_stack)
    stacked = [jnp.stack(o) for o in outs if o is not None]
    return x.reshape(b, l, d), (stacked[0], sb_k, sb_v, *stacked[1:])


def kernel(x_prompt, x_sample, state_ret, cache_sb_k, cache_sb_v, cache_band_k, cache_band_v, state_ssm, state_conv, norm_g, ffn_w_in, ffn_w_out, final_g, w_in_ab, ret_gain, w_out_ab, w_in_cd, rel_bias, conv_w, conv_b, dt_bias, a_log, d_skip, ssd_gain, w_out_cd):
    win = ffn_w_in.astype(BF16)
    wout = ffn_w_out.astype(BF16)
    w_ab = w_in_ab.astype(BF16)
    n_main = 3 * BAND_W + SSD_INNER + SSD_CONV_DIM
    w_cd = jnp.concatenate([w_in_cd, jnp.zeros(w_in_cd.shape[:2] + (LANES - SSD_HEADS,), w_in_cd.dtype)],
                           axis=-1).astype(BF16)
    assert w_cd.shape[-1] == n_main + LANES
    wts = (norm_g, win, wout, final_g, w_ab, ret_gain, w_out_ab.astype(BF16), w_cd, rel_bias, conv_w, conv_b,
           dt_bias, a_log, d_skip, ssd_gain, w_out_cd.astype(BF16))
    y_p, outs_p = _trunk(x_prompt, 0, wts, None)
    past = (state_ret, cache_sb_k, cache_sb_v, cache_band_k, cache_band_v, state_ssm, state_conv)
    y_s, outs_s = _trunk(x_sample, cache_sb_k.shape[2], wts, past)
    return (y_p, y_s) + outs_p + outs_s
```
